```python
import math
import jax, jax.numpy as jnp
from jax import lax
import numpy as np

D_MODEL = 1024
BATCH = 16
SEQ = 2048
DEPTH = 4

CHUNK = 64
Q_BLOCK = 128
D_FF = 2752
DIFF_HEADS = 4
DIFF_QK_DIM = 32
DIFF_V_DIM = 2 * DIFF_QK_DIM
GLA_HEADS = 4
GLA_K_DIM = 64
GLA_V_DIM = 128
GLA_GATE_RANK = 16
GLA_TAU = 16.0
FOX_HEADS = 4
FOX_HEAD_DIM = 64
DIFF_WIDTH = DIFF_HEADS * DIFF_V_DIM
GLA_WIDTH = GLA_HEADS * GLA_V_DIM
FOX_WIDTH = FOX_HEADS * FOX_HEAD_DIM
MIX_WIDTH = DIFF_WIDTH + GLA_WIDTH + FOX_WIDTH
IN_SPLITS = (
    DIFF_HEADS * 2 * DIFF_QK_DIM,
    DIFF_HEADS * 2 * DIFF_QK_DIM,
    DIFF_WIDTH,
    GLA_HEADS * GLA_K_DIM,
    GLA_HEADS * GLA_K_DIM,
    GLA_WIDTH,
    GLA_WIDTH,
    GLA_GATE_RANK,
    FOX_WIDTH,
    FOX_WIDTH,
    FOX_WIDTH,
    FOX_HEADS,
)
IN_WIDTH = sum(IN_SPLITS)
NORM_EPS = 1e-6

kernel_name = "hybrid_diff_gla_fox_macaron"


def rms_norm(x, g):
    xf = x.astype(jnp.float32)
    y = xf * lax.rsqrt(jnp.mean(xf * xf, axis=-1, keepdims=True) + NORM_EPS)
    return (y * g.astype(jnp.float32)).astype(x.dtype)


def swiglu(h, w13, w2):
    gu = jnp.einsum('bsd,df->bsf', h, w13)
    gate, up = jnp.split(gu, 2, axis=-1)
    return jnp.einsum('bsf,fd->bsd', jax.nn.silu(gate) * up, w2)


def split_cols(p):
    out = []
    off = 0
    for w in IN_SPLITS:
        out.append(p[..., off:off + w])
        off += w
    return out


def diff_attention(q1, q2, k1, k2, v, lam):
    S = q1.shape[1]
    slopes = 2.0 ** (-8.0 * jnp.arange(1, DIFF_HEADS + 1, dtype=jnp.float32) / DIFF_HEADS)
    scale = DIFF_QK_DIM ** -0.5
    outs = []
    for blk in range(S // Q_BLOCK):
        q0 = blk * Q_BLOCK
        kend = q0 + Q_BLOCK
        qpos = jnp.arange(q0, kend)
        kpos = jnp.arange(kend)
        allowed = (kpos[None, :] // CHUNK) <= (qpos[:, None] // CHUNK)
        dist = jnp.abs(qpos[:, None] - kpos[None, :]).astype(jnp.float32)
        bias = jnp.where(allowed[None], -slopes[:, None, None] * dist[None], -jnp.inf)
        s1 = jnp.einsum('bqhd,bkhd->bhqk', q1[:, q0:kend], k1[:, :kend]) * scale + bias
        s2 = jnp.einsum('bqhd,bkhd->bhqk', q2[:, q0:kend], k2[:, :kend]) * scale + bias
        p = jax.nn.softmax(s1, axis=-1) - lam * jax.nn.softmax(s2, axis=-1)
        outs.append(jnp.einsum('bhqk,bkhd->bqhd', p, v[:, :kend]))
    return jnp.concatenate(outs, axis=1)


def fox_attention(q, k, v, log_f):
    S = q.shape[1]
    scale = FOX_HEAD_DIM ** -0.5
    F = jnp.cumsum(log_f, axis=1).transpose(0, 2, 1)
    outs = []
    for blk in range(S // Q_BLOCK):
        q0 = blk * Q_BLOCK
        kend = q0 + Q_BLOCK
        qpos = jnp.arange(q0, kend)
        kpos = jnp.arange(kend)
        causal = kpos[None, :] <= qpos[:, None]
        s = jnp.einsum('bqhd,bkhd->bhqk', q[:, q0:kend], k[:, :kend]) * scale
        s = s + F[:, :, q0:kend, None] - F[:, :, None, :kend]
        s = jnp.where(causal, s, -jnp.inf)
        p = jax.nn.softmax(s, axis=-1)
        outs.append(jnp.einsum('bhqk,bkhd->bqhd', p, v[:, :kend]))
    return jnp.concatenate(outs, axis=1)


def gla(q, k, v, log_alpha):
    B, S, H, dk = q.shape
    dv = v.shape[-1]
    n = S // CHUNK

    def to_chunks(t):
        return t.reshape(B, n, CHUNK, H, t.shape[-1]).transpose(1, 0, 3, 2, 4)

    tri = jnp.tril(jnp.ones((CHUNK, CHUNK), dtype=bool))[:, :, None]

    def step(state, inp):
        qc, kc, vc, ac = inp
        b = jnp.cumsum(ac, axis=2)
        o_inter = jnp.einsum('bhtd,bhde->bhte', qc * jnp.exp(b), state)
        diff = b[:, :, :, None, :] - b[:, :, None, :, :]
        decay = jnp.where(tri, jnp.exp(jnp.where(tri, diff, 0.0)), 0.0)
        attn = jnp.einsum('bhtd,bhsd,bhtsd->bhts', qc, kc, decay)
        o_intra = jnp.einsum('bhts,bhse->bhte', attn, vc)
        b_last = b[:, :, -1:, :]
        k_dec = kc * jnp.exp(b_last - b)
        new_state = state * jnp.exp(b_last[:, :, 0, :])[..., None] + jnp.einsum('bhsd,bhse->bhde', k_dec, vc)
        return new_state, o_inter + o_intra

    state0 = jnp.zeros((B, H, dk, dv), jnp.float32)
    _, o = lax.scan(step, state0, (to_chunks(q), to_chunks(k), to_chunks(v), to_chunks(log_alpha)))
    return o.transpose(1, 0, 3, 2, 4).reshape(B, S, H, dv)


def hybrid_mix(h, w_in, w_out, diff_q_norm, diff_k_norm, diff_lambda, diff_out_norm,
               gla_alpha_w2, gla_alpha_b, gla_out_norm, fox_q_norm, fox_k_norm,
               fox_f_bias, layer_idx):
    B, S, _ = h.shape
    p = jnp.einsum('bsd,de->bse', h, w_in).astype(jnp.float32)
    (d_q, d_k, d_v, g_q, g_k, g_v, g_r, g_a, f_q, f_k, f_v, f_f) = split_cols(p)

    d_q = rms_norm(d_q.reshape(B, S, DIFF_HEADS, 2, DIFF_QK_DIM), diff_q_norm)
    d_k = rms_norm(d_k.reshape(B, S, DIFF_HEADS, 2, DIFF_QK_DIM), diff_k_norm)
    lam_init = 0.8 - 0.6 * math.exp(-0.3 * layer_idx)
    lp = diff_lambda.astype(jnp.float32)
    lam = jnp.exp(jnp.sum(lp[0] * lp[1])) - jnp.exp(jnp.sum(lp[2] * lp[3])) + lam_init
    a = diff_attention(d_q[..., 0, :], d_q[..., 1, :], d_k[..., 0, :], d_k[..., 1, :],
                       d_v.reshape(B, S, DIFF_HEADS, DIFF_V_DIM), lam)
    a = rms_norm(a, diff_out_norm) * (1.0 - lam_init)

    z = jnp.einsum('bsr,rk->bsk', g_a, gla_alpha_w2.astype(jnp.float32)) + gla_alpha_b.astype(jnp.float32)
    log_alpha = (jax.nn.log_sigmoid(z) / GLA_TAU).reshape(B, S, GLA_HEADS, GLA_K_DIM)
    g = gla(g_q.reshape(B, S, GLA_HEADS, GLA_K_DIM) * (GLA_K_DIM ** -0.5),
            g_k.reshape(B, S, GLA_HEADS, GLA_K_DIM),
            g_v.reshape(B, S, GLA_HEADS, GLA_V_DIM), log_alpha)
    g = rms_norm(g, gla_out_norm).reshape(B, S, GLA_WIDTH) * jax.nn.silu(g_r)

    f_q = rms_norm(f_q.reshape(B, S, FOX_HEADS, FOX_HEAD_DIM), fox_q_norm)
    f_k = rms_norm(f_k.reshape(B, S, FOX_HEADS, FOX_HEAD_DIM), fox_k_norm)
    log_f = jax.nn.log_sigmoid(f_f + fox_f_bias.astype(jnp.float32))
    c = fox_attention(f_q, f_k, f_v.reshape(B, S, FOX_HEADS, FOX_HEAD_DIM), log_f)

    o = jnp.concatenate([a.reshape(B, S, DIFF_WIDTH), g, c.reshape(B, S, FOX_WIDTH)], axis=-1)
    return jnp.einsum('bse,ed->bsd', o.astype(h.dtype), w_out)


def setup_inputs(seed: int = 0) -> dict:
    key = jax.random.key(seed)
    ks = jax.random.split(key, 24)
    L = DEPTH

    def nrm(k, shape, scale):
        return jax.random.normal(k, shape, jnp.float32) * scale

    def gain(k, shape):
        return 1.0 + 0.1 * jax.random.normal(k, shape, jnp.float32)

    return {
        "x": jax.random.normal(ks[0], (BATCH, SEQ, D_MODEL), jnp.float32),
        "ffn1_norm": gain(ks[1], (L, D_MODEL)),
        "ffn1_w13": nrm(ks[2], (L, D_MODEL, 2 * D_FF), D_MODEL ** -0.5),
        "ffn1_w2": nrm(ks[3], (L, D_FF, D_MODEL), D_FF ** -0.5),
        "mix_norm": gain(ks[4], (L, D_MODEL)),
        "w_in": nrm(ks[5], (L, D_MODEL, IN_WIDTH), D_MODEL ** -0.5),
        "w_out": nrm(ks[6], (L, MIX_WIDTH, D_MODEL), MIX_WIDTH ** -0.5),
        "diff_q_norm": gain(ks[7], (L, DIFF_QK_DIM)),
        "diff_k_norm": gain(ks[8], (L, DIFF_QK_DIM)),
        "diff_lambda": nrm(ks[9], (L, 4, DIFF_QK_DIM), 0.1),
        "diff_out_norm": gain(ks[10], (L, DIFF_V_DIM)),
        "gla_alpha_w2": nrm(ks[11], (L, GLA_GATE_RANK, GLA_HEADS * GLA_K_DIM), GLA_GATE_RANK ** -0.5),
        "gla_alpha_b": nrm(ks[12], (L, GLA_HEADS * GLA_K_DIM), 0.1),
        "gla_out_norm": gain(ks[13], (L, GLA_V_DIM)),
        "fox_q_norm": gain(ks[14], (L, FOX_HEAD_DIM)),
        "fox_k_norm": gain(ks[15], (L, FOX_HEAD_DIM)),
        "fox_f_bias": 1.0 + nrm(ks[16], (L, FOX_HEADS), 0.5),
        "ffn2_norm": gain(ks[17], (L, D_MODEL)),
        "ffn2_w13": nrm(ks[18], (L, D_MODEL, 2 * D_FF), D_MODEL ** -0.5),
        "ffn2_w2": nrm(ks[19], (L, D_FF, D_MODEL), D_FF ** -0.5),
    }


def reference(x, ffn1_norm, ffn1_w13, ffn1_w2, mix_norm, w_in, w_out, diff_q_norm,
              diff_k_norm, diff_lambda, diff_out_norm, gla_alpha_w2, gla_alpha_b,
              gla_out_norm, fox_q_norm, fox_k_norm, fox_f_bias, ffn2_norm, ffn2_w13,
              ffn2_w2):
    for i in range(DEPTH):
        x = x + 0.5 * swiglu(rms_norm(x, ffn1_norm[i]), ffn1_w13[i], ffn1_w2[i])
        x = x + hybrid_mix(rms_norm(x, mix_norm[i]), w_in[i], w_out[i], diff_q_norm[i],
                           diff_k_norm[i], diff_lambda[i], diff_out_norm[i],
                           gla_alpha_w2[i], gla_alpha_b[i], gla_out_norm[i],
                           fox_q_norm[i], fox_k_norm[i], fox_f_bias[i], i)
        x = x + 0.5 * swiglu(rms_norm(x, ffn2_norm[i]), ffn2_w13[i], ffn2_w2[i])
    return x
```

```python
import functools
import math

import numpy as np
import jax
import jax.numpy as jnp
from jax import lax
from jax.experimental import pallas as pl
from jax.experimental.pallas import tpu as pltpu

F32 = jnp.float32
BF16 = jnp.bfloat16

D_MODEL = 1024
D_FF = 2752
DEPTH = 4
CHUNK = 64
N_HEADS = 4
DIFF_QK_DIM = 32
HEAD_V = 64
GLA_K_DIM = 64
GLA_V_DIM = 128
GLA_GATE_RANK = 16
GLA_TAU = 16.0
NORM_EPS = 1e-6
IN_WIDTH = 3092

LANES = 128
FF_CHUNK = 256
D_FF_PAD = 2816
N_FF_CHUNKS = D_FF_PAD // FF_CHUNK
TOKEN_TILE = 512
ATTN_TILE = 256
GLA_TILE = 512
NEG = -1e30
VMEM_LIMIT = 56 * 1024 * 1024

C_DQ, C_DK, C_DV = 0, 256, 512
C_GQ, C_GK, C_GV, C_GR = 1024, 1280, 1536, 2048
C_FQ, C_FK, C_FV = 2560, 2816, 3072
C_GA = 3584
PROJ_WIDTH = 3712
PAD_V = N_HEADS * LANES
MIX_PAD = PAD_V + N_HEADS * GLA_V_DIM + PAD_V

NT_DIMS = (((1,), (1,)), ((), ()))
TN_DIMS = (((0,), (0,)), ((), ()))


def _cparams(*sem):
    return pltpu.CompilerParams(dimension_semantics=sem, vmem_limit_bytes=VMEM_LIMIT)


def _dot(a, b):
    return jnp.dot(a, b, preferred_element_type=F32)


def _dot_nt(a, b):
    return lax.dot_general(a, b, NT_DIMS, preferred_element_type=F32)


def _split2(x):
    hi = x.astype(BF16)
    lo = (x - hi.astype(F32)).astype(BF16)
    return hi, lo


def _split3(x):
    hi = x.astype(BF16)
    r = x - hi.astype(F32)
    mid = r.astype(BF16)
    lo = (r - mid.astype(F32)).astype(BF16)
    return hi, mid, lo


def _log_sigmoid(z):
    return jnp.minimum(z, 0.0) - jnp.log(1.0 + jnp.exp(-jnp.abs(z)))


def _rms(x, g):
    ms = jnp.mean(x * x, axis=-1, keepdims=True)
    return x * lax.rsqrt(ms + NORM_EPS) * g


def _ffn_kernel(x_ref, g_ref, wg_ref, wu_ref, w2_ref, o_ref, acc_ref):
    x = x_ref[...]
    h = _rms(x, g_ref[...]).astype(BF16)
    acc_ref[...] = jnp.zeros_like(acc_ref)

    def step(c, carry):
        gate = _dot(h, wg_ref[c])
        up = _dot(h, wu_ref[c])
        act = (gate * jax.nn.sigmoid(gate) * up).astype(BF16)
        acc_ref[...] += _dot(act, w2_ref[c])
        return carry

    lax.fori_loop(0, N_FF_CHUNKS, step, 0)
    o_ref[...] = x + 0.5 * acc_ref[...]


def _ffn(x2d, gain, wg, wu, w2):
    n = x2d.shape[0]
    tm = min(TOKEN_TILE, n)
    const3 = lambda i: (0, 0, 0)
    return pl.pallas_call(
        _ffn_kernel,
        grid=(n // tm,),
        in_specs=[
            pl.BlockSpec((tm, D_MODEL), lambda i: (i, 0)),
            pl.BlockSpec((1, D_MODEL), lambda i: (0, 0)),
            pl.BlockSpec((N_FF_CHUNKS, D_MODEL, FF_CHUNK), const3, pipeline_mode=pl.Buffered(1)),
            pl.BlockSpec((N_FF_CHUNKS, D_MODEL, FF_CHUNK), const3, pipeline_mode=pl.Buffered(1)),
            pl.BlockSpec((N_FF_CHUNKS, FF_CHUNK, D_MODEL), const3, pipeline_mode=pl.Buffered(1)),
        ],
        out_specs=pl.BlockSpec((tm, D_MODEL), lambda i: (i, 0)),
        out_shape=jax.ShapeDtypeStruct((n, D_MODEL), F32),
        scratch_shapes=[pltpu.VMEM((tm, D_MODEL), F32)],
        compiler_params=_cparams("parallel"),
        name="ffn",
    )(x2d, gain, wg, wu, w2)


def _group_norm(t, grp, inv_d, gain):
    hi, lo = _split2(t * t)
    ms = (_dot(hi, grp) + _dot(lo, grp)) * inv_d
    return t * lax.rsqrt(ms + NORM_EPS) * gain


def _proj_kernel(x_ref, g_ref, w_ref, wf_ref, qg_ref, grp32_ref, grp64_ref, aw2_ref, ab_ref,
                 fb_ref, vone_ref,
                 dq_ref, dk_ref, dv_ref, gq_ref, gk_ref, gv_ref, gr_ref, la_ref,
                 fq_ref, fk_ref, fv_ref, lf_ref):
    h = _rms(x_ref[...], g_ref[...]).astype(BF16)

    def proj(lo, width):
        return _dot(h, w_ref[:, lo:lo + width])

    grp32 = grp32_ref[...]
    grp64 = grp64_ref[...]
    vone = vone_ref[...]
    dq_ref[...] = _group_norm(proj(C_DQ, 256), grp32, 1.0 / DIFF_QK_DIM, qg_ref[0:1]).astype(BF16)
    dk_ref[...] = _group_norm(proj(C_DK, 256), grp32, 1.0 / DIFF_QK_DIM, qg_ref[1:2]).astype(BF16)
    dv_ref[...] = (proj(C_DV, PAD_V) + vone).astype(BF16)
    gq_ref[...] = proj(C_GQ, 256) * (GLA_K_DIM ** -0.5)
    gk_ref[...] = proj(C_GK, 256)
    gv_ref[...] = proj(C_GV, 512).astype(BF16)
    gr_ref[...] = proj(C_GR, 512)
    fq_ref[...] = _group_norm(proj(C_FQ, 256), grp64, 1.0 / HEAD_V, qg_ref[2:3]).astype(BF16)
    fk_ref[...] = _group_norm(proj(C_FK, 256), grp64, 1.0 / HEAD_V, qg_ref[3:4]).astype(BF16)
    fv_ref[...] = (proj(C_FV, PAD_V) + vone).astype(BF16)

    ga_hi, ga_lo = _split2(proj(C_GA, LANES))
    w_hi, w_lo = _split2(aw2_ref[...])
    z = _dot(ga_hi, w_hi) + _dot(ga_hi, w_lo) + _dot(ga_lo, w_hi) + ab_ref[...]
    la_ref[...] = _log_sigmoid(z) * (1.0 / GLA_TAU)

    ff_t = _dot_nt(wf_ref[...], h) + fb_ref[...]
    lf_ref[...] = _log_sigmoid(ff_t)


def _proj(x2d, gain, w, wf, qg, grp32, grp64, aw2, ab, fb, vone):
    n = x2d.shape[0]
    tm = min(TOKEN_TILE, n)
    row = lambda i: (i, 0)
    const = lambda i: (0, 0)

    def full(a):
        return pl.BlockSpec(a.shape, const)

    widths = [(256, BF16), (256, BF16), (PAD_V, BF16), (256, F32), (256, F32), (512, BF16),
              (512, F32), (256, F32), (256, BF16), (256, BF16), (PAD_V, BF16)]
    out_shape = [jax.ShapeDtypeStruct((n, w_), dt) for w_, dt in widths]
    out_specs = [pl.BlockSpec((tm, w_), row) for w_, _ in widths]
    out_shape.append(jax.ShapeDtypeStruct((8, n), F32))
    out_specs.append(pl.BlockSpec((8, tm), lambda i: (0, i)))
    return pl.pallas_call(
        _proj_kernel,
        grid=(n // tm,),
        in_specs=[pl.BlockSpec((tm, D_MODEL), row), full(gain), full(w), full(wf), full(qg),
                  full(grp32), full(grp64), full(aw2), full(ab), full(fb), full(vone)],
        out_specs=out_specs,
        out_shape=out_shape,
        compiler_params=_cparams("parallel"),
        name="proj_in",
    )(x2d, gain, w, wf, qg, grp32, grp64, aw2, ab, fb, vone)


def _attn_block(qs_ref, k, v_of, bias_of, m_ref, acc_ref, n_stack):
    for e in range(n_stack):
        s = _dot_nt(qs_ref[e], k) + bias_of(e)
        m_old = m_ref[e]
        m_new = jnp.maximum(m_old, jnp.max(s, axis=-1, keepdims=True))
        alpha = jnp.exp(m_old - m_new)
        p = jnp.exp(s - m_new).astype(BF16)
        acc_ref[e] = alpha * acc_ref[e] + _dot(p, v_of(e))
        m_ref[e] = m_new


def _stack_queries(q, qs_ref, n_stack):
    width = q.shape[-1] // n_stack
    col = lax.broadcasted_iota(jnp.int32, (1, q.shape[-1]), 1)
    for e in range(n_stack):
        qs_ref[e] = jnp.where(col // width == e, q, jnp.zeros_like(q))


DIFF_SLOPES = tuple(2.0 ** (-8.0 * (h + 1) / N_HEADS) for h in range(N_HEADS))


def _diff_diag_table(t):
    tl = np.arange(t)[:, None]
    sl = np.arange(t)[None, :]
    allowed = (sl // CHUNK) <= (tl // CHUNK)
    tabs = [np.where(allowed, sl_ * (tl - np.abs(tl - sl)), NEG) for sl_ in DIFF_SLOPES]
    return np.stack(tabs).astype(np.float32)


def _diff_kernel(lam_init, q_ref, k_ref, v_ref, tab_ref, lp_ref, og_ref, o_ref,
                 qs_ref, m_ref, acc_ref):
    t = q_ref.shape[0]
    n_stack = 2 * N_HEADS
    i = pl.program_id(1)
    _stack_queries(q_ref[...], qs_ref, n_stack)
    m_ref[...] = jnp.full_like(m_ref, NEG)
    acc_ref[...] = jnp.zeros_like(acc_ref)
    q0 = i * t

    def v_block(koff):
        return lambda e: v_ref[pl.ds(koff, t), (e // 2) * LANES:(e // 2 + 1) * LANES]

    def past(j, carry):
        koff = pl.multiple_of(j * t, t)
        rel = (lax.broadcasted_iota(jnp.int32, (1, t), 1) + (koff - q0)).astype(F32)
        _attn_block(qs_ref, k_ref[pl.ds(koff, t), :], v_block(koff),
                    lambda e: DIFF_SLOPES[e // 2] * rel, m_ref, acc_ref, n_stack)
        return carry

    lax.fori_loop(0, i, past, 0)
    koff = pl.multiple_of(q0, t)
    _attn_block(qs_ref, k_ref[pl.ds(koff, t), :], v_block(koff),
                lambda e: tab_ref[e // 2], m_ref, acc_ref, n_stack)

    lp = lp_ref[...]
    lam = (jnp.exp(jnp.sum(lp[0:1] * lp[1:2], keepdims=True))
           - jnp.exp(jnp.sum(lp[2:3] * lp[3:4], keepdims=True)) + lam_init)
    col = lax.broadcasted_iota(jnp.int32, (1, LANES), 1)
    for h in range(N_HEADS):
        a1 = acc_ref[2 * h]
        a2 = acc_ref[2 * h + 1]
        o1 = a1 / a1[:, HEAD_V:HEAD_V + 1]
        o2 = a2 / a2[:, HEAD_V:HEAD_V + 1]
        a = jnp.where(col < HEAD_V, o1 - lam * o2, 0.0)
        ms = jnp.sum(a * a, axis=-1, keepdims=True) * (1.0 / HEAD_V)
        y = a * lax.rsqrt(ms + NORM_EPS) * og_ref[...] * (1.0 - lam_init)
        o_ref[:, h * LANES:(h + 1) * LANES] = y.astype(BF16)


def _diff_attn(dq, dk, dv, lp, og, lam_init, batch, seq):
    t = min(ATTN_TILE, seq)
    nq = seq // t
    n = batch * seq
    tab = jnp.asarray(_diff_diag_table(t))
    return pl.pallas_call(
        functools.partial(_diff_kernel, lam_init),
        grid=(batch, nq),
        in_specs=[
            pl.BlockSpec((t, 256), lambda b, i: (b * nq + i, 0)),
            pl.BlockSpec((seq, 256), lambda b, i: (b, 0)),
            pl.BlockSpec((seq, PAD_V), lambda b, i: (b, 0)),
            pl.BlockSpec(tab.shape, lambda b, i: (0, 0, 0)),
            pl.BlockSpec(lp.shape, lambda b, i: (0, 0)),
            pl.BlockSpec(og.shape, lambda b, i: (0, 0)),
        ],
        out_specs=pl.BlockSpec((t, PAD_V), lambda b, i: (b * nq + i, 0)),
        out_shape=jax.ShapeDtypeStruct((n, PAD_V), BF16),
        scratch_shapes=[
            pltpu.VMEM((2 * N_HEADS, t, 256), BF16),
            pltpu.VMEM((2 * N_HEADS, t, 1), F32),
            pltpu.VMEM((2 * N_HEADS, t, LANES), F32),
        ],
        compiler_params=_cparams("parallel", "arbitrary"),
        name="diff_attn",
    )(dq, dk, dv, tab, lp, og)


def _fox_kernel(q_ref, k_ref, v_ref, lf_ref, tri_ref, cmask_ref, o_ref,
                qs_ref, m_ref, acc_ref, f_ref):
    t = q_ref.shape[0]
    seq = k_ref.shape[0]
    i = pl.program_id(1)

    @pl.when(i == 0)
    def _():
        carry = jnp.zeros((8, 1), F32)
        tri = tri_ref[...]
        for blk in range(seq // LANES):
            x = lf_ref[:, blk * LANES:(blk + 1) * LANES]
            hi, mid, lo = _split3(x)
            f_ref[:, blk * LANES:(blk + 1) * LANES] = _dot(hi, tri) + _dot(mid, tri) + _dot(lo, tri) + carry
            carry = carry + jnp.sum(x, axis=-1, keepdims=True)

    _stack_queries(q_ref[...], qs_ref, N_HEADS)
    m_ref[...] = jnp.full_like(m_ref, NEG)
    acc_ref[...] = jnp.zeros_like(acc_ref)

    def v_block(koff):
        return lambda e: v_ref[pl.ds(koff, t), e * LANES:(e + 1) * LANES]

    def past(j, carry):
        koff = pl.multiple_of(j * t, t)
        fneg = -f_ref[:, pl.ds(koff, t)]
        _attn_block(qs_ref, k_ref[pl.ds(koff, t), :], v_block(koff),
                    lambda e: fneg[e:e + 1, :], m_ref, acc_ref, N_HEADS)
        return carry

    lax.fori_loop(0, i, past, 0)
    koff = pl.multiple_of(i * t, t)
    fneg = -f_ref[:, pl.ds(koff, t)]
    cmask = cmask_ref[...]
    _attn_block(qs_ref, k_ref[pl.ds(koff, t), :], v_block(koff),
                lambda e: cmask + fneg[e:e + 1, :], m_ref, acc_ref, N_HEADS)

    col = lax.broadcasted_iota(jnp.int32, (1, LANES), 1)
    for h in range(N_HEADS):
        a = acc_ref[h]
        o = jnp.where(col < HEAD_V, a / a[:, HEAD_V:HEAD_V + 1], 0.0)
        o_ref[:, h * LANES:(h + 1) * LANES] = o.astype(BF16)


def _fox_attn(fq, fk, fv, lf, batch, seq):
    t = min(ATTN_TILE, seq)
    nq = seq // t
    n = batch * seq
    tri = jnp.asarray(np.triu(np.ones((LANES, LANES), np.float32)), BF16)
    pos = np.arange(t)
    cmask = jnp.asarray(np.where(pos[None, :] <= pos[:, None], 0.0, NEG).astype(np.float32))
    return pl.pallas_call(
        _fox_kernel,
        grid=(batch, nq),
        in_specs=[
            pl.BlockSpec((t, 256), lambda b, i: (b * nq + i, 0)),
            pl.BlockSpec((seq, 256), lambda b, i: (b, 0)),
            pl.BlockSpec((seq, PAD_V), lambda b, i: (b, 0)),
            pl.BlockSpec((8, seq), lambda b, i: (0, b)),
            pl.BlockSpec(tri.shape, lambda b, i: (0, 0)),
            pl.BlockSpec(cmask.shape, lambda b, i: (0, 0)),
        ],
        out_specs=pl.BlockSpec((t, PAD_V), lambda b, i: (b * nq + i, 0)),
        out_shape=jax.ShapeDtypeStruct((n, PAD_V), BF16),
        scratch_shapes=[
            pltpu.VMEM((N_HEADS, t, 256), BF16),
            pltpu.VMEM((N_HEADS, t, 1), F32),
            pltpu.VMEM((N_HEADS, t, LANES), F32),
            pltpu.VMEM((8, seq), F32),
        ],
        compiler_params=_cparams("parallel", "arbitrary"),
        name="fox_attn",
    )(fq, fk, fv, lf, tri, cmask)


GLA_LEVELS = (1, 2, 4, 8, 16, 32)
N_LEVELS = len(GLA_LEVELS)


def _gla_tables():
    c = CHUNK
    idx = np.arange(c)
    sums = np.zeros((2 * N_LEVELS + 2, c, c), np.float32)
    pair = np.zeros((N_LEVELS + 1, c, c), np.float32)
    for l, m in enumerate(GLA_LEVELS):
        pos = idx % (2 * m)
        upper = pos >= m
        mid = idx - pos + m
        j = idx[None, :]
        sums[l] = upper[:, None] & (j >= mid[:, None]) & (j <= idx[:, None])
        sums[N_LEVELS + l] = (~upper)[:, None] & (j > idx[:, None]) & (j < mid[:, None])
        same = (idx[:, None] // (2 * m)) == (idx[None, :] // (2 * m))
        pair[l] = same & upper[:, None] & (~upper)[None, :]
    sums[2 * N_LEVELS] = idx[None, :] <= idx[:, None]
    sums[2 * N_LEVELS + 1] = idx[None, :] > idx[:, None]
    pair[N_LEVELS] = np.eye(c)
    sums = sums.reshape(-1, c)
    pair = np.tile(pair, (1, N_HEADS, 1))
    return sums, pair


def _gla_kernel(q_ref, k_ref, v_ref, r_ref, a_ref, sums_ref, pair_ref, og_ref, o_ref, state_ref):
    c = CHUNK
    hk = N_HEADS * GLA_K_DIM

    @pl.when(pl.program_id(1) == 0)
    def _():
        state_ref[...] = jnp.zeros_like(state_ref)

    col = lax.broadcasted_iota(jnp.int32, (1, hk), 1)
    head_cols = [col // GLA_K_DIM == h for h in range(N_HEADS)]
    ones = jnp.ones((c, LANES), BF16)

    def stack_heads(x):
        return jnp.concatenate([jnp.where(hc, x, 0.0) for hc in head_cols], axis=0).astype(BF16)

    def chunk(ci, carry):
        r0 = pl.multiple_of(ci * c, c)
        rows = pl.ds(r0, c)
        q = q_ref[rows, :]
        k = k_ref[rows, :]
        v = v_ref[rows, :]
        a_hi, a_mid, a_lo = _split3(a_ref[rows, :])
        sums = sums_ref[...]
        cs = _dot(sums, a_hi) + _dot(sums, a_mid) + _dot(sums, a_lo)

        attn = pair_ref[N_LEVELS] * _dot_nt(stack_heads(q), k.astype(BF16))
        for l in range(N_LEVELS):
            ql = q * jnp.exp(cs[l * c:(l + 1) * c])
            kl = k * jnp.exp(cs[(N_LEVELS + l) * c:(N_LEVELS + l + 1) * c])
            attn = attn + pair_ref[l] * _dot_nt(stack_heads(ql), kl.astype(BF16))
        attn = attn.astype(BF16)

        b = cs[2 * N_LEVELS * c:(2 * N_LEVELS + 1) * c]
        state = state_ref[...]
        o_inter = _dot(stack_heads(q * jnp.exp(b)), state.astype(BF16))

        k_dec = (k * jnp.exp(cs[(2 * N_LEVELS + 1) * c:])).astype(BF16)
        upd = lax.dot_general(k_dec, v, TN_DIMS, preferred_element_type=F32)
        tot = (lax.dot_general(a_hi, ones, TN_DIMS, preferred_element_type=F32)
               + lax.dot_general(a_mid, ones, TN_DIMS, preferred_element_type=F32)
               + lax.dot_general(a_lo, ones, TN_DIMS, preferred_element_type=F32))
        decay = jnp.exp(tot)

        for h in range(N_HEADS):
            hr = slice(h * c, (h + 1) * c)
            hv = slice(h * GLA_V_DIM, (h + 1) * GLA_V_DIM)
            o = o_inter[hr] + _dot(attn[hr], v[:, hv])
            y = _rms(o, og_ref[...])
            gate = r_ref[rows, hv]
            o_ref[rows, hv] = (y * (gate * jax.nn.sigmoid(gate))).astype(BF16)
            hd = slice(h * GLA_K_DIM, (h + 1) * GLA_K_DIM)
            state_ref[hd, :] = state[hd] * decay[hd] + upd[hd, hv]
        return carry

    lax.fori_loop(0, q_ref.shape[0] // c, chunk, 0)


def _gla(gq, gk, gv, gr, la, og, batch, seq):
    tg = min(GLA_TILE, seq)
    ng = seq // tg
    n = batch * seq
    sums_np, pair_np = _gla_tables()
    sums = jnp.asarray(sums_np, BF16)
    pair = jnp.asarray(pair_np)
    hk = N_HEADS * GLA_K_DIM
    hv = N_HEADS * GLA_V_DIM
    row = lambda b, i: (b * ng + i, 0)
    return pl.pallas_call(
        _gla_kernel,
        grid=(batch, ng),
        in_specs=[
            pl.BlockSpec((tg, hk), row),
            pl.BlockSpec((tg, hk), row),
            pl.BlockSpec((tg, hv), row),
            pl.BlockSpec((tg, hv), row),
            pl.BlockSpec((tg, hk), row),
            pl.BlockSpec(sums.shape, lambda b, i: (0, 0)),
            pl.BlockSpec(pair.shape, lambda b, i: (0, 0, 0)),
            pl.BlockSpec(og.shape, lambda b, i: (0, 0)),
        ],
        out_specs=pl.BlockSpec((tg, hv), row),
        out_shape=jax.ShapeDtypeStruct((n, hv), BF16),
        scratch_shapes=[pltpu.VMEM((hk, GLA_V_DIM), F32)],
        compiler_params=_cparams("parallel", "arbitrary"),
        name="gla",
    )(gq, gk, gv, gr, la, sums, pair, og)


def _out_kernel(x_ref, a_ref, g_ref, c_ref, w_ref, o_ref):
    acc = _dot(a_ref[...], w_ref[0:PAD_V, :])
    acc = acc + _dot(g_ref[...], w_ref[PAD_V:2 * PAD_V, :])
    acc = acc + _dot(c_ref[...], w_ref[2 * PAD_V:, :])
    o_ref[...] = x_ref[...] + acc


def _out_proj(x2d, a, g, c, w):
    n = x2d.shape[0]
    tm = min(TOKEN_TILE, n)
    row = lambda i: (i, 0)
    return pl.pallas_call(
        _out_kernel,
        grid=(n // tm,),
        in_specs=[pl.BlockSpec((tm, D_MODEL), row), pl.BlockSpec((tm, PAD_V), row),
                  pl.BlockSpec((tm, PAD_V), row), pl.BlockSpec((tm, PAD_V), row),
                  pl.BlockSpec(w.shape, lambda i: (0, 0))],
        out_specs=pl.BlockSpec((tm, D_MODEL), row),
        out_shape=jax.ShapeDtypeStruct((n, D_MODEL), F32),
        compiler_params=_cparams("parallel"),
        name="proj_out",
    )(x2d, a, g, c, w)


def _in_proj_columns():
    cols = np.full((PROJ_WIDTH,), IN_WIDTH, np.int64)
    cols[C_DQ:C_DQ + 256] = np.arange(0, 256)
    cols[C_DK:C_DK + 256] = np.arange(256, 512)
    cols[C_GQ:C_GQ + 256] = np.arange(768, 1024)
    cols[C_GK:C_GK + 256] = np.arange(1024, 1280)
    cols[C_GV:C_GV + 512] = np.arange(1280, 1792)
    cols[C_GR:C_GR + 512] = np.arange(1792, 2304)
    cols[C_GA:C_GA + GLA_GATE_RANK] = np.arange(2304, 2320)
    cols[C_FQ:C_FQ + 256] = np.arange(2320, 2576)
    cols[C_FK:C_FK + 256] = np.arange(2576, 2832)
    for h in range(N_HEADS):
        cols[C_DV + h * LANES:C_DV + h * LANES + HEAD_V] = np.arange(512 + h * HEAD_V, 512 + (h + 1) * HEAD_V)
        cols[C_FV + h * LANES:C_FV + h * LANES + HEAD_V] = np.arange(2832 + h * HEAD_V, 2832 + (h + 1) * HEAD_V)
    return cols


def _out_proj_rows():
    zero_row = 1024
    rows = np.full((MIX_PAD,), zero_row, np.int64)
    for h in range(N_HEADS):
        rows[h * LANES:h * LANES + HEAD_V] = np.arange(h * HEAD_V, (h + 1) * HEAD_V)
        rows[2 * PAD_V + h * LANES:2 * PAD_V + h * LANES + HEAD_V] = np.arange(768 + h * HEAD_V, 768 + (h + 1) * HEAD_V)
    rows[PAD_V:2 * PAD_V] = np.arange(256, 768)
    return rows


def _ffn_weights(w13, w2):
    pad = D_FF_PAD - D_FF

    def cols(w):
        w = jnp.pad(w, ((0, 0), (0, pad))).astype(BF16)
        return w.reshape(D_MODEL, N_FF_CHUNKS, FF_CHUNK).transpose(1, 0, 2)

    w2p = jnp.pad(w2, ((0, pad), (0, 0))).astype(BF16).reshape(N_FF_CHUNKS, FF_CHUNK, D_MODEL)
    return cols(w13[:, :D_FF]), cols(w13[:, D_FF:]), w2p


def _same_group(width):
    g = np.arange(256) // width
    return jnp.asarray((g[:, None] == g[None, :]).astype(np.float32), BF16)


def _pad_lanes(v, width=LANES):
    return jnp.pad(v, (0, width - v.shape[0])).reshape(1, width)


def kernel(x, ffn1_norm, ffn1_w13, ffn1_w2, mix_norm, w_in, w_out, diff_q_norm, diff_k_norm,
           diff_lambda, diff_out_norm, gla_alpha_w2, gla_alpha_b, gla_out_norm, fox_q_norm,
           fox_k_norm, fox_f_bias, ffn2_norm, ffn2_w13, ffn2_w2):
    batch, seq, _ = x.shape
    x2d = x.reshape(batch * seq, D_MODEL)
    in_cols = _in_proj_columns()
    out_rows = _out_proj_rows()
    grp32 = _same_group(DIFF_QK_DIM)
    grp64 = _same_group(HEAD_V)
    vone_np = np.zeros((1, PAD_V), np.float32)
    vone_np[0, HEAD_V::LANES] = 1.0
    vone = jnp.asarray(vone_np)

    for i in range(DEPTH):
        wg, wu, w2p = _ffn_weights(ffn1_w13[i], ffn1_w2[i])
        x2d = _ffn(x2d, ffn1_norm[i].reshape(1, D_MODEL), wg, wu, w2p)

        w_ext = jnp.pad(w_in[i], ((0, 0), (0, 1)))
        w = jnp.take(w_ext, in_cols, axis=1).astype(BF16)
        wf = jnp.pad(w_in[i][:, IN_WIDTH - N_HEADS:].T, ((0, 8 - N_HEADS), (0, 0))).astype(BF16)
        qg = jnp.stack([
            jnp.tile(diff_q_norm[i], 256 // DIFF_QK_DIM) * (DIFF_QK_DIM ** -0.5),
            jnp.tile(diff_k_norm[i], 256 // DIFF_QK_DIM),
            jnp.tile(fox_q_norm[i], N_HEADS) * (HEAD_V ** -0.5),
            jnp.tile(fox_k_norm[i], N_HEADS),
        ])
        aw2 = jnp.pad(gla_alpha_w2[i], ((0, LANES - GLA_GATE_RANK), (0, 0)))
        ab = gla_alpha_b[i].reshape(1, -1)
        fb = jnp.pad(fox_f_bias[i], (0, 8 - N_HEADS)).reshape(8, 1)
        (dq, dk, dv, gq, gk, gv, gr, la, fq, fk, fv, lf) = _proj(
            x2d, mix_norm[i].reshape(1, D_MODEL), w, wf, qg, grp32, grp64, aw2, ab, fb, vone)

        lam_init = 0.8 - 0.6 * math.exp(-0.3 * i)
        a = _diff_attn(dq, dk, dv, diff_lambda[i], _pad_lanes(diff_out_norm[i]), lam_init, batch, seq)
        g = _gla(gq, gk, gv, gr, la, gla_out_norm[i].reshape(1, GLA_V_DIM), batch, seq)
        c = _fox_attn(fq, fk, fv, lf, batch, seq)

        wo_ext = jnp.pad(w_out[i], ((0, 1), (0, 0)))
        wo = jnp.take(wo_ext, out_rows, axis=0).astype(BF16)
        x2d = _out_proj(x2d, a, g, c, wo)

        wg, wu, w2p = _ffn_weights(ffn2_w13[i], ffn2_w2[i])
        x2d = _ffn(x2d, ffn2_norm[i].reshape(1, D_MODEL), wg, wu, w2p)
    return x2d.reshape(batch, seq, D_MODEL)
```

```python
import functools
import math

import numpy as np
import jax
import jax.numpy as jnp
from jax import lax
from jax.experimental import pallas as pl
from jax.experimental.pallas import tpu as pltpu

F32 = jnp.float32
BF16 = jnp.bfloat16

D_MODEL = 1024
D_FF = 2752
DEPTH = 4
CHUNK = 64
N_HEADS = 4
DIFF_QK_DIM = 32
HEAD_V = 64
GLA_K_DIM = 64
GLA_V_DIM = 128
GLA_GATE_RANK = 16
GLA_TAU = 16.0
NORM_EPS = 1e-6
IN_WIDTH = 3092

LANES = 128
FF_CHUNK = 256
D_FF_PAD = 2816
N_FF_CHUNKS = D_FF_PAD // FF_CHUNK
TOKEN_TILE = 512
ATTN_TILE = 256
GLA_TILE = 512
NEG = -1e30
VMEM_LIMIT = 56 * 1024 * 1024

QK_WIDTH = 256
V_SLAB = 80
VT_ROWS = N_HEADS * V_SLAB

C_DK, C_GQ, C_GK, C_GV, C_GR, C_FK, C_GATES = 0, 256, 512, 768, 1280, 1792, 2048
PROJ_WIDTH = 2176
FOX_GATE_COL = GLA_GATE_RANK
R_DQ, R_DV, R_FQ, R_FV = 0, 256, 576, 832
PROJ_T_ROWS = 1152

NT_DIMS = (((1,), (1,)), ((), ()))
TN_DIMS = (((0,), (0,)), ((), ()))


def _cparams(*sem):
    return pltpu.CompilerParams(dimension_semantics=sem, vmem_limit_bytes=VMEM_LIMIT)


def _dot(a, b):
    return jnp.dot(a, b, preferred_element_type=F32)


def _dot_nt(a, b):
    return lax.dot_general(a, b, NT_DIMS, preferred_element_type=F32)


def _dot_tn(a, b):
    return lax.dot_general(a, b, TN_DIMS, preferred_element_type=F32)


def _split2(x):
    hi = x.astype(BF16)
    lo = (x - hi.astype(F32)).astype(BF16)
    return hi, lo


def _split3(x):
    hi = x.astype(BF16)
    r = x - hi.astype(F32)
    mid = r.astype(BF16)
    lo = (r - mid.astype(F32)).astype(BF16)
    return hi, mid, lo


def _log_sigmoid(z):
    return jnp.minimum(z, 0.0) - jnp.log(1.0 + jnp.exp(-jnp.abs(z)))


def _rms(x, g):
    ms = jnp.mean(x * x, axis=-1, keepdims=True)
    return x * lax.rsqrt(ms + NORM_EPS) * g


def _ffn_kernel(x_ref, g_ref, wg_ref, wu_ref, w2_ref, o_ref, acc_ref):
    x = x_ref[...]
    h = _rms(x, g_ref[...]).astype(BF16)
    acc_ref[...] = jnp.zeros_like(acc_ref)

    def step(c, carry):
        gate = _dot(h, wg_ref[c])
        up = _dot(h, wu_ref[c])
        act = (gate * jax.nn.sigmoid(gate) * up).astype(BF16)
        acc_ref[...] += _dot(act, w2_ref[c])
        return carry

    lax.fori_loop(0, N_FF_CHUNKS, step, 0)
    o_ref[...] = x + 0.5 * acc_ref[...]


def _ffn(x2d, gain, wg, wu, w2):
    n = x2d.shape[0]
    tm = min(TOKEN_TILE, n)
    const3 = lambda i: (0, 0, 0)
    return pl.pallas_call(
        _ffn_kernel,
        grid=(n // tm,),
        in_specs=[
            pl.BlockSpec((tm, D_MODEL), lambda i: (i, 0)),
            pl.BlockSpec((1, D_MODEL), lambda i: (0, 0)),
            pl.BlockSpec((N_FF_CHUNKS, D_MODEL, FF_CHUNK), const3, pipeline_mode=pl.Buffered(1)),
            pl.BlockSpec((N_FF_CHUNKS, D_MODEL, FF_CHUNK), const3, pipeline_mode=pl.Buffered(1)),
            pl.BlockSpec((N_FF_CHUNKS, FF_CHUNK, D_MODEL), const3, pipeline_mode=pl.Buffered(1)),
        ],
        out_specs=pl.BlockSpec((tm, D_MODEL), lambda i: (i, 0)),
        out_shape=jax.ShapeDtypeStruct((n, D_MODEL), F32),
        scratch_shapes=[pltpu.VMEM((tm, D_MODEL), F32)],
        compiler_params=_cparams("parallel"),
        name="ffn",
    )(x2d, gain, wg, wu, w2)


def _group_norm_lanes(t, grp, inv_d, gain):
    hi, lo = _split2(t * t)
    ms = (_dot(hi, grp) + _dot(lo, grp)) * inv_d
    return t * lax.rsqrt(ms + NORM_EPS) * gain


def _group_norm_rows(t, d, gain):
    rows, cols = t.shape
    t3 = t.reshape(rows // d, d, cols)
    ms = jnp.sum(t3 * t3, axis=1, keepdims=True) * (1.0 / d)
    return (t3 * lax.rsqrt(ms + NORM_EPS)).reshape(rows, cols) * gain


def _proj_kernel(x_ref, g_ref, w_ref, wt_ref, kg_ref, qgt_ref, grp32_ref, grp64_ref, aw2_ref,
                 ab_ref, fb_ref, vone_ref,
                 dqt_ref, dk_ref, dvt_ref, gq_ref, gk_ref, gv_ref, gr_ref, la_ref,
                 fqt_ref, fk_ref, fvt_ref, lf_ref):
    h = _rms(x_ref[...], g_ref[...]).astype(BF16)

    def proj(lo, width):
        return _dot(h, w_ref[:, lo:lo + width])

    def proj_t(lo, rows):
        return _dot_nt(wt_ref[lo:lo + rows, :], h)

    vone = vone_ref[...]
    dqt_ref[...] = _group_norm_rows(proj_t(R_DQ, QK_WIDTH), DIFF_QK_DIM, qgt_ref[:, 0:1]).astype(BF16)
    dk_ref[...] = _group_norm_lanes(proj(C_DK, QK_WIDTH), grp32_ref[...], 1.0 / DIFF_QK_DIM,
                                    kg_ref[0:1]).astype(BF16)
    dvt_ref[...] = (proj_t(R_DV, VT_ROWS) + vone).astype(BF16)
    gq_ref[...] = proj(C_GQ, 256) * (GLA_K_DIM ** -0.5)
    gk_ref[...] = proj(C_GK, 256)
    gv_ref[...] = proj(C_GV, 512).astype(BF16)
    gr_ref[...] = proj(C_GR, 512)
    fqt_ref[...] = _group_norm_rows(proj_t(R_FQ, QK_WIDTH), HEAD_V, qgt_ref[:, 1:2]).astype(BF16)
    fk_ref[...] = _group_norm_lanes(proj(C_FK, QK_WIDTH), grp64_ref[...], 1.0 / HEAD_V,
                                    kg_ref[1:2]).astype(BF16)
    fvt_ref[...] = (proj_t(R_FV, VT_ROWS) + vone).astype(BF16)

    gates = proj(C_GATES, LANES)
    ga_hi, ga_lo = _split2(gates)
    w_hi, w_lo = _split2(aw2_ref[...])
    z = _dot(ga_hi, w_hi) + _dot(ga_hi, w_lo) + _dot(ga_lo, w_hi) + ab_ref[...]
    la_ref[...] = _log_sigmoid(z) * (1.0 / GLA_TAU)
    lf_ref[...] = _log_sigmoid(gates + fb_ref[...])


def _proj(x2d, gain, w, wt, kg, qgt, grp32, grp64, aw2, ab, fb, vone):
    n = x2d.shape[0]
    tm = min(TOKEN_TILE, n)
    row = lambda i: (i, 0)
    colblk = lambda i: (0, i)
    const = lambda i: (0, 0)

    def full(a):
        return pl.BlockSpec(a.shape, const)

    def tok(width, dt):
        return jax.ShapeDtypeStruct((n, width), dt), pl.BlockSpec((tm, width), row)

    def feat(rows, dt):
        return jax.ShapeDtypeStruct((rows, n), dt), pl.BlockSpec((rows, tm), colblk)

    outs = [feat(QK_WIDTH, BF16), tok(QK_WIDTH, BF16), feat(VT_ROWS, BF16), tok(256, F32), tok(256, F32),
            tok(512, BF16), tok(512, F32), tok(256, F32), feat(QK_WIDTH, BF16), tok(QK_WIDTH, BF16),
            feat(VT_ROWS, BF16), tok(LANES, F32)]
    return pl.pallas_call(
        _proj_kernel,
        grid=(n // tm,),
        in_specs=[pl.BlockSpec((tm, D_MODEL), row), full(gain), full(w), full(wt), full(kg), full(qgt),
                  full(grp32), full(grp64), full(aw2), full(ab), full(fb), full(vone)],
        out_specs=[o[1] for o in outs],
        out_shape=[o[0] for o in outs],
        compiler_params=_cparams("parallel"),
        name="proj_in",
    )(x2d, gain, w, wt, kg, qgt, grp32, grp64, aw2, ab, fb, vone)


def _attn_pipeline(n_blocks, n_stack, scores_of, shift_of, vt_of, u_ref, m_ref, acc_ref):
    m_ref[...] = jnp.full_like(m_ref, NEG)
    acc_ref[...] = jnp.zeros_like(acc_ref)
    u_ref[0] = scores_of(0, 0)

    def body(j, carry):
        j_next = jnp.minimum(j + 1, n_blocks - 1)
        for e in range(n_stack):
            if e + 1 < n_stack:
                u_ref[e + 1] = scores_of(j, e + 1)
            else:
                u_ref[0] = scores_of(j_next, 0)
            u = u_ref[e]
            c = shift_of(j, e)
            m_old = m_ref[e]
            m_new = jnp.maximum(m_old, jnp.max(u, axis=0, keepdims=True) + c)
            alpha = jnp.exp(m_old - m_new)
            p = jnp.exp(u - (m_new - c)).astype(BF16)
            acc_ref[e] = alpha * acc_ref[e] + _dot(vt_of(j, e), p)
            m_ref[e] = m_new
        return carry

    lax.fori_loop(0, n_blocks, body, 0)


def _stack_queries(qt, qst_ref, n_stack):
    depth = qt.shape[0] // n_stack
    row = lax.broadcasted_iota(jnp.int32, (qt.shape[0], 1), 0)
    for e in range(n_stack):
        qst_ref[e] = jnp.where(row // depth == e, qt, jnp.zeros_like(qt))


def _softmax_out(acc):
    return acc[0:HEAD_V] / acc[HEAD_V:HEAD_V + 1]


DIFF_SLOPES = tuple(2.0 ** (-8.0 * (h + 1) / N_HEADS) for h in range(N_HEADS))


def _diff_tables(t):
    sl = np.arange(t)[:, None]
    tl = np.arange(t)[None, :]
    allowed = (sl // CHUNK) <= (tl // CHUNK)
    past = [slope * np.broadcast_to(sl, (t, t)) for slope in DIFF_SLOPES]
    diag = [np.where(allowed, slope * (tl - np.abs(tl - sl)), NEG) for slope in DIFF_SLOPES]
    return np.stack(past + diag).astype(np.float32)


def _diff_kernel(lam_init, qt_ref, k_ref, vt_ref, bias_ref, lp_ref, og_ref, o_ref,
                 qst_ref, u_ref, m_ref, acc_ref):
    t = qt_ref.shape[1]
    n_stack = 2 * N_HEADS
    i = pl.program_id(1)
    _stack_queries(qt_ref[...], qst_ref, n_stack)

    def keys(j):
        return pl.ds(pl.multiple_of(j * t, t), t)

    def scores(j, e):
        table = jnp.where(j == i, N_HEADS, 0) + e // 2
        return _dot(k_ref[keys(j), :], qst_ref[e]) + bias_ref[table]

    def shift(j, e):
        return DIFF_SLOPES[e // 2] * ((j - i) * t).astype(F32)

    def vt(j, e):
        return vt_ref[(e // 2) * V_SLAB:(e // 2 + 1) * V_SLAB, keys(j)]

    _attn_pipeline(i + 1, n_stack, scores, shift, vt, u_ref, m_ref, acc_ref)

    lp = lp_ref[...]
    lam = (jnp.exp(jnp.sum(lp[0:1] * lp[1:2], keepdims=True))
           - jnp.exp(jnp.sum(lp[2:3] * lp[3:4], keepdims=True)) + lam_init)
    for h in range(N_HEADS):
        a = _softmax_out(acc_ref[2 * h]) - lam * _softmax_out(acc_ref[2 * h + 1])
        ms = jnp.sum(a * a, axis=0, keepdims=True) * (1.0 / HEAD_V)
        y = a * lax.rsqrt(ms + NORM_EPS) * og_ref[...] * (1.0 - lam_init)
        o_ref[h * HEAD_V:(h + 1) * HEAD_V, :] = y.astype(BF16)


def _diff_attn(dqt, dk, dvt, lp, og, lam_init, batch, seq):
    t = min(ATTN_TILE, seq)
    nq = seq // t
    n = batch * seq
    bias = jnp.asarray(_diff_tables(t))
    n_stack = 2 * N_HEADS
    return pl.pallas_call(
        functools.partial(_diff_kernel, lam_init),
        grid=(batch, nq),
        in_specs=[
            pl.BlockSpec((QK_WIDTH, t), lambda b, i: (0, b * nq + i)),
            pl.BlockSpec((seq, QK_WIDTH), lambda b, i: (b, 0)),
            pl.BlockSpec((VT_ROWS, seq), lambda b, i: (0, b)),
            pl.BlockSpec(bias.shape, lambda b, i: (0, 0, 0)),
            pl.BlockSpec(lp.shape, lambda b, i: (0, 0)),
            pl.BlockSpec(og.shape, lambda b, i: (0, 0)),
        ],
        out_specs=pl.BlockSpec((N_HEADS * HEAD_V, t), lambda b, i: (0, b * nq + i)),
        out_shape=jax.ShapeDtypeStruct((N_HEADS * HEAD_V, n), BF16),
        scratch_shapes=[
            pltpu.VMEM((n_stack, QK_WIDTH, t), BF16),
            pltpu.VMEM((n_stack, t, t), F32),
            pltpu.VMEM((n_stack, 1, t), F32),
            pltpu.VMEM((n_stack, V_SLAB, t), F32),
        ],
        compiler_params=_cparams("parallel", "arbitrary"),
        name="diff_attn",
    )(dqt, dk, dvt, bias, lp, og)


def _fox_kernel(qt_ref, k_ref, vt_ref, lf_ref, tril_ref, mask_ref, o_ref,
                qst_ref, u_ref, m_ref, acc_ref, fneg_ref):
    t = qt_ref.shape[1]
    seq = k_ref.shape[0]
    i = pl.program_id(1)

    @pl.when(i == 0)
    def _():
        tril = tril_ref[...]
        for h in range(N_HEADS):
            def blk(bi, carry):
                r0 = pl.multiple_of(bi * LANES, LANES)
                x = lf_ref[pl.ds(r0, LANES), FOX_GATE_COL + h:FOX_GATE_COL + h + 1]
                hi, mid, lo = _split3(jnp.broadcast_to(x, (LANES, LANES)))
                f = _dot(tril, hi) + _dot(tril, mid) + _dot(tril, lo) + carry
                fneg_ref[h, pl.ds(r0, LANES), :] = -f
                return f[LANES - 1:LANES, :]

            lax.fori_loop(0, seq // LANES, blk, jnp.zeros((1, LANES), F32))

    _stack_queries(qt_ref[...], qst_ref, N_HEADS)

    def keys(j):
        return pl.ds(pl.multiple_of(j * t, t), t)

    def scores(j, e):
        f = fneg_ref[e, keys(j), :]
        bias = jnp.concatenate([f] * (t // LANES), axis=1) + mask_ref[jnp.where(j == i, 1, 0)]
        return _dot(k_ref[keys(j), :], qst_ref[e]) + bias

    def vt(j, e):
        return vt_ref[e * V_SLAB:(e + 1) * V_SLAB, keys(j)]

    _attn_pipeline(i + 1, N_HEADS, scores, lambda j, e: 0.0, vt, u_ref, m_ref, acc_ref)

    for h in range(N_HEADS):
        o_ref[h * HEAD_V:(h + 1) * HEAD_V, :] = _softmax_out(acc_ref[h]).astype(BF16)


def _fox_attn(fqt, fk, fvt, lf, batch, seq):
    t = min(ATTN_TILE, seq)
    nq = seq // t
    n = batch * seq
    tril = jnp.asarray(np.tril(np.ones((LANES, LANES), np.float32)), BF16)
    pos = np.arange(t)
    causal = np.where(pos[:, None] <= pos[None, :], 0.0, NEG)
    mask = jnp.asarray(np.stack([np.zeros((t, t)), causal]).astype(np.float32))
    return pl.pallas_call(
        _fox_kernel,
        grid=(batch, nq),
        in_specs=[
            pl.BlockSpec((QK_WIDTH, t), lambda b, i: (0, b * nq + i)),
            pl.BlockSpec((seq, QK_WIDTH), lambda b, i: (b, 0)),
            pl.BlockSpec((VT_ROWS, seq), lambda b, i: (0, b)),
            pl.BlockSpec((seq, LANES), lambda b, i: (b, 0)),
            pl.BlockSpec(tril.shape, lambda b, i: (0, 0)),
            pl.BlockSpec(mask.shape, lambda b, i: (0, 0, 0)),
        ],
        out_specs=pl.BlockSpec((N_HEADS * HEAD_V, t), lambda b, i: (0, b * nq + i)),
        out_shape=jax.ShapeDtypeStruct((N_HEADS * HEAD_V, n), BF16),
        scratch_shapes=[
            pltpu.VMEM((N_HEADS, QK_WIDTH, t), BF16),
            pltpu.VMEM((N_HEADS, t, t), F32),
            pltpu.VMEM((N_HEADS, 1, t), F32),
            pltpu.VMEM((N_HEADS, V_SLAB, t), F32),
            pltpu.VMEM((N_HEADS, seq, LANES), F32),
        ],
        compiler_params=_cparams("parallel", "arbitrary"),
        name="fox_attn",
    )(fqt, fk, fvt, lf, tril, mask)


GLA_LEVELS = (1, 2, 4, 8, 16, 32)
N_LEVELS = len(GLA_LEVELS)


def _gla_tables():
    c = CHUNK
    idx = np.arange(c)
    sums = np.zeros((2 * N_LEVELS + 2, c, c), np.float32)
    pair = np.zeros((N_LEVELS + 1, c, c), np.float32)
    for l, m in enumerate(GLA_LEVELS):
        pos = idx % (2 * m)
        upper = pos >= m
        mid = idx - pos + m
        j = idx[None, :]
        sums[l] = upper[:, None] & (j >= mid[:, None]) & (j <= idx[:, None])
        sums[N_LEVELS + l] = (~upper)[:, None] & (j > idx[:, None]) & (j < mid[:, None])
        same = (idx[:, None] // (2 * m)) == (idx[None, :] // (2 * m))
        pair[l] = same & upper[:, None] & (~upper)[None, :]
    sums[2 * N_LEVELS] = idx[None, :] <= idx[:, None]
    sums[2 * N_LEVELS + 1] = idx[None, :] > idx[:, None]
    pair[N_LEVELS] = np.eye(c)
    sums = sums.reshape(-1, c)
    pair = np.tile(pair, (1, N_HEADS, 1))
    return sums, pair


def _gla_kernel(q_ref, k_ref, v_ref, r_ref, a_ref, sums_ref, pair_ref, og_ref, o_ref, state_ref):
    c = CHUNK
    hk = N_HEADS * GLA_K_DIM

    @pl.when(pl.program_id(1) == 0)
    def _():
        state_ref[...] = jnp.zeros_like(state_ref)

    col = lax.broadcasted_iota(jnp.int32, (1, hk), 1)
    head_cols = [col // GLA_K_DIM == h for h in range(N_HEADS)]
    ones = jnp.ones((c, LANES), BF16)

    def stack_heads(x):
        return jnp.concatenate([jnp.where(hc, x, 0.0) for hc in head_cols], axis=0).astype(BF16)

    def chunk(ci, carry):
        r0 = pl.multiple_of(ci * c, c)
        rows = pl.ds(r0, c)
        q = q_ref[rows, :]
        k = k_ref[rows, :]
        v = v_ref[rows, :]
        a_hi, a_mid, a_lo = _split3(a_ref[rows, :])
        sums = sums_ref[...]
        cs = _dot(sums, a_hi) + _dot(sums, a_mid) + _dot(sums, a_lo)

        attn = pair_ref[N_LEVELS] * _dot_nt(stack_heads(q), k.astype(BF16))
        for l in range(N_LEVELS):
            ql = q * jnp.exp(cs[l * c:(l + 1) * c])
            kl = k * jnp.exp(cs[(N_LEVELS + l) * c:(N_LEVELS + l + 1) * c])
            attn = attn + pair_ref[l] * _dot_nt(stack_heads(ql), kl.astype(BF16))
        attn = attn.astype(BF16)

        b = cs[2 * N_LEVELS * c:(2 * N_LEVELS + 1) * c]
        state = state_ref[...]
        o_inter = _dot(stack_heads(q * jnp.exp(b)), state.astype(BF16))

        k_dec = (k * jnp.exp(cs[(2 * N_LEVELS + 1) * c:])).astype(BF16)
        upd = _dot_tn(k_dec, v)
        tot = _dot_tn(a_hi, ones) + _dot_tn(a_mid, ones) + _dot_tn(a_lo, ones)
        decay = jnp.exp(tot)

        for h in range(N_HEADS):
            hr = slice(h * c, (h + 1) * c)
            hv = slice(h * GLA_V_DIM, (h + 1) * GLA_V_DIM)
            o = o_inter[hr] + _dot(attn[hr], v[:, hv])
            y = _rms(o, og_ref[...])
            gate = r_ref[rows, hv]
            o_ref[rows, hv] = (y * (gate * jax.nn.sigmoid(gate))).astype(BF16)
            hd = slice(h * GLA_K_DIM, (h + 1) * GLA_K_DIM)
            state_ref[hd, :] = state[hd] * decay[hd] + upd[hd, hv]
        return carry

    lax.fori_loop(0, q_ref.shape[0] // c, chunk, 0)


def _gla(gq, gk, gv, gr, la, og, batch, seq):
    tg = min(GLA_TILE, seq)
    ng = seq // tg
    n = batch * seq
    sums_np, pair_np = _gla_tables()
    sums = jnp.asarray(sums_np, BF16)
    pair = jnp.asarray(pair_np)
    hk = N_HEADS * GLA_K_DIM
    hv = N_HEADS * GLA_V_DIM
    row = lambda b, i: (b * ng + i, 0)
    return pl.pallas_call(
        _gla_kernel,
        grid=(batch, ng),
        in_specs=[
            pl.BlockSpec((tg, hk), row),
            pl.BlockSpec((tg, hk), row),
            pl.BlockSpec((tg, hv), row),
            pl.BlockSpec((tg, hv), row),
            pl.BlockSpec((tg, hk), row),
            pl.BlockSpec(sums.shape, lambda b, i: (0, 0)),
            pl.BlockSpec(pair.shape, lambda b, i: (0, 0, 0)),
            pl.BlockSpec(og.shape, lambda b, i: (0, 0)),
        ],
        out_specs=pl.BlockSpec((tg, hv), row),
        out_shape=jax.ShapeDtypeStruct((n, hv), BF16),
        scratch_shapes=[pltpu.VMEM((hk, GLA_V_DIM), F32)],
        compiler_params=_cparams("parallel", "arbitrary"),
        name="gla",
    )(gq, gk, gv, gr, la, sums, pair, og)


DIFF_WIDTH = N_HEADS * HEAD_V
GLA_WIDTH = N_HEADS * GLA_V_DIM


def _out_kernel(x_ref, at_ref, g_ref, ct_ref, w_ref, o_ref):
    acc = _dot_tn(at_ref[...], w_ref[0:DIFF_WIDTH, :])
    acc = acc + _dot(g_ref[...], w_ref[DIFF_WIDTH:DIFF_WIDTH + GLA_WIDTH, :])
    acc = acc + _dot_tn(ct_ref[...], w_ref[DIFF_WIDTH + GLA_WIDTH:, :])
    o_ref[...] = x_ref[...] + acc


def _out_proj(x2d, at, g, ct, w):
    n = x2d.shape[0]
    tm = min(TOKEN_TILE, n)
    row = lambda i: (i, 0)
    colblk = lambda i: (0, i)
    return pl.pallas_call(
        _out_kernel,
        grid=(n // tm,),
        in_specs=[pl.BlockSpec((tm, D_MODEL), row), pl.BlockSpec((DIFF_WIDTH, tm), colblk),
                  pl.BlockSpec((tm, GLA_WIDTH), row), pl.BlockSpec((DIFF_WIDTH, tm), colblk),
                  pl.BlockSpec(w.shape, lambda i: (0, 0))],
        out_specs=pl.BlockSpec((tm, D_MODEL), row),
        out_shape=jax.ShapeDtypeStruct((n, D_MODEL), F32),
        compiler_params=_cparams("parallel"),
        name="proj_out",
    )(x2d, at, g, ct, w)


def _in_proj_columns():
    cols = np.full((PROJ_WIDTH,), IN_WIDTH, np.int64)
    cols[C_DK:C_DK + 256] = np.arange(256, 512)
    cols[C_GQ:C_GQ + 256] = np.arange(768, 1024)
    cols[C_GK:C_GK + 256] = np.arange(1024, 1280)
    cols[C_GV:C_GV + 512] = np.arange(1280, 1792)
    cols[C_GR:C_GR + 512] = np.arange(1792, 2304)
    cols[C_FK:C_FK + 256] = np.arange(2576, 2832)
    cols[C_GATES:C_GATES + GLA_GATE_RANK] = np.arange(2304, 2320)
    cols[C_GATES + FOX_GATE_COL:C_GATES + FOX_GATE_COL + N_HEADS] = np.arange(3088, 3092)
    return cols


def _in_proj_rows():
    rows = np.full((PROJ_T_ROWS,), IN_WIDTH, np.int64)
    rows[R_DQ:R_DQ + 256] = np.arange(0, 256)
    rows[R_FQ:R_FQ + 256] = np.arange(2320, 2576)
    for h in range(N_HEADS):
        rows[R_DV + h * V_SLAB:R_DV + h * V_SLAB + HEAD_V] = np.arange(512 + h * HEAD_V, 512 + (h + 1) * HEAD_V)
        rows[R_FV + h * V_SLAB:R_FV + h * V_SLAB + HEAD_V] = np.arange(2832 + h * HEAD_V, 2832 + (h + 1) * HEAD_V)
    return rows


def _ffn_weights(w13, w2):
    pad = D_FF_PAD - D_FF

    def cols(w):
        w = jnp.pad(w, ((0, 0), (0, pad))).astype(BF16)
        return w.reshape(D_MODEL, N_FF_CHUNKS, FF_CHUNK).transpose(1, 0, 2)

    w2p = jnp.pad(w2, ((0, pad), (0, 0))).astype(BF16).reshape(N_FF_CHUNKS, FF_CHUNK, D_MODEL)
    return cols(w13[:, :D_FF]), cols(w13[:, D_FF:]), w2p


def _same_group(width):
    g = np.arange(QK_WIDTH) // width
    return jnp.asarray((g[:, None] == g[None, :]).astype(np.float32), BF16)


def kernel(x, ffn1_norm, ffn1_w13, ffn1_w2, mix_norm, w_in, w_out, diff_q_norm, diff_k_norm,
           diff_lambda, diff_out_norm, gla_alpha_w2, gla_alpha_b, gla_out_norm, fox_q_norm,
           fox_k_norm, fox_f_bias, ffn2_norm, ffn2_w13, ffn2_w2):
    batch, seq, _ = x.shape
    x2d = x.reshape(batch * seq, D_MODEL)
    in_cols = _in_proj_columns()
    in_rows = _in_proj_rows()
    grp32 = _same_group(DIFF_QK_DIM)
    grp64 = _same_group(HEAD_V)
    vone_np = np.zeros((VT_ROWS, 1), np.float32)
    vone_np[HEAD_V::V_SLAB, 0] = 1.0
    vone = jnp.asarray(vone_np)

    for i in range(DEPTH):
        wg, wu, w2p = _ffn_weights(ffn1_w13[i], ffn1_w2[i])
        x2d = _ffn(x2d, ffn1_norm[i].reshape(1, D_MODEL), wg, wu, w2p)

        w_ext = jnp.pad(w_in[i], ((0, 0), (0, 1)))
        w = jnp.take(w_ext, in_cols, axis=1).astype(BF16)
        wt = jnp.take(w_ext, in_rows, axis=1).T.astype(BF16)
        kg = jnp.stack([jnp.tile(diff_k_norm[i], QK_WIDTH // DIFF_QK_DIM), jnp.tile(fox_k_norm[i], N_HEADS)])
        qgt = jnp.stack([jnp.tile(diff_q_norm[i], QK_WIDTH // DIFF_QK_DIM) * (DIFF_QK_DIM ** -0.5),
                         jnp.tile(fox_q_norm[i], N_HEADS) * (HEAD_V ** -0.5)], axis=1)
        aw2 = jnp.pad(gla_alpha_w2[i], ((0, LANES - GLA_GATE_RANK), (0, 0)))
        ab = gla_alpha_b[i].reshape(1, -1)
        fb = jnp.pad(fox_f_bias[i], (FOX_GATE_COL, LANES - FOX_GATE_COL - N_HEADS)).reshape(1, LANES)
        (dqt, dk, dvt, gq, gk, gv, gr, la, fqt, fk, fvt, lf) = _proj(
            x2d, mix_norm[i].reshape(1, D_MODEL), w, wt, kg, qgt, grp32, grp64, aw2, ab, fb, vone)

        lam_init = 0.8 - 0.6 * math.exp(-0.3 * i)
        at = _diff_attn(dqt, dk, dvt, diff_lambda[i], diff_out_norm[i].reshape(HEAD_V, 1), lam_init,
                        batch, seq)
        g = _gla(gq, gk, gv, gr, la, gla_out_norm[i].reshape(1, GLA_V_DIM), batch, seq)
        ct = _fox_attn(fqt, fk, fvt, lf, batch, seq)
        x2d = _out_proj(x2d, at, g, ct, w_out[i].astype(BF16))

        wg, wu, w2p = _ffn_weights(ffn2_w13[i], ffn2_w2[i])
        x2d = _ffn(x2d, ffn2_norm[i].reshape(1, D_MODEL), wg, wu, w2p)
    return x2d.reshape(batch, seq, D_MODEL)
```

```python
import functools
import math

import numpy as np
import jax
import jax.numpy as jnp
from jax import lax
from jax.experimental import pallas as pl
from jax.experimental.pallas import tpu as pltpu

F32 = jnp.float32
BF16 = jnp.bfloat16

D_MODEL = 1024
D_FF = 2752
DEPTH = 4
CHUNK = 64
N_HEADS = 4
DIFF_QK_DIM = 32
HEAD_V = 64
GLA_K_DIM = 64
GLA_V_DIM = 128
GLA_GATE_RANK = 16
GLA_TAU = 16.0
NORM_EPS = 1e-6
IN_WIDTH = 3092

LANES = 128
FF_CHUNK = 256
D_FF_PAD = 2816
N_FF_CHUNKS = D_FF_PAD // FF_CHUNK
TOKEN_TILE = 512
ATTN_TILE = 256
GLA_TILE = 512
NEG = -1e30
VMEM_LIMIT = 56 * 1024 * 1024

QK_WIDTH = 256
V_SLAB = 80
VT_ROWS = N_HEADS * V_SLAB

C_DK, C_GQ, C_GK, C_GV, C_GR, C_FK, C_GATES = 0, 256, 512, 768, 1280, 1792, 2048
PROJ_WIDTH = 2176
FOX_GATE_COL = GLA_GATE_RANK
R_DQ, R_DV, R_FQ, R_FV = 0, 256, 576, 832
PROJ_T_ROWS = 1152

NT_DIMS = (((1,), (1,)), ((), ()))
TN_DIMS = (((0,), (0,)), ((), ()))


def _cparams(*sem):
    return pltpu.CompilerParams(dimension_semantics=sem, vmem_limit_bytes=VMEM_LIMIT)


def _dot(a, b):
    return jnp.dot(a, b, preferred_element_type=F32)


def _dot_nt(a, b):
    return lax.dot_general(a, b, NT_DIMS, preferred_element_type=F32)


def _dot_tn(a, b):
    return lax.dot_general(a, b, TN_DIMS, preferred_element_type=F32)


def _split2(x):
    hi = x.astype(BF16)
    lo = (x - hi.astype(F32)).astype(BF16)
    return hi, lo


def _split3(x):
    hi = x.astype(BF16)
    r = x - hi.astype(F32)
    mid = r.astype(BF16)
    lo = (r - mid.astype(F32)).astype(BF16)
    return hi, mid, lo


def _log_sigmoid(z):
    return jnp.minimum(z, 0.0) - jnp.log(1.0 + jnp.exp(-jnp.abs(z)))


def _rms(x, g):
    ms = jnp.mean(x * x, axis=-1, keepdims=True)
    return x * lax.rsqrt(ms + NORM_EPS) * g


def _ffn_kernel(x_ref, g_ref, wg_ref, wu_ref, w2_ref, o_ref):
    x = x_ref[...]
    h = _rms(x, g_ref[...]).astype(BF16)
    acc = None
    for c in range(N_FF_CHUNKS):
        cols = slice(c * FF_CHUNK, (c + 1) * FF_CHUNK)
        gate = _dot(h, wg_ref[:, cols])
        up = _dot(h, wu_ref[:, cols])
        act = (gate * jax.nn.sigmoid(gate) * up).astype(BF16)
        term = _dot(act, w2_ref[cols, :])
        acc = term if acc is None else acc + term
    o_ref[...] = x + 0.5 * acc


def _ffn(x2d, gain, wg, wu, w2):
    n = x2d.shape[0]
    tm = min(TOKEN_TILE, n)
    const = lambda i: (0, 0)
    return pl.pallas_call(
        _ffn_kernel,
        grid=(n // tm,),
        in_specs=[
            pl.BlockSpec((tm, D_MODEL), lambda i: (i, 0)),
            pl.BlockSpec((1, D_MODEL), const),
            pl.BlockSpec((D_MODEL, D_FF_PAD), const, pipeline_mode=pl.Buffered(1)),
            pl.BlockSpec((D_MODEL, D_FF_PAD), const, pipeline_mode=pl.Buffered(1)),
            pl.BlockSpec((D_FF_PAD, D_MODEL), const, pipeline_mode=pl.Buffered(1)),
        ],
        out_specs=pl.BlockSpec((tm, D_MODEL), lambda i: (i, 0)),
        out_shape=jax.ShapeDtypeStruct((n, D_MODEL), F32),
        compiler_params=_cparams("parallel"),
        name="ffn",
    )(x2d, gain, wg, wu, w2)


def _group_norm_lanes(t, grp, inv_d, gain):
    hi, lo = _split2(t * t)
    ms = (_dot(hi, grp) + _dot(lo, grp)) * inv_d
    return t * lax.rsqrt(ms + NORM_EPS) * gain


def _group_norm_rows(t, d, gain):
    rows, cols = t.shape
    t3 = t.reshape(rows // d, d, cols)
    ms = jnp.sum(t3 * t3, axis=1, keepdims=True) * (1.0 / d)
    return (t3 * lax.rsqrt(ms + NORM_EPS)).reshape(rows, cols) * gain


def _proj_kernel(x_ref, g_ref, w_ref, wt_ref, kg_ref, qgt_ref, grp32_ref, grp64_ref, aw2_ref,
                 ab_ref, fb_ref, vone_ref,
                 dqt_ref, dk_ref, dvt_ref, gq_ref, gk_ref, gv_ref, gr_ref, la_ref,
                 fqt_ref, fk_ref, fvt_ref, lf_ref):
    h = _rms(x_ref[...], g_ref[...]).astype(BF16)

    def proj(lo, width):
        return _dot(h, w_ref[:, lo:lo + width])

    def proj_t(lo, rows):
        return _dot_nt(wt_ref[lo:lo + rows, :], h)

    vone = vone_ref[...]
    dqt_ref[...] = _group_norm_rows(proj_t(R_DQ, QK_WIDTH), DIFF_QK_DIM, qgt_ref[:, 0:1]).astype(BF16)
    dk_ref[...] = _group_norm_lanes(proj(C_DK, QK_WIDTH), grp32_ref[...], 1.0 / DIFF_QK_DIM,
                                    kg_ref[0:1]).astype(BF16)
    dvt_ref[...] = (proj_t(R_DV, VT_ROWS) + vone).astype(BF16)
    gq_ref[...] = proj(C_GQ, 256) * (GLA_K_DIM ** -0.5)
    gk_ref[...] = proj(C_GK, 256)
    gv_ref[...] = proj(C_GV, 512).astype(BF16)
    gr_ref[...] = proj(C_GR, 512)
    fqt_ref[...] = _group_norm_rows(proj_t(R_FQ, QK_WIDTH), HEAD_V, qgt_ref[:, 1:2]).astype(BF16)
    fk_ref[...] = _group_norm_lanes(proj(C_FK, QK_WIDTH), grp64_ref[...], 1.0 / HEAD_V,
                                    kg_ref[1:2]).astype(BF16)
    fvt_ref[...] = (proj_t(R_FV, VT_ROWS) + vone).astype(BF16)

    gates = proj(C_GATES, LANES)
    ga_hi, ga_lo = _split2(gates)
    w_hi, w_lo = _split2(aw2_ref[...])
    z = _dot(ga_hi, w_hi) + _dot(ga_hi, w_lo) + _dot(ga_lo, w_hi) + ab_ref[...]
    la_ref[...] = _log_sigmoid(z) * (1.0 / GLA_TAU)
    lf_ref[...] = _log_sigmoid(gates + fb_ref[...])


def _proj(x2d, gain, w, wt, kg, qgt, grp32, grp64, aw2, ab, fb, vone):
    n = x2d.shape[0]
    tm = min(TOKEN_TILE, n)
    row = lambda i: (i, 0)
    colblk = lambda i: (0, i)
    const = lambda i: (0, 0)

    def full(a):
        return pl.BlockSpec(a.shape, const)

    def tok(width, dt):
        return jax.ShapeDtypeStruct((n, width), dt), pl.BlockSpec((tm, width), row)

    def feat(rows, dt):
        return jax.ShapeDtypeStruct((rows, n), dt), pl.BlockSpec((rows, tm), colblk)

    outs = [feat(QK_WIDTH, BF16), tok(QK_WIDTH, BF16), feat(VT_ROWS, BF16), tok(256, F32), tok(256, F32),
            tok(512, BF16), tok(512, F32), tok(256, F32), feat(QK_WIDTH, BF16), tok(QK_WIDTH, BF16),
            feat(VT_ROWS, BF16), tok(LANES, F32)]
    return pl.pallas_call(
        _proj_kernel,
        grid=(n // tm,),
        in_specs=[pl.BlockSpec((tm, D_MODEL), row), full(gain), full(w), full(wt), full(kg), full(qgt),
                  full(grp32), full(grp64), full(aw2), full(ab), full(fb), full(vone)],
        out_specs=[o[1] for o in outs],
        out_shape=[o[0] for o in outs],
        compiler_params=_cparams("parallel"),
        name="proj_in",
    )(x2d, gain, w, wt, kg, qgt, grp32, grp64, aw2, ab, fb, vone)


def _attn_pipeline(n_blocks, n_stack, scores_of, shift_of, vt_of, u_ref, m_ref, acc_ref):
    m_ref[...] = jnp.full_like(m_ref, NEG)
    acc_ref[...] = jnp.zeros_like(acc_ref)
    u_ref[0] = scores_of(0, 0)

    def body(j, carry):
        j_next = jnp.minimum(j + 1, n_blocks - 1)
        for e in range(n_stack):
            if e + 1 < n_stack:
                u_ref[e + 1] = scores_of(j, e + 1)
            else:
                u_ref[0] = scores_of(j_next, 0)
            u = u_ref[e]
            c = shift_of(j, e)
            m_old = m_ref[e]
            m_new = jnp.maximum(m_old, jnp.max(u, axis=0, keepdims=True) + c)
            alpha = jnp.exp(m_old - m_new)
            p = jnp.exp(u - (m_new - c)).astype(BF16)
            acc_ref[e] = alpha * acc_ref[e] + _dot(vt_of(j, e), p)
            m_ref[e] = m_new
        return carry

    lax.fori_loop(0, n_blocks, body, 0)


def _stack_queries(qt, qst_ref, n_stack):
    depth = qt.shape[0] // n_stack
    row = lax.broadcasted_iota(jnp.int32, (qt.shape[0], 1), 0)
    for e in range(n_stack):
        qst_ref[e] = jnp.where(row // depth == e, qt, jnp.zeros_like(qt))


def _softmax_out(acc):
    return acc[0:HEAD_V] / acc[HEAD_V:HEAD_V + 1]


DIFF_SLOPES = tuple(2.0 ** (-8.0 * (h + 1) / N_HEADS) for h in range(N_HEADS))


def _diff_tables(t):
    sl = np.arange(t)[:, None]
    tl = np.arange(t)[None, :]
    allowed = (sl // CHUNK) <= (tl // CHUNK)
    past = [slope * np.broadcast_to(sl, (t, t)) for slope in DIFF_SLOPES]
    diag = [np.where(allowed, slope * (tl - np.abs(tl - sl)), NEG) for slope in DIFF_SLOPES]
    return np.stack(past + diag).astype(np.float32)


def _diff_kernel(lam_init, qt_ref, k_ref, vt_ref, bias_ref, lp_ref, og_ref, o_ref,
                 qst_ref, u_ref, m_ref, acc_ref):
    t = qt_ref.shape[1]
    n_stack = 2 * N_HEADS
    i = pl.program_id(1)
    _stack_queries(qt_ref[...], qst_ref, n_stack)

    def keys(j):
        return pl.ds(pl.multiple_of(j * t, t), t)

    def scores(j, e):
        table = jnp.where(j == i, N_HEADS, 0) + e // 2
        return _dot(k_ref[keys(j), :], qst_ref[e]) + bias_ref[table]

    def shift(j, e):
        return DIFF_SLOPES[e // 2] * ((j - i) * t).astype(F32)

    def vt(j, e):
        return vt_ref[(e // 2) * V_SLAB:(e // 2 + 1) * V_SLAB, keys(j)]

    _attn_pipeline(i + 1, n_stack, scores, shift, vt, u_ref, m_ref, acc_ref)

    lp = lp_ref[...]
    lam = (jnp.exp(jnp.sum(lp[0:1] * lp[1:2], keepdims=True))
           - jnp.exp(jnp.sum(lp[2:3] * lp[3:4], keepdims=True)) + lam_init)
    for h in range(N_HEADS):
        a = _softmax_out(acc_ref[2 * h]) - lam * _softmax_out(acc_ref[2 * h + 1])
        ms = jnp.sum(a * a, axis=0, keepdims=True) * (1.0 / HEAD_V)
        y = a * lax.rsqrt(ms + NORM_EPS) * og_ref[...] * (1.0 - lam_init)
        o_ref[h * HEAD_V:(h + 1) * HEAD_V, :] = y.astype(BF16)


def _diff_attn(dqt, dk, dvt, lp, og, lam_init, batch, seq):
    t = min(ATTN_TILE, seq)
    nq = seq // t
    n = batch * seq
    bias = jnp.asarray(_diff_tables(t))
    n_stack = 2 * N_HEADS
    return pl.pallas_call(
        functools.partial(_diff_kernel, lam_init),
        grid=(batch, nq),
        in_specs=[
            pl.BlockSpec((QK_WIDTH, t), lambda b, i: (0, b * nq + i)),
            pl.BlockSpec((seq, QK_WIDTH), lambda b, i: (b, 0)),
            pl.BlockSpec((VT_ROWS, seq), lambda b, i: (0, b)),
            pl.BlockSpec(bias.shape, lambda b, i: (0, 0, 0)),
            pl.BlockSpec(lp.shape, lambda b, i: (0, 0)),
            pl.BlockSpec(og.shape, lambda b, i: (0, 0)),
        ],
        out_specs=pl.BlockSpec((N_HEADS * HEAD_V, t), lambda b, i: (0, b * nq + i)),
        out_shape=jax.ShapeDtypeStruct((N_HEADS * HEAD_V, n), BF16),
        scratch_shapes=[
            pltpu.VMEM((n_stack, QK_WIDTH, t), BF16),
            pltpu.VMEM((n_stack, t, t), F32),
            pltpu.VMEM((n_stack, 1, t), F32),
            pltpu.VMEM((n_stack, V_SLAB, t), F32),
        ],
        compiler_params=_cparams("parallel", "arbitrary"),
        name="diff_attn",
    )(dqt, dk, dvt, bias, lp, og)


def _fox_kernel(qt_ref, k_ref, vt_ref, lf_ref, tril_ref, mask_ref, o_ref,
                qst_ref, u_ref, m_ref, acc_ref, fneg_ref):
    t = qt_ref.shape[1]
    seq = k_ref.shape[0]
    i = pl.program_id(1)

    @pl.when(i == 0)
    def _():
        tril = tril_ref[...]
        for blk in range(seq // LANES):
            rows = slice(blk * LANES, (blk + 1) * LANES)
            x = jnp.concatenate(
                [jnp.broadcast_to(lf_ref[rows, FOX_GATE_COL + h:FOX_GATE_COL + h + 1], (LANES, LANES))
                 for h in range(N_HEADS)], axis=1)
            hi, mid, lo = _split3(x)
            f = _dot(tril, hi) + _dot(tril, mid) + _dot(tril, lo)
            for h in range(N_HEADS):
                fneg_ref[h, rows, :] = f[:, h * LANES:(h + 1) * LANES]
        for h in range(N_HEADS):
            carry = jnp.zeros((1, LANES), F32)
            for blk in range(seq // LANES):
                rows = slice(blk * LANES, (blk + 1) * LANES)
                f = fneg_ref[h, rows, :] + carry
                fneg_ref[h, rows, :] = -f
                carry = f[LANES - 1:LANES, :]

    _stack_queries(qt_ref[...], qst_ref, N_HEADS)

    def keys(j):
        return pl.ds(pl.multiple_of(j * t, t), t)

    def scores(j, e):
        f = fneg_ref[e, keys(j), :]
        bias = jnp.concatenate([f] * (t // LANES), axis=1) + mask_ref[jnp.where(j == i, 1, 0)]
        return _dot(k_ref[keys(j), :], qst_ref[e]) + bias

    def vt(j, e):
        return vt_ref[e * V_SLAB:(e + 1) * V_SLAB, keys(j)]

    _attn_pipeline(i + 1, N_HEADS, scores, lambda j, e: 0.0, vt, u_ref, m_ref, acc_ref)

    for h in range(N_HEADS):
        o_ref[h * HEAD_V:(h + 1) * HEAD_V, :] = _softmax_out(acc_ref[h]).astype(BF16)


def _fox_attn(fqt, fk, fvt, lf, batch, seq):
    t = min(ATTN_TILE, seq)
    nq = seq // t
    n = batch * seq
    tril = jnp.asarray(np.tril(np.ones((LANES, LANES), np.float32)), BF16)
    pos = np.arange(t)
    causal = np.where(pos[:, None] <= pos[None, :], 0.0, NEG)
    mask = jnp.asarray(np.stack([np.zeros((t, t)), causal]).astype(np.float32))
    return pl.pallas_call(
        _fox_kernel,
        grid=(batch, nq),
        in_specs=[
            pl.BlockSpec((QK_WIDTH, t), lambda b, i: (0, b * nq + i)),
            pl.BlockSpec((seq, QK_WIDTH), lambda b, i: (b, 0)),
            pl.BlockSpec((VT_ROWS, seq), lambda b, i: (0, b)),
            pl.BlockSpec((seq, LANES), lambda b, i: (b, 0)),
            pl.BlockSpec(tril.shape, lambda b, i: (0, 0)),
            pl.BlockSpec(mask.shape, lambda b, i: (0, 0, 0)),
        ],
        out_specs=pl.BlockSpec((N_HEADS * HEAD_V, t), lambda b, i: (0, b * nq + i)),
        out_shape=jax.ShapeDtypeStruct((N_HEADS * HEAD_V, n), BF16),
        scratch_shapes=[
            pltpu.VMEM((N_HEADS, QK_WIDTH, t), BF16),
            pltpu.VMEM((N_HEADS, t, t), F32),
            pltpu.VMEM((N_HEADS, 1, t), F32),
            pltpu.VMEM((N_HEADS, V_SLAB, t), F32),
            pltpu.VMEM((N_HEADS, seq, LANES), F32),
        ],
        compiler_params=_cparams("parallel", "arbitrary"),
        name="fox_attn",
    )(fqt, fk, fvt, lf, tril, mask)


GLA_LEVELS = (1, 2, 4, 8, 16, 32)
N_LEVELS = len(GLA_LEVELS)


def _gla_tables():
    c = CHUNK
    idx = np.arange(c)
    pair = np.zeros((N_LEVELS + 1, c, c), np.float32)
    for l, m in enumerate(GLA_LEVELS):
        upper = idx % (2 * m) >= m
        same = (idx[:, None] // (2 * m)) == (idx[None, :] // (2 * m))
        pair[l] = same & upper[:, None] & (~upper)[None, :]
    pair[N_LEVELS] = np.eye(c)
    pair = np.tile(pair, (1, N_HEADS, 1))
    tril = np.tril(np.ones((c, c), np.float32))
    return tril, pair


def _block_ref(b, m):
    c, w = b.shape
    if 2 * m >= 8:
        b3 = b.reshape(c // (2 * m), 2 * m, w)
        return jnp.broadcast_to(b3[:, m - 1:m, :], b3.shape).reshape(c, w)
    b3 = b.reshape(c // 8, 8, w)
    sub = lax.broadcasted_iota(jnp.int32, (1, 8, 1), 1)
    ref = jnp.broadcast_to(b3[:, m - 1:m, :], b3.shape)
    for start in range(2 * m, 8, 2 * m):
        pick = jnp.broadcast_to(b3[:, start + m - 1:start + m, :], b3.shape)
        ref = jnp.where(sub >= start, pick, ref)
    return ref.reshape(c, w)


def _gla_kernel(q_ref, k_ref, v_ref, r_ref, a_ref, tril_ref, pair_ref, og_ref, o_ref, state_ref):
    c = CHUNK
    hk = N_HEADS * GLA_K_DIM

    @pl.when(pl.program_id(1) == 0)
    def _():
        state_ref[...] = jnp.zeros_like(state_ref)

    col = lax.broadcasted_iota(jnp.int32, (1, hk), 1)
    head_cols = [col // GLA_K_DIM == h for h in range(N_HEADS)]

    def head_only(x, h):
        return jnp.where(head_cols[h], x, jnp.zeros_like(x))

    def stack_heads(x):
        x = x.astype(BF16)
        return jnp.concatenate([head_only(x, h) for h in range(N_HEADS)], axis=0)

    def chunk(ci, carry):
        r0 = pl.multiple_of(ci * c, c)
        rows = pl.ds(r0, c)
        q = q_ref[rows, :]
        k = k_ref[rows, :]
        v = v_ref[rows, :]
        a_hi, a_mid, a_lo = _split3(a_ref[rows, :])
        tril = tril_ref[...]
        b = _dot(tril, a_hi) + _dot(tril, a_mid) + _dot(tril, a_lo)
        b_last = b[c - 1:c, :]

        attn = pair_ref[N_LEVELS] * _dot_nt(stack_heads(q), k.astype(BF16))
        for l, m in enumerate(GLA_LEVELS):
            ref = _block_ref(b, m)
            ql = q * jnp.exp(jnp.minimum(b - ref, 0.0))
            kl = k * jnp.exp(jnp.minimum(ref - b, 0.0))
            attn = attn + pair_ref[l] * _dot_nt(stack_heads(ql), kl.astype(BF16))
        attn = attn.astype(BF16)

        state_t = state_ref[...]
        o_inter = _dot_nt(stack_heads(q * jnp.exp(b)), state_t.astype(BF16))

        k_dec = (k * jnp.exp(b_last - b)).astype(BF16)
        upd = None
        for h in range(N_HEADS):
            hr = slice(h * c, (h + 1) * c)
            hv = slice(h * GLA_V_DIM, (h + 1) * GLA_V_DIM)
            o = o_inter[hr] + _dot(attn[hr], v[:, hv])
            y = _rms(o, og_ref[...])
            gate = r_ref[rows, hv]
            o_ref[rows, hv] = (y * (gate * jax.nn.sigmoid(gate))).astype(BF16)
            term = _dot_tn(v[:, hv], head_only(k_dec, h))
            upd = term if upd is None else upd + term
        state_ref[...] = state_t * jnp.exp(b_last) + upd
        return carry

    lax.fori_loop(0, q_ref.shape[0] // c, chunk, 0)


def _gla(gq, gk, gv, gr, la, og, batch, seq):
    tg = min(GLA_TILE, seq)
    ng = seq // tg
    n = batch * seq
    tril_np, pair_np = _gla_tables()
    sums = jnp.asarray(tril_np, BF16)
    pair = jnp.asarray(pair_np)
    hk = N_HEADS * GLA_K_DIM
    hv = N_HEADS * GLA_V_DIM
    row = lambda b, i: (b * ng + i, 0)
    return pl.pallas_call(
        _gla_kernel,
        grid=(batch, ng),
        in_specs=[
            pl.BlockSpec((tg, hk), row),
            pl.BlockSpec((tg, hk), row),
            pl.BlockSpec((tg, hv), row),
            pl.BlockSpec((tg, hv), row),
            pl.BlockSpec((tg, hk), row),
            pl.BlockSpec(sums.shape, lambda b, i: (0, 0)),
            pl.BlockSpec(pair.shape, lambda b, i: (0, 0, 0)),
            pl.BlockSpec(og.shape, lambda b, i: (0, 0)),
        ],
        out_specs=pl.BlockSpec((tg, hv), row),
        out_shape=jax.ShapeDtypeStruct((n, hv), BF16),
        scratch_shapes=[pltpu.VMEM((GLA_V_DIM, hk), F32)],
        compiler_params=_cparams("parallel", "arbitrary"),
        name="gla",
    )(gq, gk, gv, gr, la, sums, pair, og)


DIFF_WIDTH = N_HEADS * HEAD_V
GLA_WIDTH = N_HEADS * GLA_V_DIM


def _out_kernel(x_ref, at_ref, g_ref, ct_ref, w_ref, o_ref):
    acc = _dot_tn(at_ref[...], w_ref[0:DIFF_WIDTH, :])
    acc = acc + _dot(g_ref[...], w_ref[DIFF_WIDTH:DIFF_WIDTH + GLA_WIDTH, :])
    acc = acc + _dot_tn(ct_ref[...], w_ref[DIFF_WIDTH + GLA_WIDTH:, :])
    o_ref[...] = x_ref[...] + acc


def _out_proj(x2d, at, g, ct, w):
    n = x2d.shape[0]
    tm = min(TOKEN_TILE, n)
    row = lambda i: (i, 0)
    colblk = lambda i: (0, i)
    return pl.pallas_call(
        _out_kernel,
        grid=(n // tm,),
        in_specs=[pl.BlockSpec((tm, D_MODEL), row), pl.BlockSpec((DIFF_WIDTH, tm), colblk),
                  pl.BlockSpec((tm, GLA_WIDTH), row), pl.BlockSpec((DIFF_WIDTH, tm), colblk),
                  pl.BlockSpec(w.shape, lambda i: (0, 0))],
        out_specs=pl.BlockSpec((tm, D_MODEL), row),
        out_shape=jax.ShapeDtypeStruct((n, D_MODEL), F32),
        compiler_params=_cparams("parallel"),
        name="proj_out",
    )(x2d, at, g, ct, w)


def _in_proj_weights(w_in):
    def zeros(width):
        return jnp.zeros((D_MODEL, width), w_in.dtype)

    w = jnp.concatenate([w_in[:, 256:512], w_in[:, 768:2304], w_in[:, 2576:2832], w_in[:, 2304:2320],
                         w_in[:, IN_WIDTH - N_HEADS:], zeros(LANES - GLA_GATE_RANK - N_HEADS)], axis=1)

    def value_slabs(lo):
        parts = []
        for h in range(N_HEADS):
            parts += [w_in[:, lo + h * HEAD_V:lo + (h + 1) * HEAD_V], zeros(V_SLAB - HEAD_V)]
        return parts

    wt = jnp.concatenate([w_in[:, 0:256]] + value_slabs(512) + [w_in[:, 2320:2576]] + value_slabs(2832), axis=1)
    assert w.shape[1] == PROJ_WIDTH and wt.shape[1] == PROJ_T_ROWS
    return w.astype(BF16), wt.T.astype(BF16)


def _ffn_weights(w13, w2):
    pad = D_FF_PAD - D_FF
    wg = jnp.pad(w13[:, :D_FF], ((0, 0), (0, pad))).astype(BF16)
    wu = jnp.pad(w13[:, D_FF:], ((0, 0), (0, pad))).astype(BF16)
    return wg, wu, jnp.pad(w2, ((0, pad), (0, 0))).astype(BF16)


def _same_group(width):
    g = np.arange(QK_WIDTH) // width
    return jnp.asarray((g[:, None] == g[None, :]).astype(np.float32), BF16)


def kernel(x, ffn1_norm, ffn1_w13, ffn1_w2, mix_norm, w_in, w_out, diff_q_norm, diff_k_norm,
           diff_lambda, diff_out_norm, gla_alpha_w2, gla_alpha_b, gla_out_norm, fox_q_norm,
           fox_k_norm, fox_f_bias, ffn2_norm, ffn2_w13, ffn2_w2):
    batch, seq, _ = x.shape
    x2d = x.reshape(batch * seq, D_MODEL)
    grp32 = _same_group(DIFF_QK_DIM)
    grp64 = _same_group(HEAD_V)
    vone_np = np.zeros((VT_ROWS, 1), np.float32)
    vone_np[HEAD_V::V_SLAB, 0] = 1.0
    vone = jnp.asarray(vone_np)

    for i in range(DEPTH):
        wg, wu, w2p = _ffn_weights(ffn1_w13[i], ffn1_w2[i])
        x2d = _ffn(x2d, ffn1_norm[i].reshape(1, D_MODEL), wg, wu, w2p)

        w, wt = _in_proj_weights(w_in[i])
        kg = jnp.stack([jnp.tile(diff_k_norm[i], QK_WIDTH // DIFF_QK_DIM), jnp.tile(fox_k_norm[i], N_HEADS)])
        qgt = jnp.stack([jnp.tile(diff_q_norm[i], QK_WIDTH // DIFF_QK_DIM) * (DIFF_QK_DIM ** -0.5),
                         jnp.tile(fox_q_norm[i], N_HEADS) * (HEAD_V ** -0.5)], axis=1)
        aw2 = jnp.pad(gla_alpha_w2[i], ((0, LANES - GLA_GATE_RANK), (0, 0)))
        ab = gla_alpha_b[i].reshape(1, -1)
        fb = jnp.pad(fox_f_bias[i], (FOX_GATE_COL, LANES - FOX_GATE_COL - N_HEADS)).reshape(1, LANES)
        (dqt, dk, dvt, gq, gk, gv, gr, la, fqt, fk, fvt, lf) = _proj(
            x2d, mix_norm[i].reshape(1, D_MODEL), w, wt, kg, qgt, grp32, grp64, aw2, ab, fb, vone)

        lam_init = 0.8 - 0.6 * math.exp(-0.3 * i)
        at = _diff_attn(dqt, dk, dvt, diff_lambda[i], diff_out_norm[i].reshape(HEAD_V, 1), lam_init,
                        batch, seq)
        g = _gla(gq, gk, gv, gr, la, gla_out_norm[i].reshape(1, GLA_V_DIM), batch, seq)
        ct = _fox_attn(fqt, fk, fvt, lf, batch, seq)
        x2d = _out_proj(x2d, at, g, ct, w_out[i].astype(BF16))

        wg, wu, w2p = _ffn_weights(ffn2_w13[i], ffn2_w2[i])
        x2d = _ffn(x2d, ffn2_norm[i].reshape(1, D_MODEL), wg, wu, w2p)
    return x2d.reshape(batch, seq, D_MODEL)
```

```python
import functools
import math

import numpy as np
import jax
import jax.numpy as jnp
from jax import lax
from jax.experimental import pallas as pl
from jax.experimental.pallas import tpu as pltpu

F32 = jnp.float32
BF16 = jnp.bfloat16

D_MODEL = 1024
D_FF = 2752
DEPTH = 4
CHUNK = 64
N_HEADS = 4
DIFF_QK_DIM = 32
HEAD_V = 64
GLA_K_DIM = 64
GLA_V_DIM = 128
GLA_GATE_RANK = 16
GLA_TAU = 16.0
NORM_EPS = 1e-6
IN_WIDTH = 3092

LANES = 128
FF_CHUNK = 256
D_FF_PAD = 2816
N_FF_CHUNKS = D_FF_PAD // FF_CHUNK
TOKEN_TILE = 512
ATTN_TILE = 256
GLA_TILE = 512
NEG = -1e30
VMEM_LIMIT = 56 * 1024 * 1024

QK_WIDTH = 256
DIFF_GROUP = 4
FOX_GROUP = 2
V_SLAB = 80
VT_ROWS = N_HEADS * V_SLAB

C_DK, C_GQ, C_GK, C_GV, C_GR, C_FK, C_GATES = 0, 256, 512, 768, 1280, 1792, 2048
PROJ_WIDTH = 2176
FOX_GATE_COL = GLA_GATE_RANK
R_DQ, R_DV, R_FQ, R_FV = 0, 256, 576, 832
PROJ_T_ROWS = 1152

NT_DIMS = (((1,), (1,)), ((), ()))
TN_DIMS = (((0,), (0,)), ((), ()))


def _cparams(*sem):
    return pltpu.CompilerParams(dimension_semantics=sem, vmem_limit_bytes=VMEM_LIMIT)


def _dot(a, b):
    return jnp.dot(a, b, preferred_element_type=F32)


def _dot_nt(a, b):
    return lax.dot_general(a, b, NT_DIMS, preferred_element_type=F32)


def _dot_tn(a, b):
    return lax.dot_general(a, b, TN_DIMS, preferred_element_type=F32)


def _split2(x):
    hi = x.astype(BF16)
    lo = (x - hi.astype(F32)).astype(BF16)
    return hi, lo


def _split3(x):
    hi = x.astype(BF16)
    r = x - hi.astype(F32)
    mid = r.astype(BF16)
    lo = (r - mid.astype(F32)).astype(BF16)
    return hi, mid, lo


def _log_sigmoid(z):
    return jnp.minimum(z, 0.0) - jnp.log(1.0 + jnp.exp(-jnp.abs(z)))


def _rms(x, g):
    ms = jnp.mean(x * x, axis=-1, keepdims=True)
    return x * lax.rsqrt(ms + NORM_EPS) * g


def _ffn_kernel(x_ref, g_ref, wg_ref, wu_ref, w2_ref, o_ref):
    x = x_ref[...]
    h = _rms(x, g_ref[...]).astype(BF16)
    acc = None
    for c in range(N_FF_CHUNKS):
        cols = slice(c * FF_CHUNK, (c + 1) * FF_CHUNK)
        gate = _dot(h, wg_ref[:, cols])
        up = _dot(h, wu_ref[:, cols])
        act = (gate * jax.nn.sigmoid(gate) * up).astype(BF16)
        term = _dot(act, w2_ref[cols, :])
        acc = term if acc is None else acc + term
    o_ref[...] = x + 0.5 * acc


def _ffn(x2d, gain, wg, wu, w2):
    n = x2d.shape[0]
    tm = min(TOKEN_TILE, n)
    const = lambda i: (0, 0)
    return pl.pallas_call(
        _ffn_kernel,
        grid=(n // tm,),
        in_specs=[
            pl.BlockSpec((tm, D_MODEL), lambda i: (i, 0)),
            pl.BlockSpec((1, D_MODEL), const),
            pl.BlockSpec((D_MODEL, D_FF_PAD), const, pipeline_mode=pl.Buffered(1)),
            pl.BlockSpec((D_MODEL, D_FF_PAD), const, pipeline_mode=pl.Buffered(1)),
            pl.BlockSpec((D_FF_PAD, D_MODEL), const, pipeline_mode=pl.Buffered(1)),
        ],
        out_specs=pl.BlockSpec((tm, D_MODEL), lambda i: (i, 0)),
        out_shape=jax.ShapeDtypeStruct((n, D_MODEL), F32),
        compiler_params=_cparams("parallel"),
        name="ffn",
    )(x2d, gain, wg, wu, w2)


def _group_norm_lanes(t, grp, inv_d, gain):
    hi, lo = _split2(t * t)
    ms = (_dot(hi, grp) + _dot(lo, grp)) * inv_d
    return t * lax.rsqrt(ms + NORM_EPS) * gain


def _group_norm_rows(t, d, gain):
    rows, cols = t.shape
    t3 = t.reshape(rows // d, d, cols)
    ms = jnp.sum(t3 * t3, axis=1, keepdims=True) * (1.0 / d)
    return (t3 * lax.rsqrt(ms + NORM_EPS)).reshape(rows, cols) * gain


def _proj_kernel(x_ref, g_ref, w_ref, wt_ref, kg_ref, qgt_ref, grp32_ref, grp64_ref, aw2_ref,
                 ab_ref, fb_ref, vone_ref,
                 dqt_ref, dk_ref, dvt_ref, gq_ref, gk_ref, gv_ref, gr_ref, la_ref,
                 fqt_ref, fk_ref, fvt_ref, lf_ref):
    h = _rms(x_ref[...], g_ref[...]).astype(BF16)

    def proj(lo, width):
        return _dot(h, w_ref[:, lo:lo + width])

    def proj_t(lo, rows):
        return _dot_nt(wt_ref[lo:lo + rows, :], h)

    vone = vone_ref[...]
    dqt_ref[...] = _group_norm_rows(proj_t(R_DQ, QK_WIDTH), DIFF_QK_DIM, qgt_ref[:, 0:1]).astype(BF16)
    dk_ref[...] = _group_norm_lanes(proj(C_DK, QK_WIDTH), grp32_ref[...], 1.0 / DIFF_QK_DIM,
                                    kg_ref[0:1]).astype(BF16)
    dvt_ref[...] = (proj_t(R_DV, VT_ROWS) + vone).astype(BF16)
    gq_ref[...] = proj(C_GQ, 256) * (GLA_K_DIM ** -0.5)
    gk_ref[...] = proj(C_GK, 256)
    gv_ref[...] = proj(C_GV, 512).astype(BF16)
    gr_ref[...] = proj(C_GR, 512)
    fqt_ref[...] = _group_norm_rows(proj_t(R_FQ, QK_WIDTH), HEAD_V, qgt_ref[:, 1:2]).astype(BF16)
    fk_ref[...] = _group_norm_lanes(proj(C_FK, QK_WIDTH), grp64_ref[...], 1.0 / HEAD_V,
                                    kg_ref[1:2]).astype(BF16)
    fvt_ref[...] = (proj_t(R_FV, VT_ROWS) + vone).astype(BF16)

    gates = proj(C_GATES, LANES)
    ga_hi, ga_lo = _split2(gates)
    w_hi, w_lo = _split2(aw2_ref[...])
    z = _dot(ga_hi, w_hi) + _dot(ga_hi, w_lo) + _dot(ga_lo, w_hi) + ab_ref[...]
    la_ref[...] = _log_sigmoid(z) * (1.0 / GLA_TAU)
    lf_ref[...] = _log_sigmoid(gates + fb_ref[...])


def _proj(x2d, gain, w, wt, kg, qgt, grp32, grp64, aw2, ab, fb, vone):
    n = x2d.shape[0]
    tm = min(TOKEN_TILE, n)
    row = lambda i: (i, 0)
    colblk = lambda i: (0, i)
    const = lambda i: (0, 0)

    def full(a):
        return pl.BlockSpec(a.shape, const)

    def tok(width, dt):
        return jax.ShapeDtypeStruct((n, width), dt), pl.BlockSpec((tm, width), row)

    def feat(rows, dt):
        return jax.ShapeDtypeStruct((rows, n), dt), pl.BlockSpec((rows, tm), colblk)

    outs = [feat(QK_WIDTH, BF16), tok(QK_WIDTH, BF16), feat(VT_ROWS, BF16), tok(256, F32), tok(256, F32),
            tok(512, BF16), tok(512, F32), tok(256, F32), feat(QK_WIDTH, BF16), tok(QK_WIDTH, BF16),
            feat(VT_ROWS, BF16), tok(LANES, F32)]
    return pl.pallas_call(
        _proj_kernel,
        grid=(n // tm,),
        in_specs=[pl.BlockSpec((tm, D_MODEL), row), full(gain), full(w), full(wt), full(kg), full(qgt),
                  full(grp32), full(grp64), full(aw2), full(ab), full(fb), full(vone)],
        out_specs=[o[1] for o in outs],
        out_shape=[o[0] for o in outs],
        compiler_params=_cparams("parallel"),
        name="proj_in",
    )(x2d, gain, w, wt, kg, qgt, grp32, grp64, aw2, ab, fb, vone)


def _max_over_rows(u):
    rows = u.shape[0]
    while rows > 8:
        rows //= 2
        u = jnp.maximum(u[:rows], u[rows:])
    return jnp.max(u, axis=0, keepdims=True)


def _attn_pipeline(n_blocks, n_stack, group, scores_of, shift_of, vt_of, u_ref, umax_ref, m_ref, acc_ref):
    t = u_ref.shape[1]
    n_pairs = n_stack // group
    assert n_pairs >= 2
    m_ref[...] = jnp.full_like(m_ref, NEG)
    acc_ref[...] = jnp.zeros_like(acc_ref)

    def issue(j, p):
        u = scores_of(j, p)
        u_ref[p] = u
        umax_ref[p] = _max_over_rows(u)

    issue(0, 0)

    def body(j, carry):
        j_next = jnp.minimum(j + 1, n_blocks - 1)
        for p in range(n_pairs):
            if p + 1 < n_pairs:
                issue(j, p + 1)
            else:
                issue(j_next, 0)
            for e in range(group * p, group * (p + 1)):
                lanes = slice((e % group) * t, (e % group + 1) * t)
                u = u_ref[p, :, lanes]
                c = shift_of(j, e)
                m_old = m_ref[e]
                m_new = jnp.maximum(m_old, umax_ref[p, :, lanes] + c)
                alpha = jnp.exp(m_old - m_new)
                p_t = jnp.exp(u - (m_new - c)).astype(BF16)
                acc_ref[e] = alpha * acc_ref[e] + _dot(vt_of(j, e), p_t)
                m_ref[e] = m_new
        return carry

    lax.fori_loop(0, n_blocks, body, 0)


def _stack_queries(qt, qst_ref, n_stack, group):
    depth = qt.shape[0] // n_stack
    row = lax.broadcasted_iota(jnp.int32, (qt.shape[0], 1), 0)
    zero = jnp.zeros_like(qt)
    for p in range(n_stack // group):
        qst_ref[p] = jnp.concatenate(
            [jnp.where(row // depth == e, qt, zero) for e in range(group * p, group * (p + 1))], axis=1)


def _softmax_out(acc):
    return acc[0:HEAD_V] / acc[HEAD_V:HEAD_V + 1]


DIFF_SLOPES = tuple(2.0 ** (-8.0 * (h + 1) / N_HEADS) for h in range(N_HEADS))


def _diff_tables(t):
    sl = np.arange(t)[:, None]
    tl = np.arange(t)[None, :]
    allowed = (sl // CHUNK) <= (tl // CHUNK)
    past = [slope * np.broadcast_to(sl, (t, t)) for slope in DIFF_SLOPES]
    diag = [np.where(allowed, slope * (tl - np.abs(tl - sl)), NEG) for slope in DIFF_SLOPES]
    return np.stack(past + diag).astype(np.float32)


def _diff_kernel(lam_init, qt_ref, k_ref, vt_ref, bias_ref, lp_ref, og_ref, o_ref,
                 qst_ref, u_ref, umax_ref, m_ref, acc_ref):
    t = qt_ref.shape[1]
    n_stack = 2 * N_HEADS
    i = pl.program_id(1)
    _stack_queries(qt_ref[...], qst_ref, n_stack, DIFF_GROUP)

    def keys(j):
        return pl.ds(pl.multiple_of(j * t, t), t)

    def scores(j, p):
        first = jnp.where(j == i, N_HEADS, 0) + p * (DIFF_GROUP // 2)
        bias = []
        for h in range(DIFF_GROUP // 2):
            bias += [bias_ref[first + h]] * 2
        return _dot(k_ref[keys(j), :], qst_ref[p]) + jnp.concatenate(bias, axis=1)

    def shift(j, e):
        return DIFF_SLOPES[e // 2] * ((j - i) * t).astype(F32)

    def vt(j, e):
        return vt_ref[(e // 2) * V_SLAB:(e // 2 + 1) * V_SLAB, keys(j)]

    _attn_pipeline(i + 1, n_stack, DIFF_GROUP, scores, shift, vt, u_ref, umax_ref, m_ref, acc_ref)

    lp = lp_ref[...]
    lam = (jnp.exp(jnp.sum(lp[0:1] * lp[1:2], keepdims=True))
           - jnp.exp(jnp.sum(lp[2:3] * lp[3:4], keepdims=True)) + lam_init)
    for h in range(N_HEADS):
        a = _softmax_out(acc_ref[2 * h]) - lam * _softmax_out(acc_ref[2 * h + 1])
        ms = jnp.sum(a * a, axis=0, keepdims=True) * (1.0 / HEAD_V)
        y = a * lax.rsqrt(ms + NORM_EPS) * og_ref[...] * (1.0 - lam_init)
        o_ref[h * HEAD_V:(h + 1) * HEAD_V, :] = y.astype(BF16)


def _diff_attn(dqt, dk, dvt, lp, og, lam_init, batch, seq):
    t = min(ATTN_TILE, seq)
    nq = seq // t
    n = batch * seq
    bias = jnp.asarray(_diff_tables(t))
    n_stack = 2 * N_HEADS
    return pl.pallas_call(
        functools.partial(_diff_kernel, lam_init),
        grid=(batch, nq),
        in_specs=[
            pl.BlockSpec((QK_WIDTH, t), lambda b, i: (0, b * nq + i)),
            pl.BlockSpec((seq, QK_WIDTH), lambda b, i: (b, 0)),
            pl.BlockSpec((VT_ROWS, seq), lambda b, i: (0, b)),
            pl.BlockSpec(bias.shape, lambda b, i: (0, 0, 0)),
            pl.BlockSpec(lp.shape, lambda b, i: (0, 0)),
            pl.BlockSpec(og.shape, lambda b, i: (0, 0)),
        ],
        out_specs=pl.BlockSpec((N_HEADS * HEAD_V, t), lambda b, i: (0, b * nq + i)),
        out_shape=jax.ShapeDtypeStruct((N_HEADS * HEAD_V, n), BF16),
        scratch_shapes=[
            pltpu.VMEM((n_stack // DIFF_GROUP, QK_WIDTH, DIFF_GROUP * t), BF16),
            pltpu.VMEM((n_stack // DIFF_GROUP, t, DIFF_GROUP * t), F32),
            pltpu.VMEM((n_stack // DIFF_GROUP, 1, DIFF_GROUP * t), F32),
            pltpu.VMEM((n_stack, 1, t), F32),
            pltpu.VMEM((n_stack, V_SLAB, t), F32),
        ],
        compiler_params=_cparams("parallel", "arbitrary"),
        name="diff_attn",
    )(dqt, dk, dvt, bias, lp, og)


def _fox_kernel(qt_ref, k_ref, vt_ref, lf_ref, tril_ref, mask_ref, o_ref,
                qst_ref, u_ref, umax_ref, m_ref, acc_ref, fneg_ref):
    t = qt_ref.shape[1]
    seq = k_ref.shape[0]
    i = pl.program_id(1)

    @pl.when(i == 0)
    def _():
        tril = tril_ref[...]
        for blk in range(seq // LANES):
            rows = slice(blk * LANES, (blk + 1) * LANES)
            x = jnp.concatenate(
                [jnp.broadcast_to(lf_ref[rows, FOX_GATE_COL + h:FOX_GATE_COL + h + 1], (LANES, LANES))
                 for h in range(N_HEADS)], axis=1)
            hi, mid, lo = _split3(x)
            f = _dot(tril, hi) + _dot(tril, mid) + _dot(tril, lo)
            for h in range(N_HEADS):
                fneg_ref[h, rows, :] = f[:, h * LANES:(h + 1) * LANES]
        for h in range(N_HEADS):
            carry = jnp.zeros((1, LANES), F32)
            for blk in range(seq // LANES):
                rows = slice(blk * LANES, (blk + 1) * LANES)
                f = fneg_ref[h, rows, :] + carry
                fneg_ref[h, rows, :] = -f
                carry = f[LANES - 1:LANES, :]

    _stack_queries(qt_ref[...], qst_ref, N_HEADS, FOX_GROUP)

    def keys(j):
        return pl.ds(pl.multiple_of(j * t, t), t)

    def scores(j, p):
        mask = mask_ref[jnp.where(j == i, 1, 0)]
        bias = [jnp.concatenate([fneg_ref[e, keys(j), :]] * (t // LANES), axis=1) + mask
                for e in range(FOX_GROUP * p, FOX_GROUP * (p + 1))]
        return _dot(k_ref[keys(j), :], qst_ref[p]) + jnp.concatenate(bias, axis=1)

    def vt(j, e):
        return vt_ref[e * V_SLAB:(e + 1) * V_SLAB, keys(j)]

    _attn_pipeline(i + 1, N_HEADS, FOX_GROUP, scores, lambda j, e: 0.0, vt, u_ref, umax_ref, m_ref, acc_ref)

    for h in range(N_HEADS):
        o_ref[h * HEAD_V:(h + 1) * HEAD_V, :] = _softmax_out(acc_ref[h]).astype(BF16)


def _fox_attn(fqt, fk, fvt, lf, batch, seq):
    t = min(ATTN_TILE, seq)
    nq = seq // t
    n = batch * seq
    tril = jnp.asarray(np.tril(np.ones((LANES, LANES), np.float32)), BF16)
    pos = np.arange(t)
    causal = np.where(pos[:, None] <= pos[None, :], 0.0, NEG)
    mask = jnp.asarray(np.stack([np.zeros((t, t)), causal]).astype(np.float32))
    return pl.pallas_call(
        _fox_kernel,
        grid=(batch, nq),
        in_specs=[
            pl.BlockSpec((QK_WIDTH, t), lambda b, i: (0, b * nq + i)),
            pl.BlockSpec((seq, QK_WIDTH), lambda b, i: (b, 0)),
            pl.BlockSpec((VT_ROWS, seq), lambda b, i: (0, b)),
            pl.BlockSpec((seq, LANES), lambda b, i: (b, 0)),
            pl.BlockSpec(tril.shape, lambda b, i: (0, 0)),
            pl.BlockSpec(mask.shape, lambda b, i: (0, 0, 0)),
        ],
        out_specs=pl.BlockSpec((N_HEADS * HEAD_V, t), lambda b, i: (0, b * nq + i)),
        out_shape=jax.ShapeDtypeStruct((N_HEADS * HEAD_V, n), BF16),
        scratch_shapes=[
            pltpu.VMEM((N_HEADS // FOX_GROUP, QK_WIDTH, FOX_GROUP * t), BF16),
            pltpu.VMEM((N_HEADS // FOX_GROUP, t, FOX_GROUP * t), F32),
            pltpu.VMEM((N_HEADS // FOX_GROUP, 1, FOX_GROUP * t), F32),
            pltpu.VMEM((N_HEADS, 1, t), F32),
            pltpu.VMEM((N_HEADS, V_SLAB, t), F32),
            pltpu.VMEM((N_HEADS, seq, LANES), F32),
        ],
        compiler_params=_cparams("parallel", "arbitrary"),
        name="fox_attn",
    )(fqt, fk, fvt, lf, tril, mask)


GLA_LEVELS = (1, 2, 4, 8, 16, 32)
N_LEVELS = len(GLA_LEVELS)


def _gla_tables():
    c = CHUNK
    idx = np.arange(c)
    pair = np.zeros((N_LEVELS + 1, c, c), np.float32)
    for l, m in enumerate(GLA_LEVELS):
        upper = idx % (2 * m) >= m
        same = (idx[:, None] // (2 * m)) == (idx[None, :] // (2 * m))
        pair[l] = same & upper[:, None] & (~upper)[None, :]
    pair[N_LEVELS] = np.eye(c)
    pair = np.tile(pair, (1, N_HEADS, 1))
    tril = np.tril(np.ones((c, c), np.float32))
    return tril, pair


def _block_ref(b, m):
    c, w = b.shape
    if 2 * m >= 8:
        b3 = b.reshape(c // (2 * m), 2 * m, w)
        return jnp.broadcast_to(b3[:, m - 1:m, :], b3.shape).reshape(c, w)
    b3 = b.reshape(c // 8, 8, w)
    sub = lax.broadcasted_iota(jnp.int32, (1, 8, 1), 1)
    ref = jnp.broadcast_to(b3[:, m - 1:m, :], b3.shape)
    for start in range(2 * m, 8, 2 * m):
        pick = jnp.broadcast_to(b3[:, start + m - 1:start + m, :], b3.shape)
        ref = jnp.where(sub >= start, pick, ref)
    return ref.reshape(c, w)


def _gla_kernel(q_ref, k_ref, v_ref, r_ref, a_ref, tril_ref, pair_ref, og_ref, o_ref, state_ref):
    c = CHUNK
    hk = N_HEADS * GLA_K_DIM

    @pl.when(pl.program_id(1) == 0)
    def _():
        state_ref[...] = jnp.zeros_like(state_ref)

    col = lax.broadcasted_iota(jnp.int32, (1, hk), 1)
    head_cols = [col // GLA_K_DIM == h for h in range(N_HEADS)]

    def head_only(x, h):
        return jnp.where(head_cols[h], x, jnp.zeros_like(x))

    def stack_heads(x):
        x = x.astype(BF16)
        return jnp.concatenate([head_only(x, h) for h in range(N_HEADS)], axis=0)

    def chunk(ci, carry):
        r0 = pl.multiple_of(ci * c, c)
        rows = pl.ds(r0, c)
        q = q_ref[rows, :]
        k = k_ref[rows, :]
        v = v_ref[rows, :]
        a_hi, a_mid, a_lo = _split3(a_ref[rows, :])
        tril = tril_ref[...]
        b = _dot(tril, a_hi) + _dot(tril, a_mid) + _dot(tril, a_lo)
        b_last = b[c - 1:c, :]

        attn = pair_ref[N_LEVELS] * _dot_nt(stack_heads(q), k.astype(BF16))
        for l, m in enumerate(GLA_LEVELS):
            ref = _block_ref(b, m)
            ql = q * jnp.exp(jnp.minimum(b - ref, 0.0))
            kl = k * jnp.exp(jnp.minimum(ref - b, 0.0))
            attn = attn + pair_ref[l] * _dot_nt(stack_heads(ql), kl.astype(BF16))
        attn = attn.astype(BF16)

        state_t = state_ref[...]
        o_inter = _dot_nt(stack_heads(q * jnp.exp(b)), state_t.astype(BF16))

        k_dec = (k * jnp.exp(b_last - b)).astype(BF16)
        upd = None
        for h in range(N_HEADS):
            hr = slice(h * c, (h + 1) * c)
            hv = slice(h * GLA_V_DIM, (h + 1) * GLA_V_DIM)
            o = o_inter[hr] + _dot(attn[hr], v[:, hv])
            y = _rms(o, og_ref[...])
            gate = r_ref[rows, hv]
            o_ref[rows, hv] = (y * (gate * jax.nn.sigmoid(gate))).astype(BF16)
            term = _dot_tn(v[:, hv], head_only(k_dec, h))
            upd = term if upd is None else upd + term
        state_ref[...] = state_t * jnp.exp(b_last) + upd
        return carry

    lax.fori_loop(0, q_ref.shape[0] // c, chunk, 0)


def _gla(gq, gk, gv, gr, la, og, batch, seq):
    tg = min(GLA_TILE, seq)
    ng = seq // tg
    n = batch * seq
    tril_np, pair_np = _gla_tables()
    sums = jnp.asarray(tril_np, BF16)
    pair = jnp.asarray(pair_np)
    hk = N_HEADS * GLA_K_DIM
    hv = N_HEADS * GLA_V_DIM
    row = lambda b, i: (b * ng + i, 0)
    return pl.pallas_call(
        _gla_kernel,
        grid=(batch, ng),
        in_specs=[
            pl.BlockSpec((tg, hk), row),
            pl.BlockSpec((tg, hk), row),
            pl.BlockSpec((tg, hv), row),
            pl.BlockSpec((tg, hv), row),
            pl.BlockSpec((tg, hk), row),
            pl.BlockSpec(sums.shape, lambda b, i: (0, 0)),
            pl.BlockSpec(pair.shape, lambda b, i: (0, 0, 0)),
            pl.BlockSpec(og.shape, lambda b, i: (0, 0)),
        ],
        out_specs=pl.BlockSpec((tg, hv), row),
        out_shape=jax.ShapeDtypeStruct((n, hv), BF16),
        scratch_shapes=[pltpu.VMEM((GLA_V_DIM, hk), F32)],
        compiler_params=_cparams("parallel", "arbitrary"),
        name="gla",
    )(gq, gk, gv, gr, la, sums, pair, og)


DIFF_WIDTH = N_HEADS * HEAD_V
GLA_WIDTH = N_HEADS * GLA_V_DIM


def _out_kernel(x_ref, at_ref, g_ref, ct_ref, w_ref, o_ref):
    acc = _dot_tn(at_ref[...], w_ref[0:DIFF_WIDTH, :])
    acc = acc + _dot(g_ref[...], w_ref[DIFF_WIDTH:DIFF_WIDTH + GLA_WIDTH, :])
    acc = acc + _dot_tn(ct_ref[...], w_ref[DIFF_WIDTH + GLA_WIDTH:, :])
    o_ref[...] = x_ref[...] + acc


def _out_proj(x2d, at, g, ct, w):
    n = x2d.shape[0]
    tm = min(TOKEN_TILE, n)
    row = lambda i: (i, 0)
    colblk = lambda i: (0, i)
    return pl.pallas_call(
        _out_kernel,
        grid=(n // tm,),
        in_specs=[pl.BlockSpec((tm, D_MODEL), row), pl.BlockSpec((DIFF_WIDTH, tm), colblk),
                  pl.BlockSpec((tm, GLA_WIDTH), row), pl.BlockSpec((DIFF_WIDTH, tm), colblk),
                  pl.BlockSpec(w.shape, lambda i: (0, 0))],
        out_specs=pl.BlockSpec((tm, D_MODEL), row),
        out_shape=jax.ShapeDtypeStruct((n, D_MODEL), F32),
        compiler_params=_cparams("parallel"),
        name="proj_out",
    )(x2d, at, g, ct, w)


def _in_proj_weights(w_in):
    def zeros(width):
        return jnp.zeros((D_MODEL, width), w_in.dtype)

    w = jnp.concatenate([w_in[:, 256:512], w_in[:, 768:2304], w_in[:, 2576:2832], w_in[:, 2304:2320],
                         w_in[:, IN_WIDTH - N_HEADS:], zeros(LANES - GLA_GATE_RANK - N_HEADS)], axis=1)

    def value_slabs(lo):
        parts = []
        for h in range(N_HEADS):
            parts += [w_in[:, lo + h * HEAD_V:lo + (h + 1) * HEAD_V], zeros(V_SLAB - HEAD_V)]
        return parts

    wt = jnp.concatenate([w_in[:, 0:256]] + value_slabs(512) + [w_in[:, 2320:2576]] + value_slabs(2832), axis=1)
    assert w.shape[1] == PROJ_WIDTH and wt.shape[1] == PROJ_T_ROWS
    return w.astype(BF16), wt.T.astype(BF16)


def _ffn_weights(w13, w2):
    pad = D_FF_PAD - D_FF
    wg = jnp.pad(w13[:, :D_FF], ((0, 0), (0, pad))).astype(BF16)
    wu = jnp.pad(w13[:, D_FF:], ((0, 0), (0, pad))).astype(BF16)
    return wg, wu, jnp.pad(w2, ((0, pad), (0, 0))).astype(BF16)


def _same_group(width):
    g = np.arange(QK_WIDTH) // width
    return jnp.asarray((g[:, None] == g[None, :]).astype(np.float32), BF16)


def kernel(x, ffn1_norm, ffn1_w13, ffn1_w2, mix_norm, w_in, w_out, diff_q_norm, diff_k_norm,
           diff_lambda, diff_out_norm, gla_alpha_w2, gla_alpha_b, gla_out_norm, fox_q_norm,
           fox_k_norm, fox_f_bias, ffn2_norm, ffn2_w13, ffn2_w2):
    batch, seq, _ = x.shape
    x2d = x.reshape(batch * seq, D_MODEL)
    grp32 = _same_group(DIFF_QK_DIM)
    grp64 = _same_group(HEAD_V)
    vone_np = np.zeros((VT_ROWS, 1), np.float32)
    vone_np[HEAD_V::V_SLAB, 0] = 1.0
    vone = jnp.asarray(vone_np)

    for i in range(DEPTH):
        wg, wu, w2p = _ffn_weights(ffn1_w13[i], ffn1_w2[i])
        x2d = _ffn(x2d, ffn1_norm[i].reshape(1, D_MODEL), wg, wu, w2p)

        w, wt = _in_proj_weights(w_in[i])
        kg = jnp.stack([jnp.tile(diff_k_norm[i], QK_WIDTH // DIFF_QK_DIM), jnp.tile(fox_k_norm[i], N_HEADS)])
        qgt = jnp.stack([jnp.tile(diff_q_norm[i], QK_WIDTH // DIFF_QK_DIM) * (DIFF_QK_DIM ** -0.5),
                         jnp.tile(fox_q_norm[i], N_HEADS) * (HEAD_V ** -0.5)], axis=1)
        aw2 = jnp.pad(gla_alpha_w2[i], ((0, LANES - GLA_GATE_RANK), (0, 0)))
        ab = gla_alpha_b[i].reshape(1, -1)
        fb = jnp.pad(fox_f_bias[i], (FOX_GATE_COL, LANES - FOX_GATE_COL - N_HEADS)).reshape(1, LANES)
        (dqt, dk, dvt, gq, gk, gv, gr, la, fqt, fk, fvt, lf) = _proj(
            x2d, mix_norm[i].reshape(1, D_MODEL), w, wt, kg, qgt, grp32, grp64, aw2, ab, fb, vone)

        lam_init = 0.8 - 0.6 * math.exp(-0.3 * i)
        at = _diff_attn(dqt, dk, dvt, diff_lambda[i], diff_out_norm[i].reshape(HEAD_V, 1), lam_init,
                        batch, seq)
        g = _gla(gq, gk, gv, gr, la, gla_out_norm[i].reshape(1, GLA_V_DIM), batch, seq)
        ct = _fox_attn(fqt, fk, fvt, lf, batch, seq)
        x2d = _out_proj(x2d, at, g, ct, w_out[i].astype(BF16))

        wg, wu, w2p = _ffn_weights(ffn2_w13[i], ffn2_w2[i])
        x2d = _ffn(x2d, ffn2_norm[i].reshape(1, D_MODEL), wg, wu, w2p)
    return x2d.reshape(batch, seq, D_MODEL)
```

```python
import functools
import math

import numpy as np
import jax
import jax.numpy as jnp
from jax import lax
from jax.experimental import pallas as pl
from jax.experimental.pallas import tpu as pltpu

F32 = jnp.float32
BF16 = jnp.bfloat16

D_MODEL = 1024
D_FF = 2752
DEPTH = 4
CHUNK = 64
N_HEADS = 4
DIFF_QK_DIM = 32
HEAD_V = 64
GLA_K_DIM = 64
GLA_V_DIM = 128
GLA_GATE_RANK = 16
GLA_TAU = 16.0
NORM_EPS = 1e-6
IN_WIDTH = 3092

LANES = 128
FF_CHUNK = 256
D_FF_PAD = 2816
N_FF_CHUNKS = D_FF_PAD // FF_CHUNK
TOKEN_TILE = 512
ATTN_TILE = 256
GLA_TILE = 512
GLA_SEQS = 4
NEG = -1e30
LOG2E = math.log2(math.e)
VMEM_LIMIT = 56 * 1024 * 1024

QK_WIDTH = 256
DIFF_GROUP = 4
FOX_GROUP = 2
V_SLAB = 80
VT_ROWS = N_HEADS * V_SLAB

C_DK, C_GQ, C_GK, C_GV, C_GR, C_FK, C_GATES = 0, 256, 512, 768, 1280, 1792, 2048
PROJ_WIDTH = 2176
FOX_GATE_COL = GLA_GATE_RANK
R_DQ, R_DV, R_FQ, R_FV = 0, 256, 576, 832
PROJ_T_ROWS = 1152

NT_DIMS = (((1,), (1,)), ((), ()))
TN_DIMS = (((0,), (0,)), ((), ()))


def _cparams(*sem):
    return pltpu.CompilerParams(dimension_semantics=sem, vmem_limit_bytes=VMEM_LIMIT)


def _dot(a, b):
    return jnp.dot(a, b, preferred_element_type=F32)


def _dot_nt(a, b):
    return lax.dot_general(a, b, NT_DIMS, preferred_element_type=F32)


def _dot_tn(a, b):
    return lax.dot_general(a, b, TN_DIMS, preferred_element_type=F32)


def _split2(x):
    hi = x.astype(BF16)
    lo = (x - hi.astype(F32)).astype(BF16)
    return hi, lo


def _split3(x):
    hi = x.astype(BF16)
    r = x - hi.astype(F32)
    mid = r.astype(BF16)
    lo = (r - mid.astype(F32)).astype(BF16)
    return hi, mid, lo


def _log_sigmoid(z):
    return jnp.minimum(z, 0.0) - jnp.log(1.0 + jnp.exp(-jnp.abs(z)))


def _rms(x, g):
    ms = jnp.mean(x * x, axis=-1, keepdims=True)
    return x * lax.rsqrt(ms + NORM_EPS) * g


def _ffn_kernel(x_ref, g_ref, wg_ref, wu_ref, w2_ref, o_ref):
    x = x_ref[...]
    h = _rms(x, g_ref[...]).astype(BF16)
    acc = None
    for c in range(N_FF_CHUNKS):
        cols = slice(c * FF_CHUNK, (c + 1) * FF_CHUNK)
        gate = _dot(h, wg_ref[:, cols])
        up = _dot(h, wu_ref[:, cols])
        act = (gate * jax.nn.sigmoid(gate) * up).astype(BF16)
        term = _dot(act, w2_ref[cols, :])
        acc = term if acc is None else acc + term
    o_ref[...] = x + 0.5 * acc


def _ffn(x2d, gain, wg, wu, w2):
    n = x2d.shape[0]
    tm = min(TOKEN_TILE, n)
    const = lambda i: (0, 0)
    return pl.pallas_call(
        _ffn_kernel,
        grid=(n // tm,),
        in_specs=[
            pl.BlockSpec((tm, D_MODEL), lambda i: (i, 0)),
            pl.BlockSpec((1, D_MODEL), const),
            pl.BlockSpec((D_MODEL, D_FF_PAD), const, pipeline_mode=pl.Buffered(1)),
            pl.BlockSpec((D_MODEL, D_FF_PAD), const, pipeline_mode=pl.Buffered(1)),
            pl.BlockSpec((D_FF_PAD, D_MODEL), const, pipeline_mode=pl.Buffered(1)),
        ],
        out_specs=pl.BlockSpec((tm, D_MODEL), lambda i: (i, 0)),
        out_shape=jax.ShapeDtypeStruct((n, D_MODEL), F32),
        compiler_params=_cparams("parallel"),
        name="ffn",
    )(x2d, gain, wg, wu, w2)


def _group_norm_lanes(t, grp, inv_d, gain):
    hi, lo = _split2(t * t)
    ms = (_dot(hi, grp) + _dot(lo, grp)) * inv_d
    return t * lax.rsqrt(ms + NORM_EPS) * gain


def _group_norm_rows(t, d, gain):
    rows, cols = t.shape
    t3 = t.reshape(rows // d, d, cols)
    ms = jnp.sum(t3 * t3, axis=1, keepdims=True) * (1.0 / d)
    return (t3 * lax.rsqrt(ms + NORM_EPS)).reshape(rows, cols) * gain


def _proj_kernel(x_ref, g_ref, w_ref, wt_ref, kg_ref, qgt_ref, grp32_ref, grp64_ref, aw2_ref,
                 ab_ref, fb_ref, vone_ref,
                 dqt_ref, dk_ref, dvt_ref, gq_ref, gk_ref, gv_ref, gr_ref, la_ref,
                 fqt_ref, fk_ref, fvt_ref, lf_ref):
    h = _rms(x_ref[...], g_ref[...]).astype(BF16)

    def proj(lo, width):
        return _dot(h, w_ref[:, lo:lo + width])

    def proj_t(lo, rows):
        return _dot_nt(wt_ref[lo:lo + rows, :], h)

    vone = vone_ref[...]
    dqt_ref[...] = _group_norm_rows(proj_t(R_DQ, QK_WIDTH), DIFF_QK_DIM, qgt_ref[:, 0:1]).astype(BF16)
    dk_ref[...] = _group_norm_lanes(proj(C_DK, QK_WIDTH), grp32_ref[...], 1.0 / DIFF_QK_DIM,
                                    kg_ref[0:1]).astype(BF16)
    dvt_ref[...] = (proj_t(R_DV, VT_ROWS) + vone).astype(BF16)
    gq_ref[...] = proj(C_GQ, 256) * (GLA_K_DIM ** -0.5)
    gk_ref[...] = proj(C_GK, 256)
    gv_ref[...] = proj(C_GV, 512).astype(BF16)
    gr_ref[...] = proj(C_GR, 512)
    fqt_ref[...] = _group_norm_rows(proj_t(R_FQ, QK_WIDTH), HEAD_V, qgt_ref[:, 1:2]).astype(BF16)
    fk_ref[...] = _group_norm_lanes(proj(C_FK, QK_WIDTH), grp64_ref[...], 1.0 / HEAD_V,
                                    kg_ref[1:2]).astype(BF16)
    fvt_ref[...] = (proj_t(R_FV, VT_ROWS) + vone).astype(BF16)

    gates = proj(C_GATES, LANES)
    ga_hi, ga_lo = _split2(gates)
    w_hi, w_lo = _split2(aw2_ref[...])
    z = _dot(ga_hi, w_hi) + _dot(ga_hi, w_lo) + _dot(ga_lo, w_hi) + ab_ref[...]
    la_ref[...] = _log_sigmoid(z) * (1.0 / GLA_TAU)
    lf_ref[...] = _log_sigmoid(gates + fb_ref[...])


def _proj(x2d, gain, w, wt, kg, qgt, grp32, grp64, aw2, ab, fb, vone):
    n = x2d.shape[0]
    tm = min(TOKEN_TILE, n)
    row = lambda i: (i, 0)
    colblk = lambda i: (0, i)
    const = lambda i: (0, 0)

    def full(a):
        return pl.BlockSpec(a.shape, const)

    def tok(width, dt):
        return jax.ShapeDtypeStruct((n, width), dt), pl.BlockSpec((tm, width), row)

    def feat(rows, dt):
        return jax.ShapeDtypeStruct((rows, n), dt), pl.BlockSpec((rows, tm), colblk)

    outs = [feat(QK_WIDTH, BF16), tok(QK_WIDTH, BF16), feat(VT_ROWS, BF16), tok(256, F32), tok(256, F32),
            tok(512, BF16), tok(512, F32), tok(256, F32), feat(QK_WIDTH, BF16), tok(QK_WIDTH, BF16),
            feat(VT_ROWS, BF16), tok(LANES, F32)]
    return pl.pallas_call(
        _proj_kernel,
        grid=(n // tm,),
        in_specs=[pl.BlockSpec((tm, D_MODEL), row), full(gain), full(w), full(wt), full(kg), full(qgt),
                  full(grp32), full(grp64), full(aw2), full(ab), full(fb), full(vone)],
        out_specs=[o[1] for o in outs],
        out_shape=[o[0] for o in outs],
        compiler_params=_cparams("parallel"),
        name="proj_in",
    )(x2d, gain, w, wt, kg, qgt, grp32, grp64, aw2, ab, fb, vone)


def _max_over_rows(u):
    rows = u.shape[0]
    while rows > 8:
        rows //= 2
        u = jnp.maximum(u[:rows], u[rows:])
    return jnp.max(u, axis=0, keepdims=True)


def _attn_pipeline(n_blocks, n_stack, group, scores_of, shift_of, vt_of, u_ref, umax_ref, m_ref, acc_ref):
    t = u_ref.shape[1]
    n_pairs = n_stack // group
    assert n_pairs >= 2
    m_ref[...] = jnp.full_like(m_ref, NEG)
    acc_ref[...] = jnp.zeros_like(acc_ref)

    def issue(j, p):
        u = scores_of(j, p)
        u_ref[p] = u
        umax_ref[p] = _max_over_rows(u)

    issue(0, 0)

    def body(j, carry):
        j_next = jnp.minimum(j + 1, n_blocks - 1)
        for p in range(n_pairs):
            if p + 1 < n_pairs:
                issue(j, p + 1)
            else:
                issue(j_next, 0)
            for e in range(group * p, group * (p + 1)):
                lanes = slice((e % group) * t, (e % group + 1) * t)
                u = u_ref[p, :, lanes]
                c = shift_of(j, e)
                m_old = m_ref[e]
                m_new = jnp.maximum(m_old, umax_ref[p, :, lanes] + c)
                alpha = jnp.exp2(m_old - m_new)
                p_t = jnp.exp2(u - (m_new - c)).astype(BF16)
                acc_ref[e] = alpha * acc_ref[e] + _dot(vt_of(j, e), p_t)
                m_ref[e] = m_new
        return carry

    lax.fori_loop(0, n_blocks, body, 0)


def _stack_queries(qt, qst_ref, n_stack, group):
    depth = qt.shape[0] // n_stack
    row = lax.broadcasted_iota(jnp.int32, (qt.shape[0], 1), 0)
    zero = jnp.zeros_like(qt)
    for p in range(n_stack // group):
        qst_ref[p] = jnp.concatenate(
            [jnp.where(row // depth == e, qt, zero) for e in range(group * p, group * (p + 1))], axis=1)


def _softmax_out(acc):
    return acc[0:HEAD_V] / acc[HEAD_V:HEAD_V + 1]


DIFF_SLOPES = tuple(2.0 ** (-8.0 * (h + 1) / N_HEADS) for h in range(N_HEADS))


def _diff_tables(t):
    sl = np.arange(t)[:, None]
    tl = np.arange(t)[None, :]
    allowed = (sl // CHUNK) <= (tl // CHUNK)
    past = [LOG2E * slope * np.broadcast_to(sl, (t, t)) for slope in DIFF_SLOPES]
    diag = [np.where(allowed, LOG2E * slope * (tl - np.abs(tl - sl)), NEG) for slope in DIFF_SLOPES]
    return np.stack(past + diag).astype(np.float32)


def _diff_kernel(lam_init, qt_ref, k_ref, vt_ref, bias_ref, lp_ref, og_ref, o_ref,
                 qst_ref, u_ref, umax_ref, m_ref, acc_ref):
    t = qt_ref.shape[1]
    n_stack = 2 * N_HEADS
    i = pl.program_id(1)
    _stack_queries(qt_ref[...], qst_ref, n_stack, DIFF_GROUP)

    def keys(j):
        return pl.ds(pl.multiple_of(j * t, t), t)

    def scores(j, p):
        first = jnp.where(j == i, N_HEADS, 0) + p * (DIFF_GROUP // 2)
        bias = []
        for h in range(DIFF_GROUP // 2):
            bias += [bias_ref[first + h]] * 2
        return _dot(k_ref[keys(j), :], qst_ref[p]) + jnp.concatenate(bias, axis=1)

    def shift(j, e):
        return (LOG2E * DIFF_SLOPES[e // 2]) * ((j - i) * t).astype(F32)

    def vt(j, e):
        return vt_ref[(e // 2) * V_SLAB:(e // 2 + 1) * V_SLAB, keys(j)]

    _attn_pipeline(i + 1, n_stack, DIFF_GROUP, scores, shift, vt, u_ref, umax_ref, m_ref, acc_ref)

    lp = lp_ref[...]
    lam = (jnp.exp(jnp.sum(lp[0:1] * lp[1:2], keepdims=True))
           - jnp.exp(jnp.sum(lp[2:3] * lp[3:4], keepdims=True)) + lam_init)
    for h in range(N_HEADS):
        a = _softmax_out(acc_ref[2 * h]) - lam * _softmax_out(acc_ref[2 * h + 1])
        ms = jnp.sum(a * a, axis=0, keepdims=True) * (1.0 / HEAD_V)
        y = a * lax.rsqrt(ms + NORM_EPS) * og_ref[...] * (1.0 - lam_init)
        o_ref[h * HEAD_V:(h + 1) * HEAD_V, :] = y.astype(BF16)


def _diff_attn(dqt, dk, dvt, lp, og, lam_init, batch, seq):
    t = min(ATTN_TILE, seq)
    nq = seq // t
    n = batch * seq
    bias = jnp.asarray(_diff_tables(t))
    n_stack = 2 * N_HEADS
    return pl.pallas_call(
        functools.partial(_diff_kernel, lam_init),
        grid=(batch, nq),
        in_specs=[
            pl.BlockSpec((QK_WIDTH, t), lambda b, i: (0, b * nq + i)),
            pl.BlockSpec((seq, QK_WIDTH), lambda b, i: (b, 0)),
            pl.BlockSpec((VT_ROWS, seq), lambda b, i: (0, b)),
            pl.BlockSpec(bias.shape, lambda b, i: (0, 0, 0)),
            pl.BlockSpec(lp.shape, lambda b, i: (0, 0)),
            pl.BlockSpec(og.shape, lambda b, i: (0, 0)),
        ],
        out_specs=pl.BlockSpec((N_HEADS * HEAD_V, t), lambda b, i: (0, b * nq + i)),
        out_shape=jax.ShapeDtypeStruct((N_HEADS * HEAD_V, n), BF16),
        scratch_shapes=[
            pltpu.VMEM((n_stack // DIFF_GROUP, QK_WIDTH, DIFF_GROUP * t), BF16),
            pltpu.VMEM((n_stack // DIFF_GROUP, t, DIFF_GROUP * t), F32),
            pltpu.VMEM((n_stack // DIFF_GROUP, 1, DIFF_GROUP * t), F32),
            pltpu.VMEM((n_stack, 1, t), F32),
            pltpu.VMEM((n_stack, V_SLAB, t), F32),
        ],
        compiler_params=_cparams("parallel", "arbitrary"),
        name="diff_attn",
    )(dqt, dk, dvt, bias, lp, og)


def _fox_kernel(qt_ref, k_ref, vt_ref, lf_ref, tril_ref, mask_ref, o_ref,
                qst_ref, u_ref, umax_ref, m_ref, acc_ref, fneg_ref):
    t = qt_ref.shape[1]
    seq = k_ref.shape[0]
    i = pl.program_id(1)

    @pl.when(i == 0)
    def _():
        tril = tril_ref[...]
        for blk in range(seq // LANES):
            rows = slice(blk * LANES, (blk + 1) * LANES)
            x = jnp.concatenate(
                [jnp.broadcast_to(lf_ref[rows, FOX_GATE_COL + h:FOX_GATE_COL + h + 1], (LANES, LANES))
                 for h in range(N_HEADS)], axis=1)
            hi, mid, lo = _split3(x)
            f = _dot(tril, hi) + _dot(tril, mid) + _dot(tril, lo)
            for h in range(N_HEADS):
                fneg_ref[h, rows, :] = f[:, h * LANES:(h + 1) * LANES]
        for h in range(N_HEADS):
            carry = jnp.zeros((1, LANES), F32)
            for blk in range(seq // LANES):
                rows = slice(blk * LANES, (blk + 1) * LANES)
                f = fneg_ref[h, rows, :] + carry
                fneg_ref[h, rows, :] = f * (-LOG2E)
                carry = f[LANES - 1:LANES, :]

    _stack_queries(qt_ref[...], qst_ref, N_HEADS, FOX_GROUP)

    def keys(j):
        return pl.ds(pl.multiple_of(j * t, t), t)

    def scores(j, p):
        mask = mask_ref[jnp.where(j == i, 1, 0)]
        bias = [jnp.concatenate([fneg_ref[e, keys(j), :]] * (t // LANES), axis=1) + mask
                for e in range(FOX_GROUP * p, FOX_GROUP * (p + 1))]
        return _dot(k_ref[keys(j), :], qst_ref[p]) + jnp.concatenate(bias, axis=1)

    def vt(j, e):
        return vt_ref[e * V_SLAB:(e + 1) * V_SLAB, keys(j)]

    _attn_pipeline(i + 1, N_HEADS, FOX_GROUP, scores, lambda j, e: 0.0, vt, u_ref, umax_ref, m_ref, acc_ref)

    for h in range(N_HEADS):
        o_ref[h * HEAD_V:(h + 1) * HEAD_V, :] = _softmax_out(acc_ref[h]).astype(BF16)


def _fox_attn(fqt, fk, fvt, lf, batch, seq):
    t = min(ATTN_TILE, seq)
    nq = seq // t
    n = batch * seq
    tril = jnp.asarray(np.tril(np.ones((LANES, LANES), np.float32)), BF16)
    pos = np.arange(t)
    causal = np.where(pos[:, None] <= pos[None, :], 0.0, NEG)
    mask = jnp.asarray(np.stack([np.zeros((t, t)), causal]).astype(np.float32))
    return pl.pallas_call(
        _fox_kernel,
        grid=(batch, nq),
        in_specs=[
            pl.BlockSpec((QK_WIDTH, t), lambda b, i: (0, b * nq + i)),
            pl.BlockSpec((seq, QK_WIDTH), lambda b, i: (b, 0)),
            pl.BlockSpec((VT_ROWS, seq), lambda b, i: (0, b)),
            pl.BlockSpec((seq, LANES), lambda b, i: (b, 0)),
            pl.BlockSpec(tril.shape, lambda b, i: (0, 0)),
            pl.BlockSpec(mask.shape, lambda b, i: (0, 0, 0)),
        ],
        out_specs=pl.BlockSpec((N_HEADS * HEAD_V, t), lambda b, i: (0, b * nq + i)),
        out_shape=jax.ShapeDtypeStruct((N_HEADS * HEAD_V, n), BF16),
        scratch_shapes=[
            pltpu.VMEM((N_HEADS // FOX_GROUP, QK_WIDTH, FOX_GROUP * t), BF16),
            pltpu.VMEM((N_HEADS // FOX_GROUP, t, FOX_GROUP * t), F32),
            pltpu.VMEM((N_HEADS // FOX_GROUP, 1, FOX_GROUP * t), F32),
            pltpu.VMEM((N_HEADS, 1, t), F32),
            pltpu.VMEM((N_HEADS, V_SLAB, t), F32),
            pltpu.VMEM((N_HEADS, seq, LANES), F32),
        ],
        compiler_params=_cparams("parallel", "arbitrary"),
        name="fox_attn",
    )(fqt, fk, fvt, lf, tril, mask)


GLA_LEVELS = (1, 2, 4, 8, 16, 32)
N_LEVELS = len(GLA_LEVELS)


def _gla_tables():
    c = CHUNK
    idx = np.arange(c)
    pair = np.zeros((N_LEVELS + 1, c, c), np.float32)
    for l, m in enumerate(GLA_LEVELS):
        upper = idx % (2 * m) >= m
        same = (idx[:, None] // (2 * m)) == (idx[None, :] // (2 * m))
        pair[l] = same & upper[:, None] & (~upper)[None, :]
    pair[N_LEVELS] = np.eye(c)
    pair = np.tile(pair, (1, N_HEADS, 1))
    tril = np.tril(np.ones((c, c), np.float32))
    return tril, pair


def _block_ref(b, m):
    c, w = b.shape
    if 2 * m >= 8:
        b3 = b.reshape(c // (2 * m), 2 * m, w)
        return jnp.broadcast_to(b3[:, m - 1:m, :], b3.shape).reshape(c, w)
    b3 = b.reshape(c // 8, 8, w)
    sub = lax.broadcasted_iota(jnp.int32, (1, 8, 1), 1)
    ref = jnp.broadcast_to(b3[:, m - 1:m, :], b3.shape)
    for start in range(2 * m, 8, 2 * m):
        pick = jnp.broadcast_to(b3[:, start + m - 1:start + m, :], b3.shape)
        ref = jnp.where(sub >= start, pick, ref)
    return ref.reshape(c, w)


def _gla_kernel(q_ref, k_ref, v_ref, r_ref, a_ref, tril_ref, pair_ref, og_ref, o_ref, state_ref):
    c = CHUNK
    hk = N_HEADS * GLA_K_DIM
    seqs = range(q_ref.shape[0])

    @pl.when(pl.program_id(1) == 0)
    def _():
        state_ref[...] = jnp.zeros_like(state_ref)

    col = lax.broadcasted_iota(jnp.int32, (1, hk), 1)
    head_cols = [col // GLA_K_DIM == h for h in range(N_HEADS)]

    def head_only(x, h):
        return jnp.where(head_cols[h], x, jnp.zeros_like(x))

    def stack_heads(x):
        x = x.astype(BF16)
        return jnp.concatenate([head_only(x, h) for h in range(N_HEADS)], axis=0)

    def chunk(ci, carry):
        r0 = pl.multiple_of(ci * c, c)
        rows = pl.ds(r0, c)
        tril = tril_ref[...]
        q = [q_ref[s, rows, :] for s in seqs]
        k = [k_ref[s, rows, :] for s in seqs]
        v = [v_ref[s, rows, :] for s in seqs]
        b = []
        for s in seqs:
            a_hi, a_mid, a_lo = _split3(a_ref[s, rows, :])
            b.append(_dot(tril, a_hi) + _dot(tril, a_mid) + _dot(tril, a_lo))
        b_last = [b[s][c - 1:c, :] for s in seqs]

        attn = [pair_ref[N_LEVELS] * _dot_nt(stack_heads(q[s]), k[s].astype(BF16)) for s in seqs]
        for l, m in enumerate(GLA_LEVELS):
            for s in seqs:
                ref = _block_ref(b[s], m)
                ql = q[s] * jnp.exp(jnp.minimum(b[s] - ref, 0.0))
                kl = k[s] * jnp.exp(jnp.minimum(ref - b[s], 0.0))
                attn[s] = attn[s] + pair_ref[l] * _dot_nt(stack_heads(ql), kl.astype(BF16))

        for s in seqs:
            attn_s = attn[s].astype(BF16)
            state_t = state_ref[s]
            o_inter = _dot_nt(stack_heads(q[s] * jnp.exp(b[s])), state_t.astype(BF16))
            k_dec = (k[s] * jnp.exp(b_last[s] - b[s])).astype(BF16)
            upd = None
            for h in range(N_HEADS):
                hr = slice(h * c, (h + 1) * c)
                hv = slice(h * GLA_V_DIM, (h + 1) * GLA_V_DIM)
                o = o_inter[hr] + _dot(attn_s[hr], v[s][:, hv])
                y = _rms(o, og_ref[...])
                gate = r_ref[s, rows, hv]
                o_ref[s, rows, hv] = (y * (gate * jax.nn.sigmoid(gate))).astype(BF16)
                term = _dot_tn(v[s][:, hv], head_only(k_dec, h))
                upd = term if upd is None else upd + term
            state_ref[s] = state_t * jnp.exp(b_last[s]) + upd
        return carry

    lax.fori_loop(0, q_ref.shape[1] // c, chunk, 0)


def _gla(gq, gk, gv, gr, la, og, batch, seq):
    tg = min(GLA_TILE, seq)
    ng = seq // tg
    together = GLA_SEQS if batch % GLA_SEQS == 0 else 1
    tril_np, pair_np = _gla_tables()
    tril = jnp.asarray(tril_np, BF16)
    pair = jnp.asarray(pair_np)
    hk = N_HEADS * GLA_K_DIM
    hv = N_HEADS * GLA_V_DIM

    def seq_major(x):
        return x.reshape(batch // together, together, seq, x.shape[-1])

    def blk(width):
        return pl.BlockSpec((None, together, tg, width), lambda b, i: (b, 0, i, 0))

    out = pl.pallas_call(
        _gla_kernel,
        grid=(batch // together, ng),
        in_specs=[blk(hk), blk(hk), blk(hv), blk(hv), blk(hk),
                  pl.BlockSpec(tril.shape, lambda b, i: (0, 0)),
                  pl.BlockSpec(pair.shape, lambda b, i: (0, 0, 0)),
                  pl.BlockSpec(og.shape, lambda b, i: (0, 0))],
        out_specs=blk(hv),
        out_shape=jax.ShapeDtypeStruct((batch // together, together, seq, hv), BF16),
        scratch_shapes=[pltpu.VMEM((together, GLA_V_DIM, hk), F32)],
        compiler_params=_cparams("parallel", "arbitrary"),
        name="gla",
    )(seq_major(gq), seq_major(gk), seq_major(gv), seq_major(gr), seq_major(la), tril, pair, og)
    return out.reshape(batch * seq, hv)


DIFF_WIDTH = N_HEADS * HEAD_V
GLA_WIDTH = N_HEADS * GLA_V_DIM


def _out_kernel(x_ref, at_ref, g_ref, ct_ref, w_ref, o_ref):
    acc = _dot_tn(at_ref[...], w_ref[0:DIFF_WIDTH, :])
    acc = acc + _dot(g_ref[...], w_ref[DIFF_WIDTH:DIFF_WIDTH + GLA_WIDTH, :])
    acc = acc + _dot_tn(ct_ref[...], w_ref[DIFF_WIDTH + GLA_WIDTH:, :])
    o_ref[...] = x_ref[...] + acc


def _out_proj(x2d, at, g, ct, w):
    n = x2d.shape[0]
    tm = min(TOKEN_TILE, n)
    row = lambda i: (i, 0)
    colblk = lambda i: (0, i)
    return pl.pallas_call(
        _out_kernel,
        grid=(n // tm,),
        in_specs=[pl.BlockSpec((tm, D_MODEL), row), pl.BlockSpec((DIFF_WIDTH, tm), colblk),
                  pl.BlockSpec((tm, GLA_WIDTH), row), pl.BlockSpec((DIFF_WIDTH, tm), colblk),
                  pl.BlockSpec(w.shape, lambda i: (0, 0))],
        out_specs=pl.BlockSpec((tm, D_MODEL), row),
        out_shape=jax.ShapeDtypeStruct((n, D_MODEL), F32),
        compiler_params=_cparams("parallel"),
        name="proj_out",
    )(x2d, at, g, ct, w)


def _in_proj_weights(w_in):
    def zeros(width):
        return jnp.zeros((D_MODEL, width), w_in.dtype)

    w = jnp.concatenate([w_in[:, 256:512], w_in[:, 768:2304], w_in[:, 2576:2832], w_in[:, 2304:2320],
                         w_in[:, IN_WIDTH - N_HEADS:], zeros(LANES - GLA_GATE_RANK - N_HEADS)], axis=1)

    def value_slabs(lo):
        parts = []
        for h in range(N_HEADS):
            parts += [w_in[:, lo + h * HEAD_V:lo + (h + 1) * HEAD_V], zeros(V_SLAB - HEAD_V)]
        return parts

    wt = jnp.concatenate([w_in[:, 0:256]] + value_slabs(512) + [w_in[:, 2320:2576]] + value_slabs(2832), axis=1)
    assert w.shape[1] == PROJ_WIDTH and wt.shape[1] == PROJ_T_ROWS
    return w.astype(BF16), wt.T.astype(BF16)


def _ffn_weights(w13, w2):
    pad = D_FF_PAD - D_FF
    wg = jnp.pad(w13[:, :D_FF], ((0, 0), (0, pad))).astype(BF16)
    wu = jnp.pad(w13[:, D_FF:], ((0, 0), (0, pad))).astype(BF16)
    return wg, wu, jnp.pad(w2, ((0, pad), (0, 0))).astype(BF16)


def _same_group(width):
    g = np.arange(QK_WIDTH) // width
    return jnp.asarray((g[:, None] == g[None, :]).astype(np.float32), BF16)


def kernel(x, ffn1_norm, ffn1_w13, ffn1_w2, mix_norm, w_in, w_out, diff_q_norm, diff_k_norm,
           diff_lambda, diff_out_norm, gla_alpha_w2, gla_alpha_b, gla_out_norm, fox_q_norm,
           fox_k_norm, fox_f_bias, ffn2_norm, ffn2_w13, ffn2_w2):
    batch, seq, _ = x.shape
    x2d = x.reshape(batch * seq, D_MODEL)
    grp32 = _same_group(DIFF_QK_DIM)
    grp64 = _same_group(HEAD_V)
    vone_np = np.zeros((VT_ROWS, 1), np.float32)
    vone_np[HEAD_V::V_SLAB, 0] = 1.0
    vone = jnp.asarray(vone_np)

    for i in range(DEPTH):
        wg, wu, w2p = _ffn_weights(ffn1_w13[i], ffn1_w2[i])
        x2d = _ffn(x2d, ffn1_norm[i].reshape(1, D_MODEL), wg, wu, w2p)

        w, wt = _in_proj_weights(w_in[i])
        kg = jnp.stack([jnp.tile(diff_k_norm[i], QK_WIDTH // DIFF_QK_DIM), jnp.tile(fox_k_norm[i], N_HEADS)])
        qgt = jnp.stack([jnp.tile(diff_q_norm[i], QK_WIDTH // DIFF_QK_DIM) * (LOG2E * DIFF_QK_DIM ** -0.5),
                         jnp.tile(fox_q_norm[i], N_HEADS) * (LOG2E * HEAD_V ** -0.5)], axis=1)
        aw2 = jnp.pad(gla_alpha_w2[i], ((0, LANES - GLA_GATE_RANK), (0, 0)))
        ab = gla_alpha_b[i].reshape(1, -1)
        fb = jnp.pad(fox_f_bias[i], (FOX_GATE_COL, LANES - FOX_GATE_COL - N_HEADS)).reshape(1, LANES)
        (dqt, dk, dvt, gq, gk, gv, gr, la, fqt, fk, fvt, lf) = _proj(
            x2d, mix_norm[i].reshape(1, D_MODEL), w, wt, kg, qgt, grp32, grp64, aw2, ab, fb, vone)

        lam_init = 0.8 - 0.6 * math.exp(-0.3 * i)
        at = _diff_attn(dqt, dk, dvt, diff_lambda[i], diff_out_norm[i].reshape(HEAD_V, 1), lam_init,
                        batch, seq)
        g = _gla(gq, gk, gv, gr, la, gla_out_norm[i].reshape(1, GLA_V_DIM), batch, seq)
        ct = _fox_attn(fqt, fk, fvt, lf, batch, seq)
        x2d = _out_proj(x2d, at, g, ct, w_out[i].astype(BF16))

        wg, wu, w2p = _ffn_weights(ffn2_w13[i], ffn2_w2[i])
        x2d = _ffn(x2d, ffn2_norm[i].reshape(1, D_MODEL), wg, wu, w2p)
    return x2d.reshape(batch, seq, D_MODEL)
```

```python
import functools
import math

import numpy as np
import jax
import jax.numpy as jnp
from jax import lax
from jax.experimental import pallas as pl
from jax.experimental.pallas import tpu as pltpu

F32 = jnp.float32
BF16 = jnp.bfloat16

D_MODEL = 1024
D_FF = 2752
DEPTH = 4
CHUNK = 64
N_HEADS = 4
DIFF_QK_DIM = 32
HEAD_V = 64
GLA_K_DIM = 64
GLA_V_DIM = 128
GLA_GATE_RANK = 16
GLA_TAU = 16.0
NORM_EPS = 1e-6
IN_WIDTH = 3092
DIFF_WIDTH = N_HEADS * HEAD_V
GLA_WIDTH = N_HEADS * GLA_V_DIM

LANES = 128
FF_CHUNK = 256
D_FF_PAD = 2816
N_FF_CHUNKS = D_FF_PAD // FF_CHUNK
TOKEN_TILE = 512
ATTN_TILE = 256
GLA_TILE = 512
GLA_SEQS = 4
NEG = -1e30
LOG2E = math.log2(math.e)
VMEM_LIMIT = 56 * 1024 * 1024

QK_WIDTH = 256
DIFF_GROUP = 4
FOX_GROUP = 2
V_SLAB = 80
VT_ROWS = N_HEADS * V_SLAB

C_DK, C_GQ, C_GK, C_GV, C_GR, C_FK, C_GATES = 0, 256, 512, 768, 1280, 1792, 2048
PROJ_WIDTH = 2176
FOX_GATE_COL = GLA_GATE_RANK
R_DQ, R_DV, R_FQ, R_FV = 0, 256, 576, 832
PROJ_T_ROWS = 1152

NT_DIMS = (((1,), (1,)), ((), ()))
TN_DIMS = (((0,), (0,)), ((), ()))


def _cparams(*sem):
    return pltpu.CompilerParams(dimension_semantics=sem, vmem_limit_bytes=VMEM_LIMIT)


def _dot(a, b):
    return jnp.dot(a, b, preferred_element_type=F32)


def _dot_nt(a, b):
    return lax.dot_general(a, b, NT_DIMS, preferred_element_type=F32)


def _dot_tn(a, b):
    return lax.dot_general(a, b, TN_DIMS, preferred_element_type=F32)


def _split2(x):
    hi = x.astype(BF16)
    lo = (x - hi.astype(F32)).astype(BF16)
    return hi, lo


def _split3(x):
    hi = x.astype(BF16)
    r = x - hi.astype(F32)
    mid = r.astype(BF16)
    lo = (r - mid.astype(F32)).astype(BF16)
    return hi, mid, lo


def _log_sigmoid(z):
    return jnp.minimum(z, 0.0) - jnp.log(1.0 + jnp.exp(-jnp.abs(z)))


def _rms(x, g):
    ms = jnp.mean(x * x, axis=-1, keepdims=True)
    return x * lax.rsqrt(ms + NORM_EPS) * g


def _swiglu_residual(x, g_ref, wg_ref, wu_ref, w2_ref):
    h = _rms(x, g_ref[...]).astype(BF16)
    acc = None
    for c in range(N_FF_CHUNKS):
        cols = slice(c * FF_CHUNK, (c + 1) * FF_CHUNK)
        gate = _dot(h, wg_ref[:, cols])
        up = _dot(h, wu_ref[:, cols])
        act = (gate * jax.nn.sigmoid(gate) * up).astype(BF16)
        term = _dot(act, w2_ref[cols, :])
        acc = term if acc is None else acc + term
    return x + 0.5 * acc


def _ffn_kernel(x_ref, g_ref, wg_ref, wu_ref, w2_ref, o_ref):
    o_ref[...] = _swiglu_residual(x_ref[...], g_ref, wg_ref, wu_ref, w2_ref)


def _mix_ffn_kernel(x_ref, at_ref, gl_ref, ct_ref, wo_ref, g_ref, wg_ref, wu_ref, w2_ref, o_ref):
    mix = _dot_tn(at_ref[...], wo_ref[0:DIFF_WIDTH, :])
    mix = mix + _dot(gl_ref[...], wo_ref[DIFF_WIDTH:DIFF_WIDTH + GLA_WIDTH, :])
    mix = mix + _dot_tn(ct_ref[...], wo_ref[DIFF_WIDTH + GLA_WIDTH:, :])
    o_ref[...] = _swiglu_residual(x_ref[...] + mix, g_ref, wg_ref, wu_ref, w2_ref)


def _ffn_specs(tm):
    const = lambda i: (0, 0)
    once = dict(pipeline_mode=pl.Buffered(1))
    return [pl.BlockSpec((1, D_MODEL), const),
            pl.BlockSpec((D_MODEL, D_FF_PAD), const, **once),
            pl.BlockSpec((D_MODEL, D_FF_PAD), const, **once),
            pl.BlockSpec((D_FF_PAD, D_MODEL), const, **once)]


def _ffn(x2d, gain, wg, wu, w2):
    n = x2d.shape[0]
    tm = min(TOKEN_TILE, n)
    row = lambda i: (i, 0)
    return pl.pallas_call(
        _ffn_kernel,
        grid=(n // tm,),
        in_specs=[pl.BlockSpec((tm, D_MODEL), row)] + _ffn_specs(tm),
        out_specs=pl.BlockSpec((tm, D_MODEL), row),
        out_shape=jax.ShapeDtypeStruct((n, D_MODEL), F32),
        compiler_params=_cparams("parallel"),
        name="ffn",
    )(x2d, gain, wg, wu, w2)


def _mix_ffn(x2d, at, gl, ct, wo, gain, wg, wu, w2):
    n = x2d.shape[0]
    tm = min(TOKEN_TILE, n)
    row = lambda i: (i, 0)
    colblk = lambda i: (0, i)
    return pl.pallas_call(
        _mix_ffn_kernel,
        grid=(n // tm,),
        in_specs=[pl.BlockSpec((tm, D_MODEL), row), pl.BlockSpec((DIFF_WIDTH, tm), colblk),
                  pl.BlockSpec((tm, GLA_WIDTH), row), pl.BlockSpec((DIFF_WIDTH, tm), colblk),
                  pl.BlockSpec(wo.shape, lambda i: (0, 0), pipeline_mode=pl.Buffered(1))] + _ffn_specs(tm),
        out_specs=pl.BlockSpec((tm, D_MODEL), row),
        out_shape=jax.ShapeDtypeStruct((n, D_MODEL), F32),
        compiler_params=_cparams("parallel"),
        name="mix_ffn",
    )(x2d, at, gl, ct, wo, gain, wg, wu, w2)


def _group_norm_lanes(t, grp, inv_d, gain):
    hi, lo = _split2(t * t)
    ms = (_dot(hi, grp) + _dot(lo, grp)) * inv_d
    return t * lax.rsqrt(ms + NORM_EPS) * gain


def _group_norm_rows(t, d, gain):
    rows, cols = t.shape
    t3 = t.reshape(rows // d, d, cols)
    ms = jnp.sum(t3 * t3, axis=1, keepdims=True) * (1.0 / d)
    return (t3 * lax.rsqrt(ms + NORM_EPS)).reshape(rows, cols) * gain


def _proj_kernel(x_ref, g_ref, w_ref, wt_ref, kg_ref, qgt_ref, grp32_ref, grp64_ref, aw2_ref,
                 ab_ref, fb_ref, vone_ref,
                 dqt_ref, dk_ref, dvt_ref, gq_ref, gk_ref, gv_ref, gr_ref, la_ref,
                 fqt_ref, fk_ref, fvt_ref, lf_ref):
    h = _rms(x_ref[...], g_ref[...]).astype(BF16)

    p_tok = _dot(h, w_ref[...])
    p_feat = _dot_nt(wt_ref[...], h)

    def proj(lo, width):
        return p_tok[:, lo:lo + width]

    def proj_t(lo, rows):
        return p_feat[lo:lo + rows, :]

    vone = vone_ref[...]
    dqt_ref[...] = _group_norm_rows(proj_t(R_DQ, QK_WIDTH), DIFF_QK_DIM, qgt_ref[:, 0:1]).astype(BF16)
    dk_ref[...] = _group_norm_lanes(proj(C_DK, QK_WIDTH), grp32_ref[...], 1.0 / DIFF_QK_DIM,
                                    kg_ref[0:1]).astype(BF16)
    dvt_ref[...] = (proj_t(R_DV, VT_ROWS) + vone).astype(BF16)
    gq_ref[...] = proj(C_GQ, 256) * (GLA_K_DIM ** -0.5)
    gk_ref[...] = proj(C_GK, 256)
    gv_ref[...] = proj(C_GV, 512).astype(BF16)
    gr_ref[...] = proj(C_GR, 512)
    fqt_ref[...] = _group_norm_rows(proj_t(R_FQ, QK_WIDTH), HEAD_V, qgt_ref[:, 1:2]).astype(BF16)
    fk_ref[...] = _group_norm_lanes(proj(C_FK, QK_WIDTH), grp64_ref[...], 1.0 / HEAD_V,
                                    kg_ref[1:2]).astype(BF16)
    fvt_ref[...] = (proj_t(R_FV, VT_ROWS) + vone).astype(BF16)

    gates = proj(C_GATES, LANES)
    ga_hi, ga_lo = _split2(gates)
    w_hi, w_lo = _split2(aw2_ref[...])
    z = _dot(ga_hi, w_hi) + _dot(ga_hi, w_lo) + _dot(ga_lo, w_hi) + ab_ref[...]
    la_ref[...] = _log_sigmoid(z) * (1.0 / GLA_TAU)
    lf_ref[...] = _log_sigmoid(gates + fb_ref[...])


def _proj(x2d, gain, w, wt, kg, qgt, grp32, grp64, aw2, ab, fb, vone):
    n = x2d.shape[0]
    tm = min(TOKEN_TILE, n)
    row = lambda i: (i, 0)
    colblk = lambda i: (0, i)
    const = lambda i: (0, 0)

    def full(a):
        return pl.BlockSpec(a.shape, const)

    def tok(width, dt):
        return jax.ShapeDtypeStruct((n, width), dt), pl.BlockSpec((tm, width), row)

    def feat(rows, dt):
        return jax.ShapeDtypeStruct((rows, n), dt), pl.BlockSpec((rows, tm), colblk)

    outs = [feat(QK_WIDTH, BF16), tok(QK_WIDTH, BF16), feat(VT_ROWS, BF16), tok(256, F32), tok(256, F32),
            tok(512, BF16), tok(512, F32), tok(256, F32), feat(QK_WIDTH, BF16), tok(QK_WIDTH, BF16),
            feat(VT_ROWS, BF16), tok(LANES, F32)]
    return pl.pallas_call(
        _proj_kernel,
        grid=(n // tm,),
        in_specs=[pl.BlockSpec((tm, D_MODEL), row), full(gain), full(w), full(wt), full(kg), full(qgt),
                  full(grp32), full(grp64), full(aw2), full(ab), full(fb), full(vone)],
        out_specs=[o[1] for o in outs],
        out_shape=[o[0] for o in outs],
        compiler_params=_cparams("parallel"),
        name="proj_in",
    )(x2d, gain, w, wt, kg, qgt, grp32, grp64, aw2, ab, fb, vone)


def _max_over_rows(u):
    rows = u.shape[0]
    while rows > 8:
        rows //= 2
        u = jnp.maximum(u[:rows], u[rows:])
    return jnp.max(u, axis=0, keepdims=True)


def _attn_pipeline(n_blocks, n_stack, group, scores_of, shift_of, vt_of, u_ref, umax_ref, m_ref, acc_ref):
    t = u_ref.shape[1]
    n_groups = n_stack // group
    n_slots = u_ref.shape[0]
    assert n_slots == max(n_groups, 2)
    m_ref[...] = jnp.full_like(m_ref, NEG)
    acc_ref[...] = jnp.zeros_like(acc_ref)

    def slot(j, p):
        return p if n_slots == n_groups else (j * n_groups + p) % n_slots

    def issue(j, p, dst):
        u = scores_of(j, p)
        u_ref[dst] = u
        umax_ref[dst] = _max_over_rows(u)

    issue(0, 0, slot(0, 0))

    def body(j, carry):
        j_next = jnp.minimum(j + 1, n_blocks - 1)
        for p in range(n_groups):
            if p + 1 < n_groups:
                issue(j, p + 1, slot(j, p + 1))
            else:
                issue(j_next, 0, slot(j + 1, 0))
            src = slot(j, p)
            for e in range(group * p, group * (p + 1)):
                lanes = slice((e % group) * t, (e % group + 1) * t)
                u = u_ref[src, :, lanes]
                c = shift_of(j, e)
                m_old = m_ref[e]
                m_new = jnp.maximum(m_old, umax_ref[src, :, lanes] + c)
                alpha = jnp.exp2(m_old - m_new)
                p_t = jnp.exp2(u - (m_new - c)).astype(BF16)
                acc_ref[e] = alpha * acc_ref[e] + _dot(vt_of(j, e), p_t)
                m_ref[e] = m_new
        return carry

    lax.fori_loop(0, n_blocks, body, 0)


def _stack_queries(qt, qst_ref, n_stack, group):
    depth = qt.shape[0] // n_stack
    row = lax.broadcasted_iota(jnp.int32, (qt.shape[0], 1), 0)
    zero = jnp.zeros_like(qt)
    for p in range(n_stack // group):
        qst_ref[p] = jnp.concatenate(
            [jnp.where(row // depth == e, qt, zero) for e in range(group * p, group * (p + 1))], axis=1)


def _softmax_out(acc):
    return acc[0:HEAD_V] / acc[HEAD_V:HEAD_V + 1]


DIFF_SLOPES = tuple(2.0 ** (-8.0 * (h + 1) / N_HEADS) for h in range(N_HEADS))


def _diff_tables(t):
    sl = np.arange(t)[:, None]
    tl = np.arange(t)[None, :]
    allowed = (sl // CHUNK) <= (tl // CHUNK)
    past = [LOG2E * slope * np.broadcast_to(sl, (t, t)) for slope in DIFF_SLOPES]
    diag = [np.where(allowed, LOG2E * slope * (tl - np.abs(tl - sl)), NEG) for slope in DIFF_SLOPES]
    return np.stack(past + diag).astype(np.float32)


def _diff_kernel(lam_init, qt_ref, k_ref, vt_ref, bias_ref, lp_ref, og_ref, o_ref,
                 qst_ref, u_ref, umax_ref, m_ref, acc_ref):
    t = qt_ref.shape[1]
    n_stack = 2 * N_HEADS
    i = pl.program_id(1)
    _stack_queries(qt_ref[...], qst_ref, n_stack, DIFF_GROUP)

    def keys(j):
        return pl.ds(pl.multiple_of(j * t, t), t)

    def scores(j, p):
        first = jnp.where(j == i, N_HEADS, 0) + p * (DIFF_GROUP // 2)
        bias = []
        for h in range(DIFF_GROUP // 2):
            bias += [bias_ref[first + h]] * 2
        return _dot(k_ref[keys(j), :], qst_ref[p]) + jnp.concatenate(bias, axis=1)

    def shift(j, e):
        return (LOG2E * DIFF_SLOPES[e // 2]) * ((j - i) * t).astype(F32)

    def vt(j, e):
        return vt_ref[(e // 2) * V_SLAB:(e // 2 + 1) * V_SLAB, keys(j)]

    _attn_pipeline(i + 1, n_stack, DIFF_GROUP, scores, shift, vt, u_ref, umax_ref, m_ref, acc_ref)

    lp = lp_ref[...]
    lam = (jnp.exp(jnp.sum(lp[0:1] * lp[1:2], keepdims=True))
           - jnp.exp(jnp.sum(lp[2:3] * lp[3:4], keepdims=True)) + lam_init)
    for h in range(N_HEADS):
        a = _softmax_out(acc_ref[2 * h]) - lam * _softmax_out(acc_ref[2 * h + 1])
        ms = jnp.sum(a * a, axis=0, keepdims=True) * (1.0 / HEAD_V)
        y = a * lax.rsqrt(ms + NORM_EPS) * og_ref[...] * (1.0 - lam_init)
        o_ref[h * HEAD_V:(h + 1) * HEAD_V, :] = y.astype(BF16)


def _diff_attn(dqt, dk, dvt, lp, og, lam_init, batch, seq):
    t = min(ATTN_TILE, seq)
    nq = seq // t
    n = batch * seq
    bias = jnp.asarray(_diff_tables(t))
    n_stack = 2 * N_HEADS
    return pl.pallas_call(
        functools.partial(_diff_kernel, lam_init),
        grid=(batch, nq),
        in_specs=[
            pl.BlockSpec((QK_WIDTH, t), lambda b, i: (0, b * nq + i)),
            pl.BlockSpec((seq, QK_WIDTH), lambda b, i: (b, 0)),
            pl.BlockSpec((VT_ROWS, seq), lambda b, i: (0, b)),
            pl.BlockSpec(bias.shape, lambda b, i: (0, 0, 0)),
            pl.BlockSpec(lp.shape, lambda b, i: (0, 0)),
            pl.BlockSpec(og.shape, lambda b, i: (0, 0)),
        ],
        out_specs=pl.BlockSpec((N_HEADS * HEAD_V, t), lambda b, i: (0, b * nq + i)),
        out_shape=jax.ShapeDtypeStruct((N_HEADS * HEAD_V, n), BF16),
        scratch_shapes=[
            pltpu.VMEM((n_stack // DIFF_GROUP, QK_WIDTH, DIFF_GROUP * t), BF16),
            pltpu.VMEM((max(n_stack // DIFF_GROUP, 2), t, DIFF_GROUP * t), F32),
            pltpu.VMEM((max(n_stack // DIFF_GROUP, 2), 1, DIFF_GROUP * t), F32),
            pltpu.VMEM((n_stack, 1, t), F32),
            pltpu.VMEM((n_stack, V_SLAB, t), F32),
        ],
        compiler_params=_cparams("parallel", "arbitrary"),
        name="diff_attn",
    )(dqt, dk, dvt, bias, lp, og)


def _fox_kernel(qt_ref, k_ref, vt_ref, lf_ref, tril_ref, mask_ref, o_ref,
                qst_ref, u_ref, umax_ref, m_ref, acc_ref, fneg_ref):
    t = qt_ref.shape[1]
    seq = k_ref.shape[0]
    i = pl.program_id(1)

    @pl.when(i == 0)
    def _():
        tril = tril_ref[...]
        for blk in range(seq // LANES):
            rows = slice(blk * LANES, (blk + 1) * LANES)
            x = jnp.concatenate(
                [jnp.broadcast_to(lf_ref[rows, FOX_GATE_COL + h:FOX_GATE_COL + h + 1], (LANES, LANES))
                 for h in range(N_HEADS)], axis=1)
            hi, mid, lo = _split3(x)
            f = _dot(tril, hi) + _dot(tril, mid) + _dot(tril, lo)
            for h in range(N_HEADS):
                fneg_ref[h, rows, :] = f[:, h * LANES:(h + 1) * LANES]
        for h in range(N_HEADS):
            carry = jnp.zeros((1, LANES), F32)
            for blk in range(seq // LANES):
                rows = slice(blk * LANES, (blk + 1) * LANES)
                f = fneg_ref[h, rows, :] + carry
                fneg_ref[h, rows, :] = f * (-LOG2E)
                carry = f[LANES - 1:LANES, :]

    _stack_queries(qt_ref[...], qst_ref, N_HEADS, FOX_GROUP)

    def keys(j):
        return pl.ds(pl.multiple_of(j * t, t), t)

    def scores(j, p):
        mask = mask_ref[jnp.where(j == i, 1, 0)]
        bias = [jnp.concatenate([fneg_ref[e, keys(j), :]] * (t // LANES), axis=1) + mask
                for e in range(FOX_GROUP * p, FOX_GROUP * (p + 1))]
        return _dot(k_ref[keys(j), :], qst_ref[p]) + jnp.concatenate(bias, axis=1)

    def vt(j, e):
        return vt_ref[e * V_SLAB:(e + 1) * V_SLAB, keys(j)]

    _attn_pipeline(i + 1, N_HEADS, FOX_GROUP, scores, lambda j, e: 0.0, vt, u_ref, umax_ref, m_ref, acc_ref)

    for h in range(N_HEADS):
        o_ref[h * HEAD_V:(h + 1) * HEAD_V, :] = _softmax_out(acc_ref[h]).astype(BF16)


def _fox_attn(fqt, fk, fvt, lf, batch, seq):
    t = min(ATTN_TILE, seq)
    nq = seq // t
    n = batch * seq
    tril = jnp.asarray(np.tril(np.ones((LANES, LANES), np.float32)), BF16)
    pos = np.arange(t)
    causal = np.where(pos[:, None] <= pos[None, :], 0.0, NEG)
    mask = jnp.asarray(np.stack([np.zeros((t, t)), causal]).astype(np.float32))
    return pl.pallas_call(
        _fox_kernel,
        grid=(batch, nq),
        in_specs=[
            pl.BlockSpec((QK_WIDTH, t), lambda b, i: (0, b * nq + i)),
            pl.BlockSpec((seq, QK_WIDTH), lambda b, i: (b, 0)),
            pl.BlockSpec((VT_ROWS, seq), lambda b, i: (0, b)),
            pl.BlockSpec((seq, LANES), lambda b, i: (b, 0)),
            pl.BlockSpec(tril.shape, lambda b, i: (0, 0)),
            pl.BlockSpec(mask.shape, lambda b, i: (0, 0, 0)),
        ],
        out_specs=pl.BlockSpec((N_HEADS * HEAD_V, t), lambda b, i: (0, b * nq + i)),
        out_shape=jax.ShapeDtypeStruct((N_HEADS * HEAD_V, n), BF16),
        scratch_shapes=[
            pltpu.VMEM((N_HEADS // FOX_GROUP, QK_WIDTH, FOX_GROUP * t), BF16),
            pltpu.VMEM((max(N_HEADS // FOX_GROUP, 2), t, FOX_GROUP * t), F32),
            pltpu.VMEM((max(N_HEADS // FOX_GROUP, 2), 1, FOX_GROUP * t), F32),
            pltpu.VMEM((N_HEADS, 1, t), F32),
            pltpu.VMEM((N_HEADS, V_SLAB, t), F32),
            pltpu.VMEM((N_HEADS, seq, LANES), F32),
        ],
        compiler_params=_cparams("parallel", "arbitrary"),
        name="fox_attn",
    )(fqt, fk, fvt, lf, tril, mask)


GLA_LEVELS = (1, 2, 4, 8, 16, 32)
N_LEVELS = len(GLA_LEVELS)


def _gla_tables():
    c = CHUNK
    idx = np.arange(c)
    pair = np.zeros((N_LEVELS + 1, c, c), np.float32)
    for l, m in enumerate(GLA_LEVELS):
        upper = idx % (2 * m) >= m
        same = (idx[:, None] // (2 * m)) == (idx[None, :] // (2 * m))
        pair[l] = same & upper[:, None] & (~upper)[None, :]
    pair[N_LEVELS] = np.eye(c)
    pair = np.tile(pair, (1, N_HEADS, 1))
    tril = np.tril(np.ones((c, c), np.float32))
    return tril, pair


def _block_ref(b, m):
    c, w = b.shape
    if 2 * m >= 8:
        b3 = b.reshape(c // (2 * m), 2 * m, w)
        return jnp.broadcast_to(b3[:, m - 1:m, :], b3.shape).reshape(c, w)
    b3 = b.reshape(c // 8, 8, w)
    sub = lax.broadcasted_iota(jnp.int32, (1, 8, 1), 1)
    ref = jnp.broadcast_to(b3[:, m - 1:m, :], b3.shape)
    for start in range(2 * m, 8, 2 * m):
        pick = jnp.broadcast_to(b3[:, start + m - 1:start + m, :], b3.shape)
        ref = jnp.where(sub >= start, pick, ref)
    return ref.reshape(c, w)


def _gla_kernel(q_ref, k_ref, v_ref, r_ref, a_ref, tril_ref, pair_ref, og_ref, o_ref, state_ref):
    c = CHUNK
    hk = N_HEADS * GLA_K_DIM
    seqs = range(q_ref.shape[0])

    @pl.when(pl.program_id(1) == 0)
    def _():
        state_ref[...] = jnp.zeros_like(state_ref)

    col = lax.broadcasted_iota(jnp.int32, (1, hk), 1)
    head_cols = [col // GLA_K_DIM == h for h in range(N_HEADS)]

    def head_only(x, h):
        return jnp.where(head_cols[h], x, jnp.zeros_like(x))

    def stack_heads(x):
        x = x.astype(BF16)
        return jnp.concatenate([head_only(x, h) for h in range(N_HEADS)], axis=0)

    def chunk(ci, carry):
        r0 = pl.multiple_of(ci * c, c)
        rows = pl.ds(r0, c)
        tril = tril_ref[...]
        q = [q_ref[s, rows, :] for s in seqs]
        k = [k_ref[s, rows, :] for s in seqs]
        v = [v_ref[s, rows, :] for s in seqs]
        b = []
        for s in seqs:
            a_hi, a_mid, a_lo = _split3(a_ref[s, rows, :])
            b.append(_dot(tril, a_hi) + _dot(tril, a_mid) + _dot(tril, a_lo))
        b_last = [b[s][c - 1:c, :] for s in seqs]

        attn = [pair_ref[N_LEVELS] * _dot_nt(stack_heads(q[s]), k[s].astype(BF16)) for s in seqs]
        for l, m in enumerate(GLA_LEVELS):
            for s in seqs:
                ref = _block_ref(b[s], m)
                ql = q[s] * jnp.exp(jnp.minimum(b[s] - ref, 0.0))
                kl = k[s] * jnp.exp(jnp.minimum(ref - b[s], 0.0))
                attn[s] = attn[s] + pair_ref[l] * _dot_nt(stack_heads(ql), kl.astype(BF16))

        for s in seqs:
            attn_s = attn[s].astype(BF16)
            state_t = state_ref[s]
            o_inter = _dot_nt(stack_heads(q[s] * jnp.exp(b[s])), state_t.astype(BF16))
            k_dec = (k[s] * jnp.exp(b_last[s] - b[s])).astype(BF16)
            upd = None
            for h in range(N_HEADS):
                hr = slice(h * c, (h + 1) * c)
                hv = slice(h * GLA_V_DIM, (h + 1) * GLA_V_DIM)
                o = o_inter[hr] + _dot(attn_s[hr], v[s][:, hv])
                y = _rms(o, og_ref[...])
                gate = r_ref[s, rows, hv]
                o_ref[s, rows, hv] = (y * (gate * jax.nn.sigmoid(gate))).astype(BF16)
                term = _dot_tn(v[s][:, hv], head_only(k_dec, h))
                upd = term if upd is None else upd + term
            state_ref[s] = state_t * jnp.exp(b_last[s]) + upd
        return carry

    lax.fori_loop(0, q_ref.shape[1] // c, chunk, 0)


def _gla(gq, gk, gv, gr, la, og, batch, seq):
    tg = min(GLA_TILE, seq)
    ng = seq // tg
    together = GLA_SEQS if batch % GLA_SEQS == 0 else 1
    tril_np, pair_np = _gla_tables()
    tril = jnp.asarray(tril_np, BF16)
    pair = jnp.asarray(pair_np)
    hk = N_HEADS * GLA_K_DIM
    hv = N_HEADS * GLA_V_DIM

    def seq_major(x):
        return x.reshape(batch // together, together, seq, x.shape[-1])

    def blk(width):
        return pl.BlockSpec((None, together, tg, width), lambda b, i: (b, 0, i, 0))

    out = pl.pallas_call(
        _gla_kernel,
        grid=(batch // together, ng),
        in_specs=[blk(hk), blk(hk), blk(hv), blk(hv), blk(hk),
                  pl.BlockSpec(tril.shape, lambda b, i: (0, 0)),
                  pl.BlockSpec(pair.shape, lambda b, i: (0, 0, 0)),
                  pl.BlockSpec(og.shape, lambda b, i: (0, 0))],
        out_specs=blk(hv),
        out_shape=jax.ShapeDtypeStruct((batch // together, together, seq, hv), BF16),
        scratch_shapes=[pltpu.VMEM((together, GLA_V_DIM, hk), F32)],
        compiler_params=_cparams("parallel", "arbitrary"),
        name="gla",
    )(seq_major(gq), seq_major(gk), seq_major(gv), seq_major(gr), seq_major(la), tril, pair, og)
    return out.reshape(batch * seq, hv)


def _in_proj_weights(w_in):
    def zeros(width):
        return jnp.zeros((D_MODEL, width), w_in.dtype)

    w = jnp.concatenate([w_in[:, 256:512], w_in[:, 768:2304], w_in[:, 2576:2832], w_in[:, 2304:2320],
                         w_in[:, IN_WIDTH - N_HEADS:], zeros(LANES - GLA_GATE_RANK - N_HEADS)], axis=1)

    def value_slabs(lo):
        parts = []
        for h in range(N_HEADS):
            parts += [w_in[:, lo + h * HEAD_V:lo + (h + 1) * HEAD_V], zeros(V_SLAB - HEAD_V)]
        return parts

    wt = jnp.concatenate([w_in[:, 0:256]] + value_slabs(512) + [w_in[:, 2320:2576]] + value_slabs(2832), axis=1)
    assert w.shape[1] == PROJ_WIDTH and wt.shape[1] == PROJ_T_ROWS
    return w.astype(BF16), wt.T.astype(BF16)


def _ffn_weights(w13, w2):
    pad = D_FF_PAD - D_FF
    wg = jnp.pad(w13[:, :D_FF], ((0, 0), (0, pad))).astype(BF16)
    wu = jnp.pad(w13[:, D_FF:], ((0, 0), (0, pad))).astype(BF16)
    return wg, wu, jnp.pad(w2, ((0, pad), (0, 0))).astype(BF16)


def _same_group(width):
    g = np.arange(QK_WIDTH) // width
    return jnp.asarray((g[:, None] == g[None, :]).astype(np.float32), BF16)


def kernel(x, ffn1_norm, ffn1_w13, ffn1_w2, mix_norm, w_in, w_out, diff_q_norm, diff_k_norm,
           diff_lambda, diff_out_norm, gla_alpha_w2, gla_alpha_b, gla_out_norm, fox_q_norm,
           fox_k_norm, fox_f_bias, ffn2_norm, ffn2_w13, ffn2_w2):
    batch, seq, _ = x.shape
    x2d = x.reshape(batch * seq, D_MODEL)
    grp32 = _same_group(DIFF_QK_DIM)
    grp64 = _same_group(HEAD_V)
    vone_np = np.zeros((VT_ROWS, 1), np.float32)
    vone_np[HEAD_V::V_SLAB, 0] = 1.0
    vone = jnp.asarray(vone_np)

    for i in range(DEPTH):
        wg, wu, w2p = _ffn_weights(ffn1_w13[i], ffn1_w2[i])
        x2d = _ffn(x2d, ffn1_norm[i].reshape(1, D_MODEL), wg, wu, w2p)

        w, wt = _in_proj_weights(w_in[i])
        kg = jnp.stack([jnp.tile(diff_k_norm[i], QK_WIDTH // DIFF_QK_DIM), jnp.tile(fox_k_norm[i], N_HEADS)])
        qgt = jnp.stack([jnp.tile(diff_q_norm[i], QK_WIDTH // DIFF_QK_DIM) * (LOG2E * DIFF_QK_DIM ** -0.5),
                         jnp.tile(fox_q_norm[i], N_HEADS) * (LOG2E * HEAD_V ** -0.5)], axis=1)
        aw2 = jnp.pad(gla_alpha_w2[i], ((0, LANES - GLA_GATE_RANK), (0, 0)))
        ab = gla_alpha_b[i].reshape(1, -1)
        fb = jnp.pad(fox_f_bias[i], (FOX_GATE_COL, LANES - FOX_GATE_COL - N_HEADS)).reshape(1, LANES)
        (dqt, dk, dvt, gq, gk, gv, gr, la, fqt, fk, fvt, lf) = _proj(
            x2d, mix_norm[i].reshape(1, D_MODEL), w, wt, kg, qgt, grp32, grp64, aw2, ab, fb, vone)

        lam_init = 0.8 - 0.6 * math.exp(-0.3 * i)
        at = _diff_attn(dqt, dk, dvt, diff_lambda[i], diff_out_norm[i].reshape(HEAD_V, 1), lam_init,
                        batch, seq)
        g = _gla(gq, gk, gv, gr, la, gla_out_norm[i].reshape(1, GLA_V_DIM), batch, seq)
        ct = _fox_attn(fqt, fk, fvt, lf, batch, seq)
        wg, wu, w2p = _ffn_weights(ffn2_w13[i], ffn2_w2[i])
        x2d = _mix_ffn(x2d, at, g, ct, w_out[i].astype(BF16), ffn2_norm[i].reshape(1, D_MODEL), wg, wu, w2p)
    return x2d.reshape(batch, seq, D_MODEL)
```

```python
import functools
import math

import numpy as np
import jax
import jax.numpy as jnp
from jax import lax
from jax.experimental import pallas as pl
from jax.experimental.pallas import tpu as pltpu

F32 = jnp.float32
BF16 = jnp.bfloat16

D_MODEL = 1024
D_FF = 2752
DEPTH = 4
CHUNK = 64
N_HEADS = 4
DIFF_QK_DIM = 32
HEAD_V = 64
GLA_K_DIM = 64
GLA_V_DIM = 128
GLA_GATE_RANK = 16
GLA_TAU = 16.0
NORM_EPS = 1e-6
IN_WIDTH = 3092
DIFF_WIDTH = N_HEADS * HEAD_V
GLA_WIDTH = N_HEADS * GLA_V_DIM

LANES = 128
FF_CHUNK = 256
D_FF_PAD = 2816
N_FF_CHUNKS = D_FF_PAD // FF_CHUNK
TOKEN_TILE = 512
ATTN_TILE = 256
GLA_TILE = 512
GLA_SEQS = 4
NEG = -1e30
LOG2E = math.log2(math.e)
VMEM_LIMIT = 56 * 1024 * 1024

QK_WIDTH = 256
V_SLAB = 80
VT_ROWS = N_HEADS * V_SLAB

C_DK, C_GQ, C_GK, C_GV, C_GR, C_FK, C_GATES = 0, 256, 512, 768, 1280, 1792, 2048
PROJ_WIDTH = 2176
FOX_GATE_COL = GLA_GATE_RANK
R_DQ, R_DV, R_FQ, R_FV = 0, 256, 576, 832
PROJ_T_ROWS = 1152

NT_DIMS = (((1,), (1,)), ((), ()))
TN_DIMS = (((0,), (0,)), ((), ()))


def _cparams(*sem):
    return pltpu.CompilerParams(dimension_semantics=sem, vmem_limit_bytes=VMEM_LIMIT)


def _dot(a, b):
    return jnp.dot(a, b, preferred_element_type=F32)


def _dot_nt(a, b):
    return lax.dot_general(a, b, NT_DIMS, preferred_element_type=F32)


def _dot_tn(a, b):
    return lax.dot_general(a, b, TN_DIMS, preferred_element_type=F32)


def _split2(x):
    hi = x.astype(BF16)
    lo = (x - hi.astype(F32)).astype(BF16)
    return hi, lo


def _split3(x):
    hi = x.astype(BF16)
    r = x - hi.astype(F32)
    mid = r.astype(BF16)
    lo = (r - mid.astype(F32)).astype(BF16)
    return hi, mid, lo


def _log_sigmoid(z):
    return jnp.minimum(z, 0.0) - jnp.log(1.0 + jnp.exp(-jnp.abs(z)))


def _rms(x, g):
    ms = jnp.mean(x * x, axis=-1, keepdims=True)
    return x * lax.rsqrt(ms + NORM_EPS) * g


def _swiglu_residual(x, g_ref, wg_ref, wu_ref, w2_ref):
    h = _rms(x, g_ref[...]).astype(BF16)
    acc = None
    for c in range(N_FF_CHUNKS):
        cols = slice(c * FF_CHUNK, (c + 1) * FF_CHUNK)
        gate = _dot(h, wg_ref[:, cols])
        up = _dot(h, wu_ref[:, cols])
        act = (gate * jax.nn.sigmoid(gate) * up).astype(BF16)
        term = _dot(act, w2_ref[cols, :])
        acc = term if acc is None else acc + term
    return x + 0.5 * acc


def _ffn_kernel(x_ref, g_ref, wg_ref, wu_ref, w2_ref, o_ref):
    o_ref[...] = _swiglu_residual(x_ref[...], g_ref, wg_ref, wu_ref, w2_ref)


def _mix_ffn_kernel(x_ref, at_ref, gl_ref, ct_ref, wo_ref, g_ref, wg_ref, wu_ref, w2_ref, o_ref):
    mix = _dot_tn(at_ref[...], wo_ref[0:DIFF_WIDTH, :])
    mix = mix + _dot(gl_ref[...], wo_ref[DIFF_WIDTH:DIFF_WIDTH + GLA_WIDTH, :])
    mix = mix + _dot_tn(ct_ref[...], wo_ref[DIFF_WIDTH + GLA_WIDTH:, :])
    o_ref[...] = _swiglu_residual(x_ref[...] + mix, g_ref, wg_ref, wu_ref, w2_ref)


def _ffn_specs(tm):
    const = lambda i: (0, 0)
    once = dict(pipeline_mode=pl.Buffered(1))
    return [pl.BlockSpec((1, D_MODEL), const),
            pl.BlockSpec((D_MODEL, D_FF_PAD), const, **once),
            pl.BlockSpec((D_MODEL, D_FF_PAD), const, **once),
            pl.BlockSpec((D_FF_PAD, D_MODEL), const, **once)]


def _ffn(x2d, gain, wg, wu, w2):
    n = x2d.shape[0]
    tm = min(TOKEN_TILE, n)
    row = lambda i: (i, 0)
    return pl.pallas_call(
        _ffn_kernel,
        grid=(n // tm,),
        in_specs=[pl.BlockSpec((tm, D_MODEL), row)] + _ffn_specs(tm),
        out_specs=pl.BlockSpec((tm, D_MODEL), row),
        out_shape=jax.ShapeDtypeStruct((n, D_MODEL), F32),
        compiler_params=_cparams("parallel"),
        name="ffn",
    )(x2d, gain, wg, wu, w2)


def _mix_ffn(x2d, at, gl, ct, wo, gain, wg, wu, w2):
    n = x2d.shape[0]
    tm = min(TOKEN_TILE, n)
    row = lambda i: (i, 0)
    colblk = lambda i: (0, i)
    return pl.pallas_call(
        _mix_ffn_kernel,
        grid=(n // tm,),
        in_specs=[pl.BlockSpec((tm, D_MODEL), row), pl.BlockSpec((DIFF_WIDTH, tm), colblk),
                  pl.BlockSpec((tm, GLA_WIDTH), row), pl.BlockSpec((DIFF_WIDTH, tm), colblk),
                  pl.BlockSpec(wo.shape, lambda i: (0, 0), pipeline_mode=pl.Buffered(1))] + _ffn_specs(tm),
        out_specs=pl.BlockSpec((tm, D_MODEL), row),
        out_shape=jax.ShapeDtypeStruct((n, D_MODEL), F32),
        compiler_params=_cparams("parallel"),
        name="mix_ffn",
    )(x2d, at, gl, ct, wo, gain, wg, wu, w2)


def _group_norm_lanes(t, grp, inv_d, gain):
    hi, lo = _split2(t * t)
    ms = (_dot(hi, grp) + _dot(lo, grp)) * inv_d
    return t * lax.rsqrt(ms + NORM_EPS) * gain


def _group_norm_rows(t, d, gain):
    rows, cols = t.shape
    t3 = t.reshape(rows // d, d, cols)
    ms = jnp.sum(t3 * t3, axis=1, keepdims=True) * (1.0 / d)
    return (t3 * lax.rsqrt(ms + NORM_EPS)).reshape(rows, cols) * gain


def _proj_kernel(x_ref, g_ref, w_ref, wt_ref, kg_ref, qgt_ref, grp32_ref, grp64_ref, aw2_ref,
                 ab_ref, fb_ref, vone_ref,
                 dqt_ref, dk_ref, dvt_ref, gq_ref, gk_ref, gv_ref, gr_ref, la_ref,
                 fqt_ref, fk_ref, fvt_ref, lf_ref):
    h = _rms(x_ref[...], g_ref[...]).astype(BF16)

    p_tok = _dot(h, w_ref[...])
    p_feat = _dot_nt(wt_ref[...], h)

    def proj(lo, width):
        return p_tok[:, lo:lo + width]

    def proj_t(lo, rows):
        return p_feat[lo:lo + rows, :]

    vone = vone_ref[...]
    dqt_ref[...] = _group_norm_rows(proj_t(R_DQ, QK_WIDTH), DIFF_QK_DIM, qgt_ref[:, 0:1]).astype(BF16)
    dk_ref[...] = _group_norm_lanes(proj(C_DK, QK_WIDTH), grp32_ref[...], 1.0 / DIFF_QK_DIM,
                                    kg_ref[0:1]).astype(BF16)
    dvt_ref[...] = (proj_t(R_DV, VT_ROWS) + vone).astype(BF16)
    gq_ref[...] = proj(C_GQ, 256) * (GLA_K_DIM ** -0.5)
    gk_ref[...] = proj(C_GK, 256)
    gv_ref[...] = proj(C_GV, 512).astype(BF16)
    gr_ref[...] = proj(C_GR, 512)
    fqt_ref[...] = _group_norm_rows(proj_t(R_FQ, QK_WIDTH), HEAD_V, qgt_ref[:, 1:2]).astype(BF16)
    fk_ref[...] = _group_norm_lanes(proj(C_FK, QK_WIDTH), grp64_ref[...], 1.0 / HEAD_V,
                                    kg_ref[1:2]).astype(BF16)
    fvt_ref[...] = (proj_t(R_FV, VT_ROWS) + vone).astype(BF16)

    gates = proj(C_GATES, LANES)
    ga_hi, ga_lo = _split2(gates)
    w_hi, w_lo = _split2(aw2_ref[...])
    z = _dot(ga_hi, w_hi) + _dot(ga_hi, w_lo) + _dot(ga_lo, w_hi) + ab_ref[...]
    la_ref[...] = _log_sigmoid(z) * (1.0 / GLA_TAU)
    lf_ref[...] = _log_sigmoid(gates + fb_ref[...])


def _proj(x2d, gain, w, wt, kg, qgt, grp32, grp64, aw2, ab, fb, vone):
    n = x2d.shape[0]
    tm = min(TOKEN_TILE, n)
    row = lambda i: (i, 0)
    colblk = lambda i: (0, i)
    const = lambda i: (0, 0)

    def full(a):
        return pl.BlockSpec(a.shape, const)

    def tok(width, dt):
        return jax.ShapeDtypeStruct((n, width), dt), pl.BlockSpec((tm, width), row)

    def feat(rows, dt):
        return jax.ShapeDtypeStruct((rows, n), dt), pl.BlockSpec((rows, tm), colblk)

    outs = [feat(QK_WIDTH, BF16), tok(QK_WIDTH, BF16), feat(VT_ROWS, BF16), tok(256, F32), tok(256, F32),
            tok(512, BF16), tok(512, F32), tok(256, F32), feat(QK_WIDTH, BF16), tok(QK_WIDTH, BF16),
            feat(VT_ROWS, BF16), tok(LANES, F32)]
    return pl.pallas_call(
        _proj_kernel,
        grid=(n // tm,),
        in_specs=[pl.BlockSpec((tm, D_MODEL), row), full(gain), full(w), full(wt), full(kg), full(qgt),
                  full(grp32), full(grp64), full(aw2), full(ab), full(fb), full(vone)],
        out_specs=[o[1] for o in outs],
        out_shape=[o[0] for o in outs],
        compiler_params=_cparams("parallel"),
        name="proj_in",
    )(x2d, gain, w, wt, kg, qgt, grp32, grp64, aw2, ab, fb, vone)


def _max_over_rows(u):
    rows = u.shape[0]
    while rows > 8:
        rows //= 2
        u = jnp.maximum(u[:rows], u[rows:])
    return jnp.max(u, axis=0, keepdims=True)


def _attn_pipeline(n_blocks, n_stack, group, scores_of, shift_of, vt_of, u_ref, umax_ref, m_ref, acc_ref):
    t = u_ref.shape[1]
    n_groups = n_stack // group
    n_slots = u_ref.shape[0]
    assert n_slots == max(n_groups, 2)
    m_ref[...] = jnp.full_like(m_ref, NEG)
    acc_ref[...] = jnp.zeros_like(acc_ref)

    def slot(j, p):
        return p if n_slots == n_groups else (j * n_groups + p) % n_slots

    def issue(j, p, dst):
        u = scores_of(j, p)
        u_ref[dst] = u
        umax_ref[dst] = _max_over_rows(u)

    issue(0, 0, slot(0, 0))

    def body(j, carry):
        j_next = jnp.minimum(j + 1, n_blocks - 1)
        for p in range(n_groups):
            if p + 1 < n_groups:
                issue(j, p + 1, slot(j, p + 1))
            else:
                issue(j_next, 0, slot(j + 1, 0))
            src = slot(j, p)
            for e in range(group * p, group * (p + 1)):
                lanes = slice((e % group) * t, (e % group + 1) * t)
                u = u_ref[src, :, lanes]
                c = shift_of(j, e)
                m_old = m_ref[e]
                m_new = jnp.maximum(m_old, umax_ref[src, :, lanes] + c)
                alpha = jnp.exp2(m_old - m_new)
                p_t = jnp.exp2(u - (m_new - c)).astype(BF16)
                acc_ref[e] = alpha * acc_ref[e] + _dot(vt_of(j, e), p_t)
                m_ref[e] = m_new
        return carry

    lax.fori_loop(0, n_blocks, body, 0)


def _stack_queries(qt, qst_ref, n_stack, group):
    depth = qt.shape[0] // n_stack
    row = lax.broadcasted_iota(jnp.int32, (qt.shape[0], 1), 0)
    zero = jnp.zeros_like(qt)
    for p in range(n_stack // group):
        qst_ref[p] = jnp.concatenate(
            [jnp.where(row // depth == e, qt, zero) for e in range(group * p, group * (p + 1))], axis=1)


def _softmax_out(acc):
    return acc[0:HEAD_V] / acc[HEAD_V:HEAD_V + 1]


DIFF_SLOPES = tuple(2.0 ** (-8.0 * (h + 1) / N_HEADS) for h in range(N_HEADS))


def _diff_tables(t):
    sl = np.arange(t)[:, None]
    tl = np.arange(t)[None, :]
    allowed = (sl // CHUNK) <= (tl // CHUNK)
    past = [LOG2E * slope * np.broadcast_to(sl, (t, t)) for slope in DIFF_SLOPES]
    diag = [np.where(allowed, LOG2E * slope * (tl - np.abs(tl - sl)), NEG) for slope in DIFF_SLOPES]
    return np.stack(past + diag).astype(np.float32)


SM_GROUP = 4
N_DIFF_COPIES = 2 * N_HEADS
N_SM_COPIES = N_DIFF_COPIES + N_HEADS


def _softmax_mixers_kernel(lam_init, dqt_ref, dk_ref, dvt_ref, fqt_ref, fk_ref, fvt_ref, lf_ref,
                           dbias_ref, fmask_ref, tril_ref, lp_ref, og_ref, a_ref, c_ref,
                           dqst_ref, fqst_ref, u_ref, umax_ref, m_ref, acc_ref, fneg_ref):
    t = dqt_ref.shape[1]
    seq = dk_ref.shape[0]
    i = pl.program_id(1)
    n_diff_groups = N_DIFF_COPIES // SM_GROUP

    @pl.when(i == 0)
    def _():
        tril = tril_ref[...]
        for blk in range(seq // LANES):
            rows = slice(blk * LANES, (blk + 1) * LANES)
            x = jnp.concatenate(
                [jnp.broadcast_to(lf_ref[rows, FOX_GATE_COL + h:FOX_GATE_COL + h + 1], (LANES, LANES))
                 for h in range(N_HEADS)], axis=1)
            hi, mid, lo = _split3(x)
            f = _dot(tril, hi) + _dot(tril, mid) + _dot(tril, lo)
            for h in range(N_HEADS):
                fneg_ref[h, rows, :] = f[:, h * LANES:(h + 1) * LANES]
        for h in range(N_HEADS):
            carry = jnp.zeros((1, LANES), F32)
            for blk in range(seq // LANES):
                rows = slice(blk * LANES, (blk + 1) * LANES)
                f = fneg_ref[h, rows, :] + carry
                fneg_ref[h, rows, :] = f * (-LOG2E)
                carry = f[LANES - 1:LANES, :]

    _stack_queries(dqt_ref[...], dqst_ref, N_DIFF_COPIES, SM_GROUP)
    _stack_queries(fqt_ref[...], fqst_ref, N_HEADS, SM_GROUP)

    def keys(j):
        return pl.ds(pl.multiple_of(j * t, t), t)

    def scores(j, p):
        if p < n_diff_groups:
            first = jnp.where(j == i, N_HEADS, 0) + p * (SM_GROUP // 2)
            bias = []
            for h in range(SM_GROUP // 2):
                bias += [dbias_ref[first + h]] * 2
            return _dot(dk_ref[keys(j), :], dqst_ref[p]) + jnp.concatenate(bias, axis=1)
        mask = fmask_ref[jnp.where(j == i, 1, 0)]
        bias = [jnp.concatenate([fneg_ref[h, keys(j), :]] * (t // LANES), axis=1) + mask
                for h in range(N_HEADS)]
        return _dot(fk_ref[keys(j), :], fqst_ref[0]) + jnp.concatenate(bias, axis=1)

    def shift(j, e):
        if e < N_DIFF_COPIES:
            return (LOG2E * DIFF_SLOPES[e // 2]) * ((j - i) * t).astype(F32)
        return 0.0

    def vt(j, e):
        if e < N_DIFF_COPIES:
            return dvt_ref[(e // 2) * V_SLAB:(e // 2 + 1) * V_SLAB, keys(j)]
        h = e - N_DIFF_COPIES
        return fvt_ref[h * V_SLAB:(h + 1) * V_SLAB, keys(j)]

    _attn_pipeline(i + 1, N_SM_COPIES, SM_GROUP, scores, shift, vt, u_ref, umax_ref, m_ref, acc_ref)

    lp = lp_ref[...]
    lam = (jnp.exp(jnp.sum(lp[0:1] * lp[1:2], keepdims=True))
           - jnp.exp(jnp.sum(lp[2:3] * lp[3:4], keepdims=True)) + lam_init)
    for h in range(N_HEADS):
        rows = slice(h * HEAD_V, (h + 1) * HEAD_V)
        a = _softmax_out(acc_ref[2 * h]) - lam * _softmax_out(acc_ref[2 * h + 1])
        ms = jnp.sum(a * a, axis=0, keepdims=True) * (1.0 / HEAD_V)
        y = a * lax.rsqrt(ms + NORM_EPS) * og_ref[...] * (1.0 - lam_init)
        a_ref[rows, :] = y.astype(BF16)
        c_ref[rows, :] = _softmax_out(acc_ref[N_DIFF_COPIES + h]).astype(BF16)


def _softmax_mixers(dqt, dk, dvt, fqt, fk, fvt, lf, lp, og, lam_init, batch, seq):
    t = min(ATTN_TILE, seq)
    nq = seq // t
    n = batch * seq
    dbias = jnp.asarray(_diff_tables(t))
    tril = jnp.asarray(np.tril(np.ones((LANES, LANES), np.float32)), BF16)
    pos = np.arange(t)
    causal = np.where(pos[:, None] <= pos[None, :], 0.0, NEG)
    fmask = jnp.asarray(np.stack([np.zeros((t, t)), causal]).astype(np.float32))
    n_groups = N_SM_COPIES // SM_GROUP
    q_blk = pl.BlockSpec((QK_WIDTH, t), lambda b, i: (0, b * nq + i))
    k_blk = pl.BlockSpec((seq, QK_WIDTH), lambda b, i: (b, 0))
    v_blk = pl.BlockSpec((VT_ROWS, seq), lambda b, i: (0, b))
    out_blk = pl.BlockSpec((DIFF_WIDTH, t), lambda b, i: (0, b * nq + i))

    def const(x):
        return pl.BlockSpec(x.shape, lambda b, i: (0,) * x.ndim)

    return pl.pallas_call(
        functools.partial(_softmax_mixers_kernel, lam_init),
        grid=(batch, nq),
        in_specs=[q_blk, k_blk, v_blk, q_blk, k_blk, v_blk,
                  pl.BlockSpec((seq, LANES), lambda b, i: (b, 0)),
                  const(dbias), const(fmask), const(tril), const(lp), const(og)],
        out_specs=[out_blk, out_blk],
        out_shape=[jax.ShapeDtypeStruct((DIFF_WIDTH, n), BF16)] * 2,
        scratch_shapes=[
            pltpu.VMEM((N_DIFF_COPIES // SM_GROUP, QK_WIDTH, SM_GROUP * t), BF16),
            pltpu.VMEM((N_HEADS // SM_GROUP, QK_WIDTH, SM_GROUP * t), BF16),
            pltpu.VMEM((n_groups, t, SM_GROUP * t), F32),
            pltpu.VMEM((n_groups, 1, SM_GROUP * t), F32),
            pltpu.VMEM((N_SM_COPIES, 1, t), F32),
            pltpu.VMEM((N_SM_COPIES, V_SLAB, t), F32),
            pltpu.VMEM((N_HEADS, seq, LANES), F32),
        ],
        compiler_params=_cparams("parallel", "arbitrary"),
        name="softmax_mixers",
    )(dqt, dk, dvt, fqt, fk, fvt, lf, dbias, fmask, tril, lp, og)


GLA_LEVELS = (1, 2, 4, 8, 16, 32)
N_LEVELS = len(GLA_LEVELS)


def _gla_tables():
    c = CHUNK
    idx = np.arange(c)
    pair = np.zeros((N_LEVELS + 1, c, c), np.float32)
    for l, m in enumerate(GLA_LEVELS):
        upper = idx % (2 * m) >= m
        same = (idx[:, None] // (2 * m)) == (idx[None, :] // (2 * m))
        pair[l] = same & upper[:, None] & (~upper)[None, :]
    pair[N_LEVELS] = np.eye(c)
    pair = np.tile(pair, (1, N_HEADS, 1))
    tril = np.tril(np.ones((c, c), np.float32))
    return tril, pair


def _block_ref(b, m):
    c, w = b.shape
    if 2 * m >= 8:
        b3 = b.reshape(c // (2 * m), 2 * m, w)
        return jnp.broadcast_to(b3[:, m - 1:m, :], b3.shape).reshape(c, w)
    b3 = b.reshape(c // 8, 8, w)
    sub = lax.broadcasted_iota(jnp.int32, (1, 8, 1), 1)
    ref = jnp.broadcast_to(b3[:, m - 1:m, :], b3.shape)
    for start in range(2 * m, 8, 2 * m):
        pick = jnp.broadcast_to(b3[:, start + m - 1:start + m, :], b3.shape)
        ref = jnp.where(sub >= start, pick, ref)
    return ref.reshape(c, w)


def _gla_kernel(q_ref, k_ref, v_ref, r_ref, a_ref, tril_ref, pair_ref, og_ref, o_ref, state_ref):
    c = CHUNK
    hk = N_HEADS * GLA_K_DIM
    seqs = range(q_ref.shape[0])

    @pl.when(pl.program_id(1) == 0)
    def _():
        state_ref[...] = jnp.zeros_like(state_ref)

    col = lax.broadcasted_iota(jnp.int32, (1, hk), 1)
    head_cols = [col // GLA_K_DIM == h for h in range(N_HEADS)]

    def head_only(x, h):
        return jnp.where(head_cols[h], x, jnp.zeros_like(x))

    def stack_heads(x):
        x = x.astype(BF16)
        return jnp.concatenate([head_only(x, h) for h in range(N_HEADS)], axis=0)

    def chunk(ci, carry):
        r0 = pl.multiple_of(ci * c, c)
        rows = pl.ds(r0, c)
        tril = tril_ref[...]
        q = [q_ref[s, rows, :] for s in seqs]
        k = [k_ref[s, rows, :] for s in seqs]
        v = [v_ref[s, rows, :] for s in seqs]
        b = []
        for s in seqs:
            a_hi, a_mid, a_lo = _split3(a_ref[s, rows, :])
            b.append(_dot(tril, a_hi) + _dot(tril, a_mid) + _dot(tril, a_lo))
        b_last = [b[s][c - 1:c, :] for s in seqs]

        attn = [pair_ref[N_LEVELS] * _dot_nt(stack_heads(q[s]), k[s].astype(BF16)) for s in seqs]
        for l, m in enumerate(GLA_LEVELS):
            for s in seqs:
                ref = _block_ref(b[s], m)
                ql = q[s] * jnp.exp(jnp.minimum(b[s] - ref, 0.0))
                kl = k[s] * jnp.exp(jnp.minimum(ref - b[s], 0.0))
                attn[s] = attn[s] + pair_ref[l] * _dot_nt(stack_heads(ql), kl.astype(BF16))

        for s in seqs:
            attn_s = attn[s].astype(BF16)
            state_t = state_ref[s]
            o_inter = _dot_nt(stack_heads(q[s] * jnp.exp(b[s])), state_t.astype(BF16))
            k_dec = (k[s] * jnp.exp(b_last[s] - b[s])).astype(BF16)
            upd = None
            for h in range(N_HEADS):
                hr = slice(h * c, (h + 1) * c)
                hv = slice(h * GLA_V_DIM, (h + 1) * GLA_V_DIM)
                o = o_inter[hr] + _dot(attn_s[hr], v[s][:, hv])
                y = _rms(o, og_ref[...])
                gate = r_ref[s, rows, hv]
                o_ref[s, rows, hv] = (y * (gate * jax.nn.sigmoid(gate))).astype(BF16)
                term = _dot_tn(v[s][:, hv], head_only(k_dec, h))
                upd = term if upd is None else upd + term
            state_ref[s] = state_t * jnp.exp(b_last[s]) + upd
        return carry

    lax.fori_loop(0, q_ref.shape[1] // c, chunk, 0)


def _gla(gq, gk, gv, gr, la, og, batch, seq):
    tg = min(GLA_TILE, seq)
    ng = seq // tg
    together = GLA_SEQS if batch % GLA_SEQS == 0 else 1
    tril_np, pair_np = _gla_tables()
    tril = jnp.asarray(tril_np, BF16)
    pair = jnp.asarray(pair_np)
    hk = N_HEADS * GLA_K_DIM
    hv = N_HEADS * GLA_V_DIM

    def seq_major(x):
        return x.reshape(batch // together, together, seq, x.shape[-1])

    def blk(width):
        return pl.BlockSpec((None, together, tg, width), lambda b, i: (b, 0, i, 0))

    out = pl.pallas_call(
        _gla_kernel,
        grid=(batch // together, ng),
        in_specs=[blk(hk), blk(hk), blk(hv), blk(hv), blk(hk),
                  pl.BlockSpec(tril.shape, lambda b, i: (0, 0)),
                  pl.BlockSpec(pair.shape, lambda b, i: (0, 0, 0)),
                  pl.BlockSpec(og.shape, lambda b, i: (0, 0))],
        out_specs=blk(hv),
        out_shape=jax.ShapeDtypeStruct((batch // together, together, seq, hv), BF16),
        scratch_shapes=[pltpu.VMEM((together, GLA_V_DIM, hk), F32)],
        compiler_params=_cparams("parallel", "arbitrary"),
        name="gla",
    )(seq_major(gq), seq_major(gk), seq_major(gv), seq_major(gr), seq_major(la), tril, pair, og)
    return out.reshape(batch * seq, hv)


def _in_proj_weights(w_in):
    def zeros(width):
        return jnp.zeros((D_MODEL, width), w_in.dtype)

    w = jnp.concatenate([w_in[:, 256:512], w_in[:, 768:2304], w_in[:, 2576:2832], w_in[:, 2304:2320],
                         w_in[:, IN_WIDTH - N_HEADS:], zeros(LANES - GLA_GATE_RANK - N_HEADS)], axis=1)

    def value_slabs(lo):
        parts = []
        for h in range(N_HEADS):
            parts += [w_in[:, lo + h * HEAD_V:lo + (h + 1) * HEAD_V], zeros(V_SLAB - HEAD_V)]
        return parts

    wt = jnp.concatenate([w_in[:, 0:256]] + value_slabs(512) + [w_in[:, 2320:2576]] + value_slabs(2832), axis=1)
    assert w.shape[1] == PROJ_WIDTH and wt.shape[1] == PROJ_T_ROWS
    return w.astype(BF16), wt.T.astype(BF16)


def _ffn_weights(w13, w2):
    pad = D_FF_PAD - D_FF
    wg = jnp.pad(w13[:, :D_FF], ((0, 0), (0, pad))).astype(BF16)
    wu = jnp.pad(w13[:, D_FF:], ((0, 0), (0, pad))).astype(BF16)
    return wg, wu, jnp.pad(w2, ((0, pad), (0, 0))).astype(BF16)


def _same_group(width):
    g = np.arange(QK_WIDTH) // width
    return jnp.asarray((g[:, None] == g[None, :]).astype(np.float32), BF16)


def kernel(x, ffn1_norm, ffn1_w13, ffn1_w2, mix_norm, w_in, w_out, diff_q_norm, diff_k_norm,
           diff_lambda, diff_out_norm, gla_alpha_w2, gla_alpha_b, gla_out_norm, fox_q_norm,
           fox_k_norm, fox_f_bias, ffn2_norm, ffn2_w13, ffn2_w2):
    batch, seq, _ = x.shape
    x2d = x.reshape(batch * seq, D_MODEL)
    grp32 = _same_group(DIFF_QK_DIM)
    grp64 = _same_group(HEAD_V)
    vone_np = np.zeros((VT_ROWS, 1), np.float32)
    vone_np[HEAD_V::V_SLAB, 0] = 1.0
    vone = jnp.asarray(vone_np)

    for i in range(DEPTH):
        wg, wu, w2p = _ffn_weights(ffn1_w13[i], ffn1_w2[i])
        x2d = _ffn(x2d, ffn1_norm[i].reshape(1, D_MODEL), wg, wu, w2p)

        w, wt = _in_proj_weights(w_in[i])
        kg = jnp.stack([jnp.tile(diff_k_norm[i], QK_WIDTH // DIFF_QK_DIM), jnp.tile(fox_k_norm[i], N_HEADS)])
        qgt = jnp.stack([jnp.tile(diff_q_norm[i], QK_WIDTH // DIFF_QK_DIM) * (LOG2E * DIFF_QK_DIM ** -0.5),
                         jnp.tile(fox_q_norm[i], N_HEADS) * (LOG2E * HEAD_V ** -0.5)], axis=1)
        aw2 = jnp.pad(gla_alpha_w2[i], ((0, LANES - GLA_GATE_RANK), (0, 0)))
        ab = gla_alpha_b[i].reshape(1, -1)
        fb = jnp.pad(fox_f_bias[i], (FOX_GATE_COL, LANES - FOX_GATE_COL - N_HEADS)).reshape(1, LANES)
        (dqt, dk, dvt, gq, gk, gv, gr, la, fqt, fk, fvt, lf) = _proj(
            x2d, mix_norm[i].reshape(1, D_MODEL), w, wt, kg, qgt, grp32, grp64, aw2, ab, fb, vone)

        lam_init = 0.8 - 0.6 * math.exp(-0.3 * i)
        at, ct = _softmax_mixers(dqt, dk, dvt, fqt, fk, fvt, lf, diff_lambda[i],
                                 diff_out_norm[i].reshape(HEAD_V, 1), lam_init, batch, seq)
        g = _gla(gq, gk, gv, gr, la, gla_out_norm[i].reshape(1, GLA_V_DIM), batch, seq)
        wg, wu, w2p = _ffn_weights(ffn2_w13[i], ffn2_w2[i])
        x2d = _mix_ffn(x2d, at, g, ct, w_out[i].astype(BF16), ffn2_norm[i].reshape(1, D_MODEL), wg, wu, w2p)
    return x2d.reshape(batch, seq, D_MODEL)
```

```python
import functools
import math

import numpy as np
import jax
import jax.numpy as jnp
from jax import lax
from jax.experimental import pallas as pl
from jax.experimental.pallas import tpu as pltpu

F32 = jnp.float32
BF16 = jnp.bfloat16

D_MODEL = 1024
D_FF = 2752
DEPTH = 4
CHUNK = 64
N_HEADS = 4
DIFF_QK_DIM = 32
HEAD_V = 64
GLA_K_DIM = 64
GLA_V_DIM = 128
GLA_GATE_RANK = 16
GLA_TAU = 16.0
NORM_EPS = 1e-6
IN_WIDTH = 3092
DIFF_WIDTH = N_HEADS * HEAD_V
GLA_WIDTH = N_HEADS * GLA_V_DIM

LANES = 128
FF_CHUNK = 256
D_FF_PAD = 2816
N_FF_CHUNKS = D_FF_PAD // FF_CHUNK
TOKEN_TILE = 512
ATTN_TILE = 256
GLA_TILE = 512
GLA_SEQS = 4
NEG = -1e30
LOG2E = math.log2(math.e)
VMEM_LIMIT = 56 * 1024 * 1024

QK_WIDTH = 256
V_SLAB = 80
VT_ROWS = N_HEADS * V_SLAB

C_DK, C_GQ, C_GK, C_GV, C_GR, C_FK, C_GATES = 0, 256, 512, 768, 1280, 1792, 2048
PROJ_WIDTH = 2176
FOX_GATE_COL = GLA_GATE_RANK
R_DQ, R_DV, R_FQ, R_FV = 0, 256, 576, 832
PROJ_T_ROWS = 1152

NT_DIMS = (((1,), (1,)), ((), ()))
TN_DIMS = (((0,), (0,)), ((), ()))


def _cparams(*sem):
    return pltpu.CompilerParams(dimension_semantics=sem, vmem_limit_bytes=VMEM_LIMIT)


def _layer_spec(stack, layer, **kwargs):
    zeros = (0,) * (stack.ndim - 1)
    return pl.BlockSpec((None,) + stack.shape[1:], lambda *_: (layer,) + zeros, **kwargs)


def _dot(a, b):
    return jnp.dot(a, b, preferred_element_type=F32)


def _dot_nt(a, b):
    return lax.dot_general(a, b, NT_DIMS, preferred_element_type=F32)


def _dot_tn(a, b):
    return lax.dot_general(a, b, TN_DIMS, preferred_element_type=F32)


def _split2(x):
    hi = x.astype(BF16)
    lo = (x - hi.astype(F32)).astype(BF16)
    return hi, lo


def _split3(x):
    hi = x.astype(BF16)
    r = x - hi.astype(F32)
    mid = r.astype(BF16)
    lo = (r - mid.astype(F32)).astype(BF16)
    return hi, mid, lo


def _log_sigmoid(z):
    return jnp.minimum(z, 0.0) - jnp.log(1.0 + jnp.exp(-jnp.abs(z)))


def _rms(x, g):
    ms = jnp.mean(x * x, axis=-1, keepdims=True)
    return x * lax.rsqrt(ms + NORM_EPS) * g


def _swiglu_residual(x, g_ref, wg_ref, wu_ref, w2_ref):
    h = _rms(x, g_ref[...]).astype(BF16)
    acc = None
    for c in range(N_FF_CHUNKS):
        cols = slice(c * FF_CHUNK, (c + 1) * FF_CHUNK)
        gate = _dot(h, wg_ref[:, cols])
        up = _dot(h, wu_ref[:, cols])
        act = (gate * jax.nn.sigmoid(gate) * up).astype(BF16)
        term = _dot(act, w2_ref[cols, :])
        acc = term if acc is None else acc + term
    return x + 0.5 * acc


def _ffn_kernel(x_ref, g_ref, wg_ref, wu_ref, w2_ref, o_ref):
    o_ref[...] = _swiglu_residual(x_ref[...], g_ref, wg_ref, wu_ref, w2_ref)


def _mix_ffn_kernel(x_ref, at_ref, gl_ref, ct_ref, wo_ref, g_ref, wg_ref, wu_ref, w2_ref, o_ref):
    mix = _dot_tn(at_ref[...], wo_ref[0:DIFF_WIDTH, :])
    mix = mix + _dot(gl_ref[...], wo_ref[DIFF_WIDTH:DIFF_WIDTH + GLA_WIDTH, :])
    mix = mix + _dot_tn(ct_ref[...], wo_ref[DIFF_WIDTH + GLA_WIDTH:, :])
    o_ref[...] = _swiglu_residual(x_ref[...] + mix, g_ref, wg_ref, wu_ref, w2_ref)


ONCE = dict(pipeline_mode=pl.Buffered(1))


def _ffn_specs(layer, gain, wg, wu, w2):
    return [_layer_spec(gain, layer), _layer_spec(wg, layer, **ONCE), _layer_spec(wu, layer, **ONCE),
            _layer_spec(w2, layer, **ONCE)]


def _ffn(x2d, layer, gain, wg, wu, w2):
    n = x2d.shape[0]
    tm = min(TOKEN_TILE, n)
    row = lambda i: (i, 0)
    return pl.pallas_call(
        _ffn_kernel,
        grid=(n // tm,),
        in_specs=[pl.BlockSpec((tm, D_MODEL), row)] + _ffn_specs(layer, gain, wg, wu, w2),
        out_specs=pl.BlockSpec((tm, D_MODEL), row),
        out_shape=jax.ShapeDtypeStruct((n, D_MODEL), F32),
        compiler_params=_cparams("parallel"),
        name="ffn",
    )(x2d, gain, wg, wu, w2)


def _mix_ffn(x2d, layer, at, gl, ct, wo, gain, wg, wu, w2):
    n = x2d.shape[0]
    tm = min(TOKEN_TILE, n)
    row = lambda i: (i, 0)
    colblk = lambda i: (0, i)
    return pl.pallas_call(
        _mix_ffn_kernel,
        grid=(n // tm,),
        in_specs=[pl.BlockSpec((tm, D_MODEL), row), pl.BlockSpec((DIFF_WIDTH, tm), colblk),
                  pl.BlockSpec((tm, GLA_WIDTH), row), pl.BlockSpec((DIFF_WIDTH, tm), colblk),
                  _layer_spec(wo, layer, **ONCE)] + _ffn_specs(layer, gain, wg, wu, w2),
        out_specs=pl.BlockSpec((tm, D_MODEL), row),
        out_shape=jax.ShapeDtypeStruct((n, D_MODEL), F32),
        compiler_params=_cparams("parallel"),
        name="mix_ffn",
    )(x2d, at, gl, ct, wo, gain, wg, wu, w2)


def _group_norm_lanes(t, grp, inv_d, gain):
    hi, lo = _split2(t * t)
    ms = (_dot(hi, grp) + _dot(lo, grp)) * inv_d
    return t * lax.rsqrt(ms + NORM_EPS) * gain


def _group_norm_rows(t, d, gain):
    rows, cols = t.shape
    t3 = t.reshape(rows // d, d, cols)
    ms = jnp.sum(t3 * t3, axis=1, keepdims=True) * (1.0 / d)
    return (t3 * lax.rsqrt(ms + NORM_EPS)).reshape(rows, cols) * gain


def _proj_kernel(x_ref, g_ref, w_ref, wt_ref, kg_ref, qgt_ref, grp32_ref, grp64_ref, aw2_ref,
                 ab_ref, fb_ref, vone_ref,
                 dqt_ref, dk_ref, dvt_ref, gq_ref, gk_ref, gv_ref, gr_ref, la_ref,
                 fqt_ref, fk_ref, fvt_ref, lf_ref):
    h = _rms(x_ref[...], g_ref[...]).astype(BF16)

    p_tok = _dot(h, w_ref[...])
    p_feat = _dot_nt(wt_ref[...], h)

    def proj(lo, width):
        return p_tok[:, lo:lo + width]

    def proj_t(lo, rows):
        return p_feat[lo:lo + rows, :]

    vone = vone_ref[...]
    dqt_ref[...] = _group_norm_rows(proj_t(R_DQ, QK_WIDTH), DIFF_QK_DIM, qgt_ref[:, 0:1]).astype(BF16)
    dk_ref[...] = _group_norm_lanes(proj(C_DK, QK_WIDTH), grp32_ref[...], 1.0 / DIFF_QK_DIM,
                                    kg_ref[0:1]).astype(BF16)
    dvt_ref[...] = (proj_t(R_DV, VT_ROWS) + vone).astype(BF16)
    gq_ref[...] = proj(C_GQ, 256) * (GLA_K_DIM ** -0.5)
    gk_ref[...] = proj(C_GK, 256)
    gv_ref[...] = proj(C_GV, 512).astype(BF16)
    gr_ref[...] = proj(C_GR, 512)
    fqt_ref[...] = _group_norm_rows(proj_t(R_FQ, QK_WIDTH), HEAD_V, qgt_ref[:, 1:2]).astype(BF16)
    fk_ref[...] = _group_norm_lanes(proj(C_FK, QK_WIDTH), grp64_ref[...], 1.0 / HEAD_V,
                                    kg_ref[1:2]).astype(BF16)
    fvt_ref[...] = (proj_t(R_FV, VT_ROWS) + vone).astype(BF16)

    gates = proj(C_GATES, LANES)
    ga_hi, ga_lo = _split2(gates)
    w_hi, w_lo = _split2(aw2_ref[...])
    z = _dot(ga_hi, w_hi) + _dot(ga_hi, w_lo) + _dot(ga_lo, w_hi) + ab_ref[...]
    la_ref[...] = _log_sigmoid(z) * (1.0 / GLA_TAU)
    lf_ref[...] = _log_sigmoid(gates + fb_ref[...])


def _proj(x2d, layer, gain, w, wt, kg, qgt, grp32, grp64, aw2, ab, fb, vone):
    n = x2d.shape[0]
    tm = min(TOKEN_TILE, n)
    row = lambda i: (i, 0)
    colblk = lambda i: (0, i)
    const = lambda i: (0, 0)

    def full(a):
        return pl.BlockSpec(a.shape, const)

    def per_layer(a):
        return _layer_spec(a, layer)

    def tok(width, dt):
        return jax.ShapeDtypeStruct((n, width), dt), pl.BlockSpec((tm, width), row)

    def feat(rows, dt):
        return jax.ShapeDtypeStruct((rows, n), dt), pl.BlockSpec((rows, tm), colblk)

    outs = [feat(QK_WIDTH, BF16), tok(QK_WIDTH, BF16), feat(VT_ROWS, BF16), tok(256, F32), tok(256, F32),
            tok(512, BF16), tok(512, F32), tok(256, F32), feat(QK_WIDTH, BF16), tok(QK_WIDTH, BF16),
            feat(VT_ROWS, BF16), tok(LANES, F32)]
    return pl.pallas_call(
        _proj_kernel,
        grid=(n // tm,),
        in_specs=[pl.BlockSpec((tm, D_MODEL), row), per_layer(gain), per_layer(w), per_layer(wt),
                  per_layer(kg), per_layer(qgt), full(grp32), full(grp64), per_layer(aw2), per_layer(ab),
                  per_layer(fb), full(vone)],
        out_specs=[o[1] for o in outs],
        out_shape=[o[0] for o in outs],
        compiler_params=_cparams("parallel"),
        name="proj_in",
    )(x2d, gain, w, wt, kg, qgt, grp32, grp64, aw2, ab, fb, vone)


def _max_over_rows(u):
    rows = u.shape[0]
    while rows > 8:
        rows //= 2
        u = jnp.maximum(u[:rows], u[rows:])
    return jnp.max(u, axis=0, keepdims=True)


def _attn_pipeline(n_blocks, n_stack, group, scores_of, shift_of, vt_of, u_ref, umax_ref, m_ref, acc_ref):
    t = u_ref.shape[1]
    n_groups = n_stack // group
    n_slots = u_ref.shape[0]
    assert n_slots == max(n_groups, 2)
    m_ref[...] = jnp.full_like(m_ref, NEG)
    acc_ref[...] = jnp.zeros_like(acc_ref)

    def slot(j, p):
        return p if n_slots == n_groups else (j * n_groups + p) % n_slots

    def issue(j, p, dst):
        u = scores_of(j, p)
        u_ref[dst] = u
        umax_ref[dst] = _max_over_rows(u)

    issue(0, 0, slot(0, 0))

    def body(j, carry):
        j_next = jnp.minimum(j + 1, n_blocks - 1)
        for p in range(n_groups):
            if p + 1 < n_groups:
                issue(j, p + 1, slot(j, p + 1))
            else:
                issue(j_next, 0, slot(j + 1, 0))
            src = slot(j, p)
            for e in range(group * p, group * (p + 1)):
                lanes = slice((e % group) * t, (e % group + 1) * t)
                u = u_ref[src, :, lanes]
                c = shift_of(j, e)
                m_old = m_ref[e]
                m_new = jnp.maximum(m_old, umax_ref[src, :, lanes] + c)
                alpha = jnp.exp2(m_old - m_new)
                p_t = jnp.exp2(u - (m_new - c)).astype(BF16)
                acc_ref[e] = alpha * acc_ref[e] + _dot(vt_of(j, e), p_t)
                m_ref[e] = m_new
        return carry

    lax.fori_loop(0, n_blocks // 2, lambda jj, carry: body(2 * jj + 1, body(2 * jj, carry)), 0)
    lax.fori_loop(0, n_blocks % 2, lambda _, carry: body(n_blocks - 1, carry), 0)


def _stack_queries(qt, qst_ref, n_stack, group):
    depth = qt.shape[0] // n_stack
    row = lax.broadcasted_iota(jnp.int32, (qt.shape[0], 1), 0)
    zero = jnp.zeros_like(qt)
    for p in range(n_stack // group):
        qst_ref[p] = jnp.concatenate(
            [jnp.where(row // depth == e, qt, zero) for e in range(group * p, group * (p + 1))], axis=1)


def _softmax_out(acc):
    return acc[0:HEAD_V] / acc[HEAD_V:HEAD_V + 1]


DIFF_SLOPES = tuple(2.0 ** (-8.0 * (h + 1) / N_HEADS) for h in range(N_HEADS))


def _diff_tables(t):
    sl = np.arange(t)[:, None]
    tl = np.arange(t)[None, :]
    allowed = (sl // CHUNK) <= (tl // CHUNK)
    past = [LOG2E * slope * np.broadcast_to(sl, (t, t)) for slope in DIFF_SLOPES]
    diag = [np.where(allowed, LOG2E * slope * (tl - np.abs(tl - sl)), NEG) for slope in DIFF_SLOPES]
    return np.stack(past + diag).astype(np.float32)


SM_GROUP = 4
N_DIFF_COPIES = 2 * N_HEADS
N_SM_COPIES = N_DIFF_COPIES + N_HEADS


def _softmax_mixers_kernel(lam_init, dqt_ref, dk_ref, dvt_ref, fqt_ref, fk_ref, fvt_ref, lf_ref,
                           dbias_ref, fmask_ref, tril_ref, lp_ref, og_ref, a_ref, c_ref,
                           dqst_ref, fqst_ref, u_ref, umax_ref, m_ref, acc_ref, fneg_ref):
    t = dqt_ref.shape[1]
    seq = dk_ref.shape[0]
    i = pl.program_id(1)
    n_diff_groups = N_DIFF_COPIES // SM_GROUP

    @pl.when(i == 0)
    def _():
        tril = tril_ref[...]
        for blk in range(seq // LANES):
            rows = slice(blk * LANES, (blk + 1) * LANES)
            x = jnp.concatenate(
                [jnp.broadcast_to(lf_ref[rows, FOX_GATE_COL + h:FOX_GATE_COL + h + 1], (LANES, LANES))
                 for h in range(N_HEADS)], axis=1)
            hi, mid, lo = _split3(x)
            f = _dot(tril, hi) + _dot(tril, mid) + _dot(tril, lo)
            for h in range(N_HEADS):
                fneg_ref[h, rows, :] = f[:, h * LANES:(h + 1) * LANES]
        for h in range(N_HEADS):
            carry = jnp.zeros((1, LANES), F32)
            for blk in range(seq // LANES):
                rows = slice(blk * LANES, (blk + 1) * LANES)
                f = fneg_ref[h, rows, :] + carry
                fneg_ref[h, rows, :] = f * (-LOG2E)
                carry = f[LANES - 1:LANES, :]

    _stack_queries(dqt_ref[...], dqst_ref, N_DIFF_COPIES, SM_GROUP)
    _stack_queries(fqt_ref[...], fqst_ref, N_HEADS, SM_GROUP)

    def keys(j):
        return pl.ds(pl.multiple_of(j * t, t), t)

    def scores(j, p):
        if p < n_diff_groups:
            first = jnp.where(j == i, N_HEADS, 0) + p * (SM_GROUP // 2)
            bias = []
            for h in range(SM_GROUP // 2):
                bias += [dbias_ref[first + h]] * 2
            return _dot(dk_ref[keys(j), :], dqst_ref[p]) + jnp.concatenate(bias, axis=1)
        mask = fmask_ref[jnp.where(j == i, 1, 0)]
        bias = [jnp.concatenate([fneg_ref[h, keys(j), :]] * (t // LANES), axis=1) + mask
                for h in range(N_HEADS)]
        return _dot(fk_ref[keys(j), :], fqst_ref[0]) + jnp.concatenate(bias, axis=1)

    def shift(j, e):
        if e < N_DIFF_COPIES:
            return (LOG2E * DIFF_SLOPES[e // 2]) * ((j - i) * t).astype(F32)
        return 0.0

    def vt(j, e):
        if e < N_DIFF_COPIES:
            return dvt_ref[(e // 2) * V_SLAB:(e // 2 + 1) * V_SLAB, keys(j)]
        h = e - N_DIFF_COPIES
        return fvt_ref[h * V_SLAB:(h + 1) * V_SLAB, keys(j)]

    _attn_pipeline(i + 1, N_SM_COPIES, SM_GROUP, scores, shift, vt, u_ref, umax_ref, m_ref, acc_ref)

    lp = lp_ref[...]
    lam = (jnp.exp(jnp.sum(lp[0:1] * lp[1:2], keepdims=True))
           - jnp.exp(jnp.sum(lp[2:3] * lp[3:4], keepdims=True)) + lam_init)
    for h in range(N_HEADS):
        rows = slice(h * HEAD_V, (h + 1) * HEAD_V)
        a = _softmax_out(acc_ref[2 * h]) - lam * _softmax_out(acc_ref[2 * h + 1])
        ms = jnp.sum(a * a, axis=0, keepdims=True) * (1.0 / HEAD_V)
        y = a * lax.rsqrt(ms + NORM_EPS) * og_ref[...] * (1.0 - lam_init)
        a_ref[rows, :] = y.astype(BF16)
        c_ref[rows, :] = _softmax_out(acc_ref[N_DIFF_COPIES + h]).astype(BF16)


def _softmax_mixers(dqt, dk, dvt, fqt, fk, fvt, lf, layer, lp, og, lam_init, batch, seq):
    t = min(ATTN_TILE, seq)
    nq = seq // t
    n = batch * seq
    dbias = jnp.asarray(_diff_tables(t))
    tril = jnp.asarray(np.tril(np.ones((LANES, LANES), np.float32)), BF16)
    pos = np.arange(t)
    causal = np.where(pos[:, None] <= pos[None, :], 0.0, NEG)
    fmask = jnp.asarray(np.stack([np.zeros((t, t)), causal]).astype(np.float32))
    n_groups = N_SM_COPIES // SM_GROUP
    q_blk = pl.BlockSpec((QK_WIDTH, t), lambda b, i: (0, b * nq + i))
    k_blk = pl.BlockSpec((seq, QK_WIDTH), lambda b, i: (b, 0))
    v_blk = pl.BlockSpec((VT_ROWS, seq), lambda b, i: (0, b))
    out_blk = pl.BlockSpec((DIFF_WIDTH, t), lambda b, i: (0, b * nq + i))

    def const(x):
        return pl.BlockSpec(x.shape, lambda b, i: (0,) * x.ndim)

    return pl.pallas_call(
        functools.partial(_softmax_mixers_kernel, lam_init),
        grid=(batch, nq),
        in_specs=[q_blk, k_blk, v_blk, q_blk, k_blk, v_blk,
                  pl.BlockSpec((seq, LANES), lambda b, i: (b, 0)),
                  const(dbias), const(fmask), const(tril), _layer_spec(lp, layer), _layer_spec(og, layer)],
        out_specs=[out_blk, out_blk],
        out_shape=[jax.ShapeDtypeStruct((DIFF_WIDTH, n), BF16)] * 2,
        scratch_shapes=[
            pltpu.VMEM((N_DIFF_COPIES // SM_GROUP, QK_WIDTH, SM_GROUP * t), BF16),
            pltpu.VMEM((N_HEADS // SM_GROUP, QK_WIDTH, SM_GROUP * t), BF16),
            pltpu.VMEM((n_groups, t, SM_GROUP * t), F32),
            pltpu.VMEM((n_groups, 1, SM_GROUP * t), F32),
            pltpu.VMEM((N_SM_COPIES, 1, t), F32),
            pltpu.VMEM((N_SM_COPIES, V_SLAB, t), F32),
            pltpu.VMEM((N_HEADS, seq, LANES), F32),
        ],
        compiler_params=_cparams("parallel", "arbitrary"),
        name="softmax_mixers",
    )(dqt, dk, dvt, fqt, fk, fvt, lf, dbias, fmask, tril, lp, og)


GLA_LEVELS = (1, 2, 4, 8, 16, 32)
N_LEVELS = len(GLA_LEVELS)


def _gla_tables():
    c = CHUNK
    idx = np.arange(c)
    pair = np.zeros((N_LEVELS + 1, c, c), np.float32)
    for l, m in enumerate(GLA_LEVELS):
        upper = idx % (2 * m) >= m
        same = (idx[:, None] // (2 * m)) == (idx[None, :] // (2 * m))
        pair[l] = same & upper[:, None] & (~upper)[None, :]
    pair[N_LEVELS] = np.eye(c)
    pair = np.tile(pair, (1, N_HEADS, 1))
    tril = np.tril(np.ones((c, c), np.float32))
    return tril, pair


def _block_ref(b, m):
    c, w = b.shape
    if 2 * m >= 8:
        b3 = b.reshape(c // (2 * m), 2 * m, w)
        return jnp.broadcast_to(b3[:, m - 1:m, :], b3.shape).reshape(c, w)
    b3 = b.reshape(c // 8, 8, w)
    sub = lax.broadcasted_iota(jnp.int32, (1, 8, 1), 1)
    ref = jnp.broadcast_to(b3[:, m - 1:m, :], b3.shape)
    for start in range(2 * m, 8, 2 * m):
        pick = jnp.broadcast_to(b3[:, start + m - 1:start + m, :], b3.shape)
        ref = jnp.where(sub >= start, pick, ref)
    return ref.reshape(c, w)


def _gla_kernel(q_ref, k_ref, v_ref, r_ref, a_ref, tril_ref, pair_ref, og_ref, o_ref, state_ref):
    c = CHUNK
    hk = N_HEADS * GLA_K_DIM
    seqs = range(q_ref.shape[0])

    @pl.when(pl.program_id(1) == 0)
    def _():
        state_ref[...] = jnp.zeros_like(state_ref)

    col = lax.broadcasted_iota(jnp.int32, (1, hk), 1)
    head_cols = [col // GLA_K_DIM == h for h in range(N_HEADS)]

    def head_only(x, h):
        return jnp.where(head_cols[h], x, jnp.zeros_like(x))

    def stack_heads(x):
        x = x.astype(BF16)
        return jnp.concatenate([head_only(x, h) for h in range(N_HEADS)], axis=0)

    def chunk(ci, carry):
        r0 = pl.multiple_of(ci * c, c)
        rows = pl.ds(r0, c)
        tril = tril_ref[...]
        q = [q_ref[s, rows, :] for s in seqs]
        k = [k_ref[s, rows, :] for s in seqs]
        v = [v_ref[s, rows, :] for s in seqs]
        b = []
        for s in seqs:
            a_hi, a_mid, a_lo = _split3(a_ref[s, rows, :])
            b.append(_dot(tril, a_hi) + _dot(tril, a_mid) + _dot(tril, a_lo))
        b_last = [b[s][c - 1:c, :] for s in seqs]

        attn = [pair_ref[N_LEVELS] * _dot_nt(stack_heads(q[s]), k[s].astype(BF16)) for s in seqs]
        for l, m in enumerate(GLA_LEVELS):
            for s in seqs:
                ref = _block_ref(b[s], m)
                ql = q[s] * jnp.exp(jnp.minimum(b[s] - ref, 0.0))
                kl = k[s] * jnp.exp(jnp.minimum(ref - b[s], 0.0))
                attn[s] = attn[s] + pair_ref[l] * _dot_nt(stack_heads(ql), kl.astype(BF16))

        for s in seqs:
            attn_s = attn[s].astype(BF16)
            state_t = state_ref[s]
            o_inter = _dot_nt(stack_heads(q[s] * jnp.exp(b[s])), state_t.astype(BF16))
            k_dec = (k[s] * jnp.exp(b_last[s] - b[s])).astype(BF16)
            upd = None
            for h in range(N_HEADS):
                hr = slice(h * c, (h + 1) * c)
                hv = slice(h * GLA_V_DIM, (h + 1) * GLA_V_DIM)
                o = o_inter[hr] + _dot(attn_s[hr], v[s][:, hv])
                y = _rms(o, og_ref[...])
                gate = r_ref[s, rows, hv]
                o_ref[s, rows, hv] = (y * (gate * jax.nn.sigmoid(gate))).astype(BF16)
                term = _dot_tn(v[s][:, hv], head_only(k_dec, h))
                upd = term if upd is None else upd + term
            state_ref[s] = state_t * jnp.exp(b_last[s]) + upd
        return carry

    lax.fori_loop(0, q_ref.shape[1] // c, chunk, 0)


def _gla(gq, gk, gv, gr, la, layer, og, batch, seq):
    tg = min(GLA_TILE, seq)
    ng = seq // tg
    together = GLA_SEQS if batch % GLA_SEQS == 0 else 1
    tril_np, pair_np = _gla_tables()
    tril = jnp.asarray(tril_np, BF16)
    pair = jnp.asarray(pair_np)
    hk = N_HEADS * GLA_K_DIM
    hv = N_HEADS * GLA_V_DIM

    def seq_major(x):
        return x.reshape(batch // together, together, seq, x.shape[-1])

    def blk(width):
        return pl.BlockSpec((None, together, tg, width), lambda b, i: (b, 0, i, 0))

    out = pl.pallas_call(
        _gla_kernel,
        grid=(batch // together, ng),
        in_specs=[blk(hk), blk(hk), blk(hv), blk(hv), blk(hk),
                  pl.BlockSpec(tril.shape, lambda b, i: (0, 0)),
                  pl.BlockSpec(pair.shape, lambda b, i: (0, 0, 0)),
                  _layer_spec(og, layer)],
        out_specs=blk(hv),
        out_shape=jax.ShapeDtypeStruct((batch // together, together, seq, hv), BF16),
        scratch_shapes=[pltpu.VMEM((together, GLA_V_DIM, hk), F32)],
        compiler_params=_cparams("parallel", "arbitrary"),
        name="gla",
    )(seq_major(gq), seq_major(gk), seq_major(gv), seq_major(gr), seq_major(la), tril, pair, og)
    return out.reshape(batch * seq, hv)


def _in_proj_weights(w_in):
    def cols(lo, hi):
        return w_in[..., lo:hi]

    def zeros(width):
        return jnp.zeros(w_in.shape[:-1] + (width,), w_in.dtype)

    w = jnp.concatenate([cols(256, 512), cols(768, 2304), cols(2576, 2832), cols(2304, 2320),
                         cols(IN_WIDTH - N_HEADS, IN_WIDTH), zeros(LANES - GLA_GATE_RANK - N_HEADS)], axis=-1)

    def value_slabs(lo):
        parts = []
        for h in range(N_HEADS):
            parts += [cols(lo + h * HEAD_V, lo + (h + 1) * HEAD_V), zeros(V_SLAB - HEAD_V)]
        return parts

    wt = jnp.concatenate([cols(0, 256)] + value_slabs(512) + [cols(2320, 2576)] + value_slabs(2832), axis=-1)
    assert w.shape[-1] == PROJ_WIDTH and wt.shape[-1] == PROJ_T_ROWS
    return w.astype(BF16), jnp.swapaxes(wt, -1, -2).astype(BF16)


def _ffn_weights(w13, w2):
    pad = D_FF_PAD - D_FF
    lead = [(0, 0)] * (w13.ndim - 2)
    wg = jnp.pad(w13[..., :D_FF], lead + [(0, 0), (0, pad)]).astype(BF16)
    wu = jnp.pad(w13[..., D_FF:], lead + [(0, 0), (0, pad)]).astype(BF16)
    return wg, wu, jnp.pad(w2, lead + [(0, pad), (0, 0)]).astype(BF16)


def _same_group(width):
    g = np.arange(QK_WIDTH) // width
    return jnp.asarray((g[:, None] == g[None, :]).astype(np.float32), BF16)


def kernel(x, ffn1_norm, ffn1_w13, ffn1_w2, mix_norm, w_in, w_out, diff_q_norm, diff_k_norm,
           diff_lambda, diff_out_norm, gla_alpha_w2, gla_alpha_b, gla_out_norm, fox_q_norm,
           fox_k_norm, fox_f_bias, ffn2_norm, ffn2_w13, ffn2_w2):
    batch, seq, _ = x.shape
    x2d = x.reshape(batch * seq, D_MODEL)
    grp32 = _same_group(DIFF_QK_DIM)
    grp64 = _same_group(HEAD_V)
    vone_np = np.zeros((VT_ROWS, 1), np.float32)
    vone_np[HEAD_V::V_SLAB, 0] = 1.0
    vone = jnp.asarray(vone_np)

    ffn1 = (ffn1_norm[:, None, :],) + _ffn_weights(ffn1_w13, ffn1_w2)
    ffn2 = (ffn2_norm[:, None, :],) + _ffn_weights(ffn2_w13, ffn2_w2)
    w, wt = _in_proj_weights(w_in)
    wo = w_out.astype(BF16)
    groups32 = QK_WIDTH // DIFF_QK_DIM
    kg = jnp.stack([jnp.tile(diff_k_norm, (1, groups32)), jnp.tile(fox_k_norm, (1, N_HEADS))], axis=1)
    qgt = jnp.stack([jnp.tile(diff_q_norm, (1, groups32)) * (LOG2E * DIFF_QK_DIM ** -0.5),
                     jnp.tile(fox_q_norm, (1, N_HEADS)) * (LOG2E * HEAD_V ** -0.5)], axis=2)
    aw2 = jnp.pad(gla_alpha_w2, ((0, 0), (0, LANES - GLA_GATE_RANK), (0, 0)))
    ab = gla_alpha_b[:, None, :]
    fb = jnp.pad(fox_f_bias, ((0, 0), (FOX_GATE_COL, LANES - FOX_GATE_COL - N_HEADS)))[:, None, :]
    mix_gain = mix_norm[:, None, :]
    diff_og = diff_out_norm[:, :, None]
    gla_og = gla_out_norm[:, None, :]

    for i in range(DEPTH):
        x2d = _ffn(x2d, i, *ffn1)
        (dqt, dk, dvt, gq, gk, gv, gr, la, fqt, fk, fvt, lf) = _proj(
            x2d, i, mix_gain, w, wt, kg, qgt, grp32, grp64, aw2, ab, fb, vone)
        lam_init = 0.8 - 0.6 * math.exp(-0.3 * i)
        at, ct = _softmax_mixers(dqt, dk, dvt, fqt, fk, fvt, lf, i, diff_lambda, diff_og, lam_init, batch, seq)
        g = _gla(gq, gk, gv, gr, la, i, gla_og, batch, seq)
        x2d = _mix_ffn(x2d, i, at, g, ct, wo, *ffn2)
    return x2d.reshape(batch, seq, D_MODEL)
```

```python
import functools
import math

import numpy as np
import jax
import jax.numpy as jnp
from jax import lax
from jax.experimental import pallas as pl
from jax.experimental.pallas import tpu as pltpu

F32 = jnp.float32
BF16 = jnp.bfloat16

D_MODEL = 1024
D_FF = 2752
DEPTH = 4
CHUNK = 64
N_HEADS = 4
DIFF_QK_DIM = 32
HEAD_V = 64
GLA_K_DIM = 64
GLA_V_DIM = 128
GLA_GATE_RANK = 16
GLA_TAU = 16.0
NORM_EPS = 1e-6
IN_WIDTH = 3092
DIFF_WIDTH = N_HEADS * HEAD_V
GLA_WIDTH = N_HEADS * GLA_V_DIM

LANES = 128
FF_CHUNK = 256
D_FF_PAD = 2816
N_FF_CHUNKS = D_FF_PAD // FF_CHUNK
TOKEN_TILE = 512
FFN_TILE = 1024
ATTN_TILE = 256
GLA_TILE = 512
GLA_SEQS = 4
NEG = -1e30
LOG2E = math.log2(math.e)
VMEM_LIMIT = 56 * 1024 * 1024

QK_WIDTH = 256
V_SLAB = 80
VT_ROWS = N_HEADS * V_SLAB

C_DK, C_GQ, C_GK, C_GV, C_GR, C_FK, C_GATES = 0, 256, 512, 768, 1280, 1792, 2048
PROJ_WIDTH = 2176
FOX_GATE_COL = GLA_GATE_RANK
R_DQ, R_DV, R_FQ, R_FV = 0, 256, 576, 832
PROJ_T_ROWS = 1152

NT_DIMS = (((1,), (1,)), ((), ()))
TN_DIMS = (((0,), (0,)), ((), ()))


def _cparams(*sem):
    return pltpu.CompilerParams(dimension_semantics=sem, vmem_limit_bytes=VMEM_LIMIT)


def _layer_spec(stack, layer, **kwargs):
    zeros = (0,) * (stack.ndim - 1)
    return pl.BlockSpec((None,) + stack.shape[1:], lambda *_: (layer,) + zeros, **kwargs)


def _dot(a, b):
    return jnp.dot(a, b, preferred_element_type=F32)


def _dot_nt(a, b):
    return lax.dot_general(a, b, NT_DIMS, preferred_element_type=F32)


def _dot_tn(a, b):
    return lax.dot_general(a, b, TN_DIMS, preferred_element_type=F32)


def _split2(x):
    hi = x.astype(BF16)
    lo = (x - hi.astype(F32)).astype(BF16)
    return hi, lo


def _split3(x):
    hi = x.astype(BF16)
    r = x - hi.astype(F32)
    mid = r.astype(BF16)
    lo = (r - mid.astype(F32)).astype(BF16)
    return hi, mid, lo


def _log_sigmoid(z):
    return jnp.minimum(z, 0.0) - jnp.log(1.0 + jnp.exp(-jnp.abs(z)))


def _rms(x, g):
    ms = jnp.mean(x * x, axis=-1, keepdims=True)
    return x * lax.rsqrt(ms + NORM_EPS) * g


def _swiglu_residual(x, g_ref, wg_ref, wu_ref, w2_ref):
    h = _rms(x, g_ref[...]).astype(BF16)
    acc = None
    for c in range(N_FF_CHUNKS):
        cols = slice(c * FF_CHUNK, (c + 1) * FF_CHUNK)
        gate = _dot(h, wg_ref[:, cols])
        up = _dot(h, wu_ref[:, cols])
        act = (gate * jax.nn.sigmoid(gate) * up).astype(BF16)
        term = _dot(act, w2_ref[cols, :])
        acc = term if acc is None else acc + term
    return x + 0.5 * acc


def _ffn_kernel(x_ref, g_ref, wg_ref, wu_ref, w2_ref, o_ref):
    o_ref[...] = _swiglu_residual(x_ref[...], g_ref, wg_ref, wu_ref, w2_ref)


def _mix_ffn_kernel(x_ref, at_ref, gl_ref, ct_ref, wo_ref, g_ref, wg_ref, wu_ref, w2_ref, o_ref):
    mix = _dot_tn(at_ref[...], wo_ref[0:DIFF_WIDTH, :])
    mix = mix + _dot(gl_ref[...], wo_ref[DIFF_WIDTH:DIFF_WIDTH + GLA_WIDTH, :])
    mix = mix + _dot_tn(ct_ref[...], wo_ref[DIFF_WIDTH + GLA_WIDTH:, :])
    o_ref[...] = _swiglu_residual(x_ref[...] + mix, g_ref, wg_ref, wu_ref, w2_ref)


ONCE = dict(pipeline_mode=pl.Buffered(1))


def _ffn_specs(layer, gain, wg, wu, w2):
    return [_layer_spec(gain, layer), _layer_spec(wg, layer, **ONCE), _layer_spec(wu, layer, **ONCE),
            _layer_spec(w2, layer, **ONCE)]


def _ffn(x2d, layer, gain, wg, wu, w2):
    n = x2d.shape[0]
    tm = min(FFN_TILE, n)
    row = lambda i: (i, 0)
    return pl.pallas_call(
        _ffn_kernel,
        grid=(n // tm,),
        in_specs=[pl.BlockSpec((tm, D_MODEL), row)] + _ffn_specs(layer, gain, wg, wu, w2),
        out_specs=pl.BlockSpec((tm, D_MODEL), row),
        out_shape=jax.ShapeDtypeStruct((n, D_MODEL), F32),
        compiler_params=_cparams("parallel"),
        name="ffn",
    )(x2d, gain, wg, wu, w2)


def _mix_ffn(x2d, layer, at, gl, ct, wo, gain, wg, wu, w2):
    n = x2d.shape[0]
    tm = min(FFN_TILE, n)
    row = lambda i: (i, 0)
    colblk = lambda i: (0, i)
    return pl.pallas_call(
        _mix_ffn_kernel,
        grid=(n // tm,),
        in_specs=[pl.BlockSpec((tm, D_MODEL), row), pl.BlockSpec((DIFF_WIDTH, tm), colblk),
                  pl.BlockSpec((tm, GLA_WIDTH), row), pl.BlockSpec((DIFF_WIDTH, tm), colblk),
                  _layer_spec(wo, layer, **ONCE)] + _ffn_specs(layer, gain, wg, wu, w2),
        out_specs=pl.BlockSpec((tm, D_MODEL), row),
        out_shape=jax.ShapeDtypeStruct((n, D_MODEL), F32),
        compiler_params=_cparams("parallel"),
        name="mix_ffn",
    )(x2d, at, gl, ct, wo, gain, wg, wu, w2)


def _group_norm_lanes(t, grp, inv_d, gain):
    hi, lo = _split2(t * t)
    ms = (_dot(hi, grp) + _dot(lo, grp)) * inv_d
    return t * lax.rsqrt(ms + NORM_EPS) * gain


def _group_norm_rows(t, d, gain):
    rows, cols = t.shape
    t3 = t.reshape(rows // d, d, cols)
    ms = jnp.sum(t3 * t3, axis=1, keepdims=True) * (1.0 / d)
    return (t3 * lax.rsqrt(ms + NORM_EPS)).reshape(rows, cols) * gain


def _proj_kernel(x_ref, g_ref, w_ref, wt_ref, kg_ref, qgt_ref, grp32_ref, grp64_ref, aw2_ref,
                 ab_ref, fb_ref, vone_ref,
                 dqt_ref, dk_ref, dvt_ref, gq_ref, gk_ref, gv_ref, gr_ref, la_ref,
                 fqt_ref, fk_ref, fvt_ref, lf_ref):
    h = _rms(x_ref[...], g_ref[...]).astype(BF16)

    p_tok = _dot(h, w_ref[...])
    p_feat = _dot_nt(wt_ref[...], h)

    def proj(lo, width):
        return p_tok[:, lo:lo + width]

    def proj_t(lo, rows):
        return p_feat[lo:lo + rows, :]

    vone = vone_ref[...]
    dqt_ref[...] = _group_norm_rows(proj_t(R_DQ, QK_WIDTH), DIFF_QK_DIM, qgt_ref[:, 0:1]).astype(BF16)
    dk_ref[...] = _group_norm_lanes(proj(C_DK, QK_WIDTH), grp32_ref[...], 1.0 / DIFF_QK_DIM,
                                    kg_ref[0:1]).astype(BF16)
    dvt_ref[...] = (proj_t(R_DV, VT_ROWS) + vone).astype(BF16)
    gq_ref[...] = proj(C_GQ, 256) * (GLA_K_DIM ** -0.5)
    gk_ref[...] = proj(C_GK, 256)
    gv_ref[...] = proj(C_GV, 512).astype(BF16)
    gr_ref[...] = proj(C_GR, 512)
    fqt_ref[...] = _group_norm_rows(proj_t(R_FQ, QK_WIDTH), HEAD_V, qgt_ref[:, 1:2]).astype(BF16)
    fk_ref[...] = _group_norm_lanes(proj(C_FK, QK_WIDTH), grp64_ref[...], 1.0 / HEAD_V,
                                    kg_ref[1:2]).astype(BF16)
    fvt_ref[...] = (proj_t(R_FV, VT_ROWS) + vone).astype(BF16)

    gates = proj(C_GATES, LANES)
    ga_hi, ga_lo = _split2(gates)
    w_hi, w_lo = _split2(aw2_ref[...])
    z = _dot(ga_hi, w_hi) + _dot(ga_hi, w_lo) + _dot(ga_lo, w_hi) + ab_ref[...]
    la_ref[...] = _log_sigmoid(z) * (1.0 / GLA_TAU)
    lf_ref[...] = _log_sigmoid(gates + fb_ref[...])


def _proj(x2d, layer, gain, w, wt, kg, qgt, grp32, grp64, aw2, ab, fb, vone):
    n = x2d.shape[0]
    tm = min(TOKEN_TILE, n)
    row = lambda i: (i, 0)
    colblk = lambda i: (0, i)
    const = lambda i: (0, 0)

    def full(a):
        return pl.BlockSpec(a.shape, const)

    def per_layer(a):
        return _layer_spec(a, layer)

    def tok(width, dt):
        return jax.ShapeDtypeStruct((n, width), dt), pl.BlockSpec((tm, width), row)

    def feat(rows, dt):
        return jax.ShapeDtypeStruct((rows, n), dt), pl.BlockSpec((rows, tm), colblk)

    outs = [feat(QK_WIDTH, BF16), tok(QK_WIDTH, BF16), feat(VT_ROWS, BF16), tok(256, F32), tok(256, F32),
            tok(512, BF16), tok(512, F32), tok(256, F32), feat(QK_WIDTH, BF16), tok(QK_WIDTH, BF16),
            feat(VT_ROWS, BF16), tok(LANES, F32)]
    return pl.pallas_call(
        _proj_kernel,
        grid=(n // tm,),
        in_specs=[pl.BlockSpec((tm, D_MODEL), row), per_layer(gain), per_layer(w), per_layer(wt),
                  per_layer(kg), per_layer(qgt), full(grp32), full(grp64), per_layer(aw2), per_layer(ab),
                  per_layer(fb), full(vone)],
        out_specs=[o[1] for o in outs],
        out_shape=[o[0] for o in outs],
        compiler_params=_cparams("parallel"),
        name="proj_in",
    )(x2d, gain, w, wt, kg, qgt, grp32, grp64, aw2, ab, fb, vone)


def _max_over_rows(u):
    rows = u.shape[0]
    while rows > 8:
        rows //= 2
        u = jnp.maximum(u[:rows], u[rows:])
    return jnp.max(u, axis=0, keepdims=True)


def _attn_pipeline(n_blocks, n_stack, group, scores_of, shift_of, vt_of, u_ref, umax_ref, m_ref, acc_ref):
    t = u_ref.shape[-1]
    n_groups = n_stack // group
    n_slots = u_ref.shape[0]
    assert n_slots == max(n_groups, 2)
    m_ref[...] = jnp.full_like(m_ref, NEG)
    acc_ref[...] = jnp.zeros_like(acc_ref)

    def slot(j, p):
        return p if n_slots == n_groups else (j * n_groups + p) % n_slots

    def issue(j, p, dst):
        u = scores_of(j, p)
        for g in range(group):
            u_ref[dst, g] = u[:, g * t:(g + 1) * t]
        umax_ref[dst] = _max_over_rows(u)

    issue(0, 0, slot(0, 0))

    def body(j, carry):
        j_next = jnp.minimum(j + 1, n_blocks - 1)
        for p in range(n_groups):
            if p + 1 < n_groups:
                issue(j, p + 1, slot(j, p + 1))
            else:
                issue(j_next, 0, slot(j + 1, 0))
            src = slot(j, p)
            for e in range(group * p, group * (p + 1)):
                lanes = slice((e % group) * t, (e % group + 1) * t)
                u = u_ref[src, e % group]
                c = shift_of(j, e)
                m_old = m_ref[e]
                m_new = jnp.maximum(m_old, umax_ref[src, :, lanes] + c)
                alpha = jnp.exp2(m_old - m_new)
                p_t = jnp.exp2(u - (m_new - c)).astype(BF16)
                acc_ref[e] = alpha * acc_ref[e] + _dot(vt_of(j, e), p_t)
                m_ref[e] = m_new
        return carry

    lax.fori_loop(0, n_blocks // 2, lambda jj, carry: body(2 * jj + 1, body(2 * jj, carry)), 0)
    lax.fori_loop(0, n_blocks % 2, lambda _, carry: body(n_blocks - 1, carry), 0)


def _stack_queries(qt_ref, qst_ref, n_stack, group, clear):
    depth, t = qt_ref.shape[0] // n_stack, qt_ref.shape[1]

    @pl.when(clear)
    def _():
        qst_ref[...] = jnp.zeros_like(qst_ref)

    for e in range(n_stack):
        rows = slice(e * depth, (e + 1) * depth)
        qst_ref[e // group, rows, (e % group) * t:(e % group + 1) * t] = qt_ref[rows, :]


def _softmax_out(acc):
    return acc[0:HEAD_V] / acc[HEAD_V:HEAD_V + 1]


DIFF_SLOPES = tuple(2.0 ** (-8.0 * (h + 1) / N_HEADS) for h in range(N_HEADS))


def _diff_tables(t):
    sl = np.arange(t)[:, None]
    tl = np.arange(t)[None, :]
    allowed = (sl // CHUNK) <= (tl // CHUNK)
    past = [LOG2E * slope * np.broadcast_to(sl, (t, t)) for slope in DIFF_SLOPES]
    diag = [np.where(allowed, LOG2E * slope * (tl - np.abs(tl - sl)), NEG) for slope in DIFF_SLOPES]
    return np.stack(past + diag).astype(np.float32)


SM_GROUP = 4
N_DIFF_COPIES = 2 * N_HEADS
N_SM_COPIES = N_DIFF_COPIES + N_HEADS


def _softmax_mixers_kernel(lam_init, dqt_ref, dk_ref, dvt_ref, fqt_ref, fk_ref, fvt_ref, lf_ref,
                           dbias_ref, fmask_ref, tril_ref, lp_ref, og_ref, a_ref, c_ref,
                           dqst_ref, fqst_ref, u_ref, umax_ref, m_ref, acc_ref, fneg_ref):
    t = dqt_ref.shape[1]
    seq = dk_ref.shape[0]
    i = pl.program_id(1)
    n_diff_groups = N_DIFF_COPIES // SM_GROUP

    @pl.when(i == 0)
    def _():
        tril = tril_ref[...]
        for blk in range(seq // LANES):
            rows = slice(blk * LANES, (blk + 1) * LANES)
            x = jnp.concatenate(
                [jnp.broadcast_to(lf_ref[rows, FOX_GATE_COL + h:FOX_GATE_COL + h + 1], (LANES, LANES))
                 for h in range(N_HEADS)], axis=1)
            hi, mid, lo = _split3(x)
            f = _dot(tril, hi) + _dot(tril, mid) + _dot(tril, lo)
            for h in range(N_HEADS):
                fneg_ref[h, rows, :] = f[:, h * LANES:(h + 1) * LANES]
        for h in range(N_HEADS):
            carry = jnp.zeros((1, LANES), F32)
            for blk in range(seq // LANES):
                rows = slice(blk * LANES, (blk + 1) * LANES)
                f = fneg_ref[h, rows, :] + carry
                fneg_ref[h, rows, :] = f * (-LOG2E)
                carry = f[LANES - 1:LANES, :]

    _stack_queries(dqt_ref, dqst_ref, N_DIFF_COPIES, SM_GROUP, i == 0)
    _stack_queries(fqt_ref, fqst_ref, N_HEADS, SM_GROUP, i == 0)

    def keys(j):
        return pl.ds(pl.multiple_of(j * t, t), t)

    def scores(j, p):
        if p < n_diff_groups:
            first = jnp.where(j == i, N_HEADS, 0) + p * (SM_GROUP // 2)
            bias = []
            for h in range(SM_GROUP // 2):
                bias += [dbias_ref[first + h]] * 2
            return _dot(dk_ref[keys(j), :], dqst_ref[p]) + jnp.concatenate(bias, axis=1)
        mask = fmask_ref[jnp.where(j == i, 1, 0)]
        bias = [jnp.concatenate([fneg_ref[h, keys(j), :]] * (t // LANES), axis=1) + mask
                for h in range(N_HEADS)]
        return _dot(fk_ref[keys(j), :], fqst_ref[0]) + jnp.concatenate(bias, axis=1)

    def shift(j, e):
        if e < N_DIFF_COPIES:
            return (LOG2E * DIFF_SLOPES[e // 2]) * ((j - i) * t).astype(F32)
        return 0.0

    def vt(j, e):
        if e < N_DIFF_COPIES:
            return dvt_ref[(e // 2) * V_SLAB:(e // 2 + 1) * V_SLAB, keys(j)]
        h = e - N_DIFF_COPIES
        return fvt_ref[h * V_SLAB:(h + 1) * V_SLAB, keys(j)]

    _attn_pipeline(i + 1, N_SM_COPIES, SM_GROUP, scores, shift, vt, u_ref, umax_ref, m_ref, acc_ref)

    lp = lp_ref[...]
    lam = (jnp.exp(jnp.sum(lp[0:1] * lp[1:2], keepdims=True))
           - jnp.exp(jnp.sum(lp[2:3] * lp[3:4], keepdims=True)) + lam_init)
    for h in range(N_HEADS):
        rows = slice(h * HEAD_V, (h + 1) * HEAD_V)
        a = _softmax_out(acc_ref[2 * h]) - lam * _softmax_out(acc_ref[2 * h + 1])
        ms = jnp.sum(a * a, axis=0, keepdims=True) * (1.0 / HEAD_V)
        y = a * lax.rsqrt(ms + NORM_EPS) * og_ref[...] * (1.0 - lam_init)
        a_ref[rows, :] = y.astype(BF16)
        c_ref[rows, :] = _softmax_out(acc_ref[N_DIFF_COPIES + h]).astype(BF16)


def _softmax_mixers(dqt, dk, dvt, fqt, fk, fvt, lf, layer, lp, og, lam_init, batch, seq):
    t = min(ATTN_TILE, seq)
    nq = seq // t
    n = batch * seq
    dbias = jnp.asarray(_diff_tables(t))
    tril = jnp.asarray(np.tril(np.ones((LANES, LANES), np.float32)), BF16)
    pos = np.arange(t)
    causal = np.where(pos[:, None] <= pos[None, :], 0.0, NEG)
    fmask = jnp.asarray(np.stack([np.zeros((t, t)), causal]).astype(np.float32))
    n_groups = N_SM_COPIES // SM_GROUP
    q_blk = pl.BlockSpec((QK_WIDTH, t), lambda b, i: (0, b * nq + i))
    k_blk = pl.BlockSpec((seq, QK_WIDTH), lambda b, i: (b, 0))
    v_blk = pl.BlockSpec((VT_ROWS, seq), lambda b, i: (0, b))
    out_blk = pl.BlockSpec((DIFF_WIDTH, t), lambda b, i: (0, b * nq + i))

    def const(x):
        return pl.BlockSpec(x.shape, lambda b, i: (0,) * x.ndim)

    return pl.pallas_call(
        functools.partial(_softmax_mixers_kernel, lam_init),
        grid=(batch, nq),
        in_specs=[q_blk, k_blk, v_blk, q_blk, k_blk, v_blk,
                  pl.BlockSpec((seq, LANES), lambda b, i: (b, 0)),
                  const(dbias), const(fmask), const(tril), _layer_spec(lp, layer), _layer_spec(og, layer)],
        out_specs=[out_blk, out_blk],
        out_shape=[jax.ShapeDtypeStruct((DIFF_WIDTH, n), BF16)] * 2,
        scratch_shapes=[
            pltpu.VMEM((N_DIFF_COPIES // SM_GROUP, QK_WIDTH, SM_GROUP * t), BF16),
            pltpu.VMEM((N_HEADS // SM_GROUP, QK_WIDTH, SM_GROUP * t), BF16),
            pltpu.VMEM((n_groups, SM_GROUP, t, t), F32),
            pltpu.VMEM((n_groups, 1, SM_GROUP * t), F32),
            pltpu.VMEM((N_SM_COPIES, 1, t), F32),
            pltpu.VMEM((N_SM_COPIES, V_SLAB, t), F32),
            pltpu.VMEM((N_HEADS, seq, LANES), F32),
        ],
        compiler_params=_cparams("parallel", "arbitrary"),
        name="softmax_mixers",
    )(dqt, dk, dvt, fqt, fk, fvt, lf, dbias, fmask, tril, lp, og)


GLA_LEVELS = (1, 2, 4, 8, 16, 32)
N_LEVELS = len(GLA_LEVELS)


def _gla_tables():
    c = CHUNK
    idx = np.arange(c)
    pair = np.zeros((N_LEVELS + 1, c, c), np.float32)
    for l, m in enumerate(GLA_LEVELS):
        upper = idx % (2 * m) >= m
        same = (idx[:, None] // (2 * m)) == (idx[None, :] // (2 * m))
        pair[l] = same & upper[:, None] & (~upper)[None, :]
    pair[N_LEVELS] = np.eye(c)
    pair = np.tile(pair, (1, N_HEADS, 1))
    tril = np.tril(np.ones((c, c), np.float32))
    return tril, pair


def _block_ref(b, m):
    c, w = b.shape
    if 2 * m >= 8:
        b3 = b.reshape(c // (2 * m), 2 * m, w)
        return jnp.broadcast_to(b3[:, m - 1:m, :], b3.shape).reshape(c, w)
    b3 = b.reshape(c // 8, 8, w)
    sub = lax.broadcasted_iota(jnp.int32, (1, 8, 1), 1)
    ref = jnp.broadcast_to(b3[:, m - 1:m, :], b3.shape)
    for start in range(2 * m, 8, 2 * m):
        pick = jnp.broadcast_to(b3[:, start + m - 1:start + m, :], b3.shape)
        ref = jnp.where(sub >= start, pick, ref)
    return ref.reshape(c, w)


def _gla_kernel(q_ref, k_ref, v_ref, r_ref, a_ref, tril_ref, pair_ref, og_ref, o_ref, state_ref):
    c = CHUNK
    hk = N_HEADS * GLA_K_DIM
    seqs = range(q_ref.shape[0])

    @pl.when(pl.program_id(1) == 0)
    def _():
        state_ref[...] = jnp.zeros_like(state_ref)

    col = lax.broadcasted_iota(jnp.int32, (1, hk), 1)
    head_cols = [col // GLA_K_DIM == h for h in range(N_HEADS)]

    def head_only(x, h):
        return jnp.where(head_cols[h], x, jnp.zeros_like(x))

    def stack_heads(x):
        x = x.astype(BF16)
        return jnp.concatenate([head_only(x, h) for h in range(N_HEADS)], axis=0)

    def chunk(ci, carry):
        r0 = pl.multiple_of(ci * c, c)
        rows = pl.ds(r0, c)
        tril = tril_ref[...]
        q = [q_ref[s, rows, :] for s in seqs]
        k = [k_ref[s, rows, :] for s in seqs]
        v = [v_ref[s, rows, :] for s in seqs]
        b = []
        for s in seqs:
            a_hi, a_mid, a_lo = _split3(a_ref[s, rows, :])
            b.append(_dot(tril, a_hi) + _dot(tril, a_mid) + _dot(tril, a_lo))
        b_last = [b[s][c - 1:c, :] for s in seqs]

        attn = [pair_ref[N_LEVELS] * _dot_nt(stack_heads(q[s]), k[s].astype(BF16)) for s in seqs]
        for l, m in enumerate(GLA_LEVELS):
            for s in seqs:
                ref = _block_ref(b[s], m)
                ql = q[s] * jnp.exp(jnp.minimum(b[s] - ref, 0.0))
                kl = k[s] * jnp.exp(jnp.minimum(ref - b[s], 0.0))
                attn[s] = attn[s] + pair_ref[l] * _dot_nt(stack_heads(ql), kl.astype(BF16))

        for s in seqs:
            attn_s = attn[s].astype(BF16)
            state_t = state_ref[s]
            o_inter = _dot_nt(stack_heads(q[s] * jnp.exp(b[s])), state_t.astype(BF16))
            k_dec = (k[s] * jnp.exp(b_last[s] - b[s])).astype(BF16)
            upd = None
            for h in range(N_HEADS):
                hr = slice(h * c, (h + 1) * c)
                hv = slice(h * GLA_V_DIM, (h + 1) * GLA_V_DIM)
                o = o_inter[hr] + _dot(attn_s[hr], v[s][:, hv])
                y = _rms(o, og_ref[...])
                gate = r_ref[s, rows, hv]
                o_ref[s, rows, hv] = (y * (gate * jax.nn.sigmoid(gate))).astype(BF16)
                term = _dot_tn(v[s][:, hv], head_only(k_dec, h))
                upd = term if upd is None else upd + term
            state_ref[s] = state_t * jnp.exp(b_last[s]) + upd
        return carry

    lax.fori_loop(0, q_ref.shape[1] // c, chunk, 0)


def _gla(gq, gk, gv, gr, la, layer, og, batch, seq):
    tg = min(GLA_TILE, seq)
    ng = seq // tg
    together = GLA_SEQS if batch % GLA_SEQS == 0 else 1
    tril_np, pair_np = _gla_tables()
    tril = jnp.asarray(tril_np, BF16)
    pair = jnp.asarray(pair_np)
    hk = N_HEADS * GLA_K_DIM
    hv = N_HEADS * GLA_V_DIM

    def seq_major(x):
        return x.reshape(batch // together, together, seq, x.shape[-1])

    def blk(width):
        return pl.BlockSpec((None, together, tg, width), lambda b, i: (b, 0, i, 0))

    out = pl.pallas_call(
        _gla_kernel,
        grid=(batch // together, ng),
        in_specs=[blk(hk), blk(hk), blk(hv), blk(hv), blk(hk),
                  pl.BlockSpec(tril.shape, lambda b, i: (0, 0)),
                  pl.BlockSpec(pair.shape, lambda b, i: (0, 0, 0)),
                  _layer_spec(og, layer)],
        out_specs=blk(hv),
        out_shape=jax.ShapeDtypeStruct((batch // together, together, seq, hv), BF16),
        scratch_shapes=[pltpu.VMEM((together, GLA_V_DIM, hk), F32)],
        compiler_params=_cparams("parallel", "arbitrary"),
        name="gla",
    )(seq_major(gq), seq_major(gk), seq_major(gv), seq_major(gr), seq_major(la), tril, pair, og)
    return out.reshape(batch * seq, hv)


def _in_proj_weights(w_in):
    w_in = w_in.astype(BF16)

    def cols(lo, hi):
        return w_in[..., lo:hi]

    def zeros(width):
        return jnp.zeros(w_in.shape[:-1] + (width,), w_in.dtype)

    w = jnp.concatenate([cols(256, 512), cols(768, 2304), cols(2576, 2832), cols(2304, 2320),
                         cols(IN_WIDTH - N_HEADS, IN_WIDTH), zeros(LANES - GLA_GATE_RANK - N_HEADS)], axis=-1)

    def value_slabs(lo):
        parts = []
        for h in range(N_HEADS):
            parts += [cols(lo + h * HEAD_V, lo + (h + 1) * HEAD_V), zeros(V_SLAB - HEAD_V)]
        return parts

    wt = jnp.concatenate([cols(0, 256)] + value_slabs(512) + [cols(2320, 2576)] + value_slabs(2832), axis=-1)
    assert w.shape[-1] == PROJ_WIDTH and wt.shape[-1] == PROJ_T_ROWS
    return w, jnp.swapaxes(wt, -1, -2)


def _ffn_weights(w13, w2):
    pad = D_FF_PAD - D_FF
    lead = [(0, 0)] * (w13.ndim - 2)
    wg = jnp.pad(w13[..., :D_FF], lead + [(0, 0), (0, pad)]).astype(BF16)
    wu = jnp.pad(w13[..., D_FF:], lead + [(0, 0), (0, pad)]).astype(BF16)
    return wg, wu, jnp.pad(w2, lead + [(0, pad), (0, 0)]).astype(BF16)


def _same_group(width):
    g = np.arange(QK_WIDTH) // width
    return jnp.asarray((g[:, None] == g[None, :]).astype(np.float32), BF16)


def kernel(x, ffn1_norm, ffn1_w13, ffn1_w2, mix_norm, w_in, w_out, diff_q_norm, diff_k_norm,
           diff_lambda, diff_out_norm, gla_alpha_w2, gla_alpha_b, gla_out_norm, fox_q_norm,
           fox_k_norm, fox_f_bias, ffn2_norm, ffn2_w13, ffn2_w2):
    batch, seq, _ = x.shape
    x2d = x.reshape(batch * seq, D_MODEL)
    grp32 = _same_group(DIFF_QK_DIM)
    grp64 = _same_group(HEAD_V)
    vone_np = np.zeros((VT_ROWS, 1), np.float32)
    vone_np[HEAD_V::V_SLAB, 0] = 1.0
    vone = jnp.asarray(vone_np)

    ffn1 = (ffn1_norm[:, None, :],) + _ffn_weights(ffn1_w13, ffn1_w2)
    ffn2 = (ffn2_norm[:, None, :],) + _ffn_weights(ffn2_w13, ffn2_w2)
    w, wt = _in_proj_weights(w_in)
    wo = w_out.astype(BF16)
    groups32 = QK_WIDTH // DIFF_QK_DIM
    kg = jnp.stack([jnp.tile(diff_k_norm, (1, groups32)), jnp.tile(fox_k_norm, (1, N_HEADS))], axis=1)
    qgt = jnp.stack([jnp.tile(diff_q_norm, (1, groups32)) * (LOG2E * DIFF_QK_DIM ** -0.5),
                     jnp.tile(fox_q_norm, (1, N_HEADS)) * (LOG2E * HEAD_V ** -0.5)], axis=2)
    aw2 = jnp.pad(gla_alpha_w2, ((0, 0), (0, LANES - GLA_GATE_RANK), (0, 0)))
    ab = gla_alpha_b[:, None, :]
    fb = jnp.pad(fox_f_bias, ((0, 0), (FOX_GATE_COL, LANES - FOX_GATE_COL - N_HEADS)))[:, None, :]
    mix_gain = mix_norm[:, None, :]
    diff_og = diff_out_norm[:, :, None]
    gla_og = gla_out_norm[:, None, :]

    for i in range(DEPTH):
        x2d = _ffn(x2d, i, *ffn1)
        (dqt, dk, dvt, gq, gk, gv, gr, la, fqt, fk, fvt, lf) = _proj(
            x2d, i, mix_gain, w, wt, kg, qgt, grp32, grp64, aw2, ab, fb, vone)
        lam_init = 0.8 - 0.6 * math.exp(-0.3 * i)
        at, ct = _softmax_mixers(dqt, dk, dvt, fqt, fk, fvt, lf, i, diff_lambda, diff_og, lam_init, batch, seq)
        g = _gla(gq, gk, gv, gr, la, i, gla_og, batch, seq)
        x2d = _mix_ffn(x2d, i, at, g, ct, wo, *ffn2)
    return x2d.reshape(batch, seq, D_MODEL)
```

```python
import functools
import math

import numpy as np
import jax
import jax.numpy as jnp
from jax import lax
from jax.experimental import pallas as pl
from jax.experimental.pallas import tpu as pltpu

F32 = jnp.float32
BF16 = jnp.bfloat16

D_MODEL = 1024
D_FF = 2752
DEPTH = 4
CHUNK = 64
N_HEADS = 4
DIFF_QK_DIM = 32
HEAD_V = 64
GLA_K_DIM = 64
GLA_V_DIM = 128
GLA_GATE_RANK = 16
GLA_TAU = 16.0
NORM_EPS = 1e-6
IN_WIDTH = 3092
DIFF_WIDTH = N_HEADS * HEAD_V
GLA_WIDTH = N_HEADS * GLA_V_DIM

LANES = 128
FF_CHUNK = 256
D_FF_PAD = 2816
N_FF_CHUNKS = D_FF_PAD // FF_CHUNK
TOKEN_TILE = 512
FFN_TILE = 1024
ATTN_TILE = 256
GLA_TILE = 256
GLA_SEQS = 8
NEG = -1e30
LOG2E = math.log2(math.e)
VMEM_LIMIT = 56 * 1024 * 1024

QK_WIDTH = 256
V_SLAB = 80
VT_ROWS = N_HEADS * V_SLAB

C_DK, C_GQ, C_GK, C_GV, C_GR, C_FK, C_GATES = 0, 256, 512, 768, 1280, 1792, 2048
PROJ_WIDTH = 2176
FOX_GATE_COL = GLA_GATE_RANK
R_DQ, R_DV, R_FQ, R_FV = 0, 256, 576, 832
PROJ_T_ROWS = 1152

NT_DIMS = (((1,), (1,)), ((), ()))
TN_DIMS = (((0,), (0,)), ((), ()))


def _cparams(*sem):
    return pltpu.CompilerParams(dimension_semantics=sem, vmem_limit_bytes=VMEM_LIMIT)


def _layer_spec(stack, layer, **kwargs):
    zeros = (0,) * (stack.ndim - 1)
    return pl.BlockSpec((None,) + stack.shape[1:], lambda *_: (layer,) + zeros, **kwargs)


def _dot(a, b):
    return jnp.dot(a, b, preferred_element_type=F32)


def _dot_nt(a, b):
    return lax.dot_general(a, b, NT_DIMS, preferred_element_type=F32)


def _dot_tn(a, b):
    return lax.dot_general(a, b, TN_DIMS, preferred_element_type=F32)


def _split2(x):
    hi = x.astype(BF16)
    lo = (x - hi.astype(F32)).astype(BF16)
    return hi, lo


def _split3(x):
    hi = x.astype(BF16)
    r = x - hi.astype(F32)
    mid = r.astype(BF16)
    lo = (r - mid.astype(F32)).astype(BF16)
    return hi, mid, lo


def _log_sigmoid(z):
    return jnp.minimum(z, 0.0) - jnp.log(1.0 + jnp.exp(-jnp.abs(z)))


def _rms(x, g):
    ms = jnp.mean(x * x, axis=-1, keepdims=True)
    return x * lax.rsqrt(ms + NORM_EPS) * g


def _swiglu_residual(x, g_ref, wg_ref, wu_ref, w2_ref):
    h = _rms(x, g_ref[...]).astype(BF16)
    acc = None
    for c in range(N_FF_CHUNKS):
        cols = slice(c * FF_CHUNK, (c + 1) * FF_CHUNK)
        gate = _dot(h, wg_ref[:, cols])
        up = _dot(h, wu_ref[:, cols])
        act = (gate * jax.nn.sigmoid(gate) * up).astype(BF16)
        term = _dot(act, w2_ref[cols, :])
        acc = term if acc is None else acc + term
    return x + 0.5 * acc


def _ffn_kernel(x_ref, g_ref, wg_ref, wu_ref, w2_ref, o_ref):
    o_ref[...] = _swiglu_residual(x_ref[...], g_ref, wg_ref, wu_ref, w2_ref)


def _mix_ffn_kernel(x_ref, at_ref, gl_ref, ct_ref, wo_ref, g_ref, wg_ref, wu_ref, w2_ref, o_ref):
    mix = _dot_tn(at_ref[...], wo_ref[0:DIFF_WIDTH, :])
    mix = mix + _dot(gl_ref[...], wo_ref[DIFF_WIDTH:DIFF_WIDTH + GLA_WIDTH, :])
    mix = mix + _dot_tn(ct_ref[...], wo_ref[DIFF_WIDTH + GLA_WIDTH:, :])
    o_ref[...] = _swiglu_residual(x_ref[...] + mix, g_ref, wg_ref, wu_ref, w2_ref)


ONCE = dict(pipeline_mode=pl.Buffered(1))


def _ffn_specs(layer, gain, wg, wu, w2):
    return [_layer_spec(gain, layer), _layer_spec(wg, layer, **ONCE), _layer_spec(wu, layer, **ONCE),
            _layer_spec(w2, layer, **ONCE)]


def _ffn(x2d, layer, gain, wg, wu, w2):
    n = x2d.shape[0]
    tm = min(FFN_TILE, n)
    row = lambda i: (i, 0)
    return pl.pallas_call(
        _ffn_kernel,
        grid=(n // tm,),
        in_specs=[pl.BlockSpec((tm, D_MODEL), row)] + _ffn_specs(layer, gain, wg, wu, w2),
        out_specs=pl.BlockSpec((tm, D_MODEL), row),
        out_shape=jax.ShapeDtypeStruct((n, D_MODEL), F32),
        compiler_params=_cparams("parallel"),
        name="ffn",
    )(x2d, gain, wg, wu, w2)


def _mix_ffn(x2d, layer, at, gl, ct, wo, gain, wg, wu, w2):
    n = x2d.shape[0]
    tm = min(FFN_TILE, n)
    row = lambda i: (i, 0)
    colblk = lambda i: (0, i)
    return pl.pallas_call(
        _mix_ffn_kernel,
        grid=(n // tm,),
        in_specs=[pl.BlockSpec((tm, D_MODEL), row), pl.BlockSpec((DIFF_WIDTH, tm), colblk),
                  pl.BlockSpec((tm, GLA_WIDTH), row), pl.BlockSpec((DIFF_WIDTH, tm), colblk),
                  _layer_spec(wo, layer, **ONCE)] + _ffn_specs(layer, gain, wg, wu, w2),
        out_specs=pl.BlockSpec((tm, D_MODEL), row),
        out_shape=jax.ShapeDtypeStruct((n, D_MODEL), F32),
        compiler_params=_cparams("parallel"),
        name="mix_ffn",
    )(x2d, at, gl, ct, wo, gain, wg, wu, w2)


def _group_norm_lanes(t, grp, inv_d, gain):
    hi, lo = _split2(t * t)
    ms = (_dot(hi, grp) + _dot(lo, grp)) * inv_d
    return t * lax.rsqrt(ms + NORM_EPS) * gain


def _group_norm_rows(t, d, gain):
    rows, cols = t.shape
    t3 = t.reshape(rows // d, d, cols)
    ms = jnp.sum(t3 * t3, axis=1, keepdims=True) * (1.0 / d)
    return (t3 * lax.rsqrt(ms + NORM_EPS)).reshape(rows, cols) * gain


def _proj_kernel(x_ref, g_ref, w_ref, wt_ref, kg_ref, qgt_ref, grp32_ref, grp64_ref, aw2_ref,
                 ab_ref, fb_ref, vone_ref,
                 dqt_ref, dk_ref, dvt_ref, gq_ref, gk_ref, gv_ref, gr_ref, la_ref,
                 fqt_ref, fk_ref, fvt_ref, lf_ref):
    h = _rms(x_ref[...], g_ref[...]).astype(BF16)

    p_tok = _dot(h, w_ref[...])
    p_feat = _dot_nt(wt_ref[...], h)

    def proj(lo, width):
        return p_tok[:, lo:lo + width]

    def proj_t(lo, rows):
        return p_feat[lo:lo + rows, :]

    vone = vone_ref[...]
    dqt_ref[...] = _group_norm_rows(proj_t(R_DQ, QK_WIDTH), DIFF_QK_DIM, qgt_ref[:, 0:1]).astype(BF16)
    dk_ref[...] = _group_norm_lanes(proj(C_DK, QK_WIDTH), grp32_ref[...], 1.0 / DIFF_QK_DIM,
                                    kg_ref[0:1]).astype(BF16)
    dvt_ref[...] = (proj_t(R_DV, VT_ROWS) + vone).astype(BF16)
    gq_ref[...] = proj(C_GQ, 256) * (GLA_K_DIM ** -0.5)
    gk_ref[...] = proj(C_GK, 256)
    gv_ref[...] = proj(C_GV, 512).astype(BF16)
    gr_ref[...] = proj(C_GR, 512)
    fqt_ref[...] = _group_norm_rows(proj_t(R_FQ, QK_WIDTH), HEAD_V, qgt_ref[:, 1:2]).astype(BF16)
    fk_ref[...] = _group_norm_lanes(proj(C_FK, QK_WIDTH), grp64_ref[...], 1.0 / HEAD_V,
                                    kg_ref[1:2]).astype(BF16)
    fvt_ref[...] = (proj_t(R_FV, VT_ROWS) + vone).astype(BF16)

    gates = proj(C_GATES, LANES)
    ga_hi, ga_lo = _split2(gates)
    w_hi, w_lo = _split2(aw2_ref[...])
    z = _dot(ga_hi, w_hi) + _dot(ga_hi, w_lo) + _dot(ga_lo, w_hi) + ab_ref[...]
    la_ref[...] = _log_sigmoid(z) * (1.0 / GLA_TAU)
    lf_ref[...] = _log_sigmoid(gates + fb_ref[...])


def _proj(x2d, layer, gain, w, wt, kg, qgt, grp32, grp64, aw2, ab, fb, vone):
    n = x2d.shape[0]
    tm = min(TOKEN_TILE, n)
    row = lambda i: (i, 0)
    colblk = lambda i: (0, i)
    const = lambda i: (0, 0)

    def full(a):
        return pl.BlockSpec(a.shape, const)

    def per_layer(a):
        return _layer_spec(a, layer)

    def tok(width, dt):
        return jax.ShapeDtypeStruct((n, width), dt), pl.BlockSpec((tm, width), row)

    def feat(rows, dt):
        return jax.ShapeDtypeStruct((rows, n), dt), pl.BlockSpec((rows, tm), colblk)

    outs = [feat(QK_WIDTH, BF16), tok(QK_WIDTH, BF16), feat(VT_ROWS, BF16), tok(256, F32), tok(256, F32),
            tok(512, BF16), tok(512, F32), tok(256, F32), feat(QK_WIDTH, BF16), tok(QK_WIDTH, BF16),
            feat(VT_ROWS, BF16), tok(LANES, F32)]
    return pl.pallas_call(
        _proj_kernel,
        grid=(n // tm,),
        in_specs=[pl.BlockSpec((tm, D_MODEL), row), per_layer(gain), per_layer(w), per_layer(wt),
                  per_layer(kg), per_layer(qgt), full(grp32), full(grp64), per_layer(aw2), per_layer(ab),
                  per_layer(fb), full(vone)],
        out_specs=[o[1] for o in outs],
        out_shape=[o[0] for o in outs],
        compiler_params=_cparams("parallel"),
        name="proj_in",
    )(x2d, gain, w, wt, kg, qgt, grp32, grp64, aw2, ab, fb, vone)


def _max_over_rows(u):
    rows = u.shape[0]
    while rows > 8:
        rows //= 2
        u = jnp.maximum(u[:rows], u[rows:])
    return jnp.max(u, axis=0, keepdims=True)


def _attn_pipeline(n_blocks, n_stack, group, scores_of, shift_of, vt_of, u_ref, umax_ref, m_ref, acc_ref):
    t = u_ref.shape[-1]
    n_groups = n_stack // group
    n_slots = u_ref.shape[0]
    assert n_slots == max(n_groups, 2)
    m_ref[...] = jnp.full_like(m_ref, NEG)
    acc_ref[...] = jnp.zeros_like(acc_ref)

    def slot(j, p):
        return p if n_slots == n_groups else (j * n_groups + p) % n_slots

    def issue(j, p, dst):
        u = scores_of(j, p)
        for g in range(group):
            u_ref[dst, g] = u[:, g * t:(g + 1) * t]
        umax_ref[dst] = _max_over_rows(u)

    issue(0, 0, slot(0, 0))

    def body(j, carry):
        j_next = jnp.minimum(j + 1, n_blocks - 1)
        for p in range(n_groups):
            if p + 1 < n_groups:
                issue(j, p + 1, slot(j, p + 1))
            else:
                issue(j_next, 0, slot(j + 1, 0))
            src = slot(j, p)
            for e in range(group * p, group * (p + 1)):
                lanes = slice((e % group) * t, (e % group + 1) * t)
                u = u_ref[src, e % group]
                c = shift_of(j, e)
                m_old = m_ref[e]
                m_new = jnp.maximum(m_old, umax_ref[src, :, lanes] + c)
                alpha = jnp.exp2(m_old - m_new)
                p_t = jnp.exp2(u - (m_new - c)).astype(BF16)
                acc_ref[e] = alpha * acc_ref[e] + _dot(vt_of(j, e), p_t)
                m_ref[e] = m_new
        return carry

    lax.fori_loop(0, n_blocks // 2, lambda jj, carry: body(2 * jj + 1, body(2 * jj, carry)), 0)
    lax.fori_loop(0, n_blocks % 2, lambda _, carry: body(n_blocks - 1, carry), 0)


def _stack_queries(qt_ref, qst_ref, n_stack, group, clear):
    depth, t = qt_ref.shape[0] // n_stack, qt_ref.shape[1]

    @pl.when(clear)
    def _():
        qst_ref[...] = jnp.zeros_like(qst_ref)

    for e in range(n_stack):
        rows = slice(e * depth, (e + 1) * depth)
        qst_ref[e // group, rows, (e % group) * t:(e % group + 1) * t] = qt_ref[rows, :]


def _softmax_out(acc):
    return acc[0:HEAD_V] / acc[HEAD_V:HEAD_V + 1]


DIFF_SLOPES = tuple(2.0 ** (-8.0 * (h + 1) / N_HEADS) for h in range(N_HEADS))


def _diff_tables(t):
    sl = np.arange(t)[:, None]
    tl = np.arange(t)[None, :]
    allowed = (sl // CHUNK) <= (tl // CHUNK)
    past = [LOG2E * slope * np.broadcast_to(sl, (t, t)) for slope in DIFF_SLOPES]
    diag = [np.where(allowed, LOG2E * slope * (tl - np.abs(tl - sl)), NEG) for slope in DIFF_SLOPES]
    return np.stack(past + diag).astype(np.float32)


SM_GROUP = 4
N_DIFF_COPIES = 2 * N_HEADS
N_SM_COPIES = N_DIFF_COPIES + N_HEADS


def _softmax_mixers_kernel(lam_init, dqt_ref, dk_ref, dvt_ref, fqt_ref, fk_ref, fvt_ref, lf_ref,
                           dbias_ref, fmask_ref, tril_ref, lp_ref, og_ref, a_ref, c_ref,
                           dqst_ref, fqst_ref, u_ref, umax_ref, m_ref, acc_ref, fneg_ref):
    t = dqt_ref.shape[1]
    seq = dk_ref.shape[0]
    i = pl.program_id(1)
    n_diff_groups = N_DIFF_COPIES // SM_GROUP

    @pl.when(i == 0)
    def _():
        tril = tril_ref[...]
        for blk in range(seq // LANES):
            rows = slice(blk * LANES, (blk + 1) * LANES)
            x = jnp.concatenate(
                [jnp.broadcast_to(lf_ref[rows, FOX_GATE_COL + h:FOX_GATE_COL + h + 1], (LANES, LANES))
                 for h in range(N_HEADS)], axis=1)
            hi, mid, lo = _split3(x)
            f = _dot(tril, hi) + _dot(tril, mid) + _dot(tril, lo)
            for h in range(N_HEADS):
                fneg_ref[h, rows, :] = f[:, h * LANES:(h + 1) * LANES]
        for h in range(N_HEADS):
            carry = jnp.zeros((1, LANES), F32)
            for blk in range(seq // LANES):
                rows = slice(blk * LANES, (blk + 1) * LANES)
                f = fneg_ref[h, rows, :] + carry
                fneg_ref[h, rows, :] = f * (-LOG2E)
                carry = f[LANES - 1:LANES, :]

    _stack_queries(dqt_ref, dqst_ref, N_DIFF_COPIES, SM_GROUP, i == 0)
    _stack_queries(fqt_ref, fqst_ref, N_HEADS, SM_GROUP, i == 0)

    def keys(j):
        return pl.ds(pl.multiple_of(j * t, t), t)

    def scores(j, p):
        if p < n_diff_groups:
            first = jnp.where(j == i, N_HEADS, 0) + p * (SM_GROUP // 2)
            bias = []
            for h in range(SM_GROUP // 2):
                bias += [dbias_ref[first + h]] * 2
            return _dot(dk_ref[keys(j), :], dqst_ref[p]) + jnp.concatenate(bias, axis=1)
        mask = fmask_ref[jnp.where(j == i, 1, 0)]
        bias = [jnp.concatenate([fneg_ref[h, keys(j), :]] * (t // LANES), axis=1) + mask
                for h in range(N_HEADS)]
        return _dot(fk_ref[keys(j), :], fqst_ref[0]) + jnp.concatenate(bias, axis=1)

    def shift(j, e):
        if e < N_DIFF_COPIES:
            return (LOG2E * DIFF_SLOPES[e // 2]) * ((j - i) * t).astype(F32)
        return 0.0

    def vt(j, e):
        if e < N_DIFF_COPIES:
            return dvt_ref[(e // 2) * V_SLAB:(e // 2 + 1) * V_SLAB, keys(j)]
        h = e - N_DIFF_COPIES
        return fvt_ref[h * V_SLAB:(h + 1) * V_SLAB, keys(j)]

    _attn_pipeline(i + 1, N_SM_COPIES, SM_GROUP, scores, shift, vt, u_ref, umax_ref, m_ref, acc_ref)

    lp = lp_ref[...]
    lam = (jnp.exp(jnp.sum(lp[0:1] * lp[1:2], keepdims=True))
           - jnp.exp(jnp.sum(lp[2:3] * lp[3:4], keepdims=True)) + lam_init)
    for h in range(N_HEADS):
        rows = slice(h * HEAD_V, (h + 1) * HEAD_V)
        a = _softmax_out(acc_ref[2 * h]) - lam * _softmax_out(acc_ref[2 * h + 1])
        ms = jnp.sum(a * a, axis=0, keepdims=True) * (1.0 / HEAD_V)
        y = a * lax.rsqrt(ms + NORM_EPS) * og_ref[...] * (1.0 - lam_init)
        a_ref[rows, :] = y.astype(BF16)
        c_ref[rows, :] = _softmax_out(acc_ref[N_DIFF_COPIES + h]).astype(BF16)


def _softmax_mixers(dqt, dk, dvt, fqt, fk, fvt, lf, layer, lp, og, lam_init, batch, seq):
    t = min(ATTN_TILE, seq)
    nq = seq // t
    n = batch * seq
    dbias = jnp.asarray(_diff_tables(t))
    tril = jnp.asarray(np.tril(np.ones((LANES, LANES), np.float32)), BF16)
    pos = np.arange(t)
    causal = np.where(pos[:, None] <= pos[None, :], 0.0, NEG)
    fmask = jnp.asarray(np.stack([np.zeros((t, t)), causal]).astype(np.float32))
    n_groups = N_SM_COPIES // SM_GROUP
    q_blk = pl.BlockSpec((QK_WIDTH, t), lambda b, i: (0, b * nq + i))
    k_blk = pl.BlockSpec((seq, QK_WIDTH), lambda b, i: (b, 0))
    v_blk = pl.BlockSpec((VT_ROWS, seq), lambda b, i: (0, b))
    out_blk = pl.BlockSpec((DIFF_WIDTH, t), lambda b, i: (0, b * nq + i))

    def const(x):
        return pl.BlockSpec(x.shape, lambda b, i: (0,) * x.ndim)

    return pl.pallas_call(
        functools.partial(_softmax_mixers_kernel, lam_init),
        grid=(batch, nq),
        in_specs=[q_blk, k_blk, v_blk, q_blk, k_blk, v_blk,
                  pl.BlockSpec((seq, LANES), lambda b, i: (b, 0)),
                  const(dbias), const(fmask), const(tril), _layer_spec(lp, layer), _layer_spec(og, layer)],
        out_specs=[out_blk, out_blk],
        out_shape=[jax.ShapeDtypeStruct((DIFF_WIDTH, n), BF16)] * 2,
        scratch_shapes=[
            pltpu.VMEM((N_DIFF_COPIES // SM_GROUP, QK_WIDTH, SM_GROUP * t), BF16),
            pltpu.VMEM((N_HEADS // SM_GROUP, QK_WIDTH, SM_GROUP * t), BF16),
            pltpu.VMEM((n_groups, SM_GROUP, t, t), F32),
            pltpu.VMEM((n_groups, 1, SM_GROUP * t), F32),
            pltpu.VMEM((N_SM_COPIES, 1, t), F32),
            pltpu.VMEM((N_SM_COPIES, V_SLAB, t), F32),
            pltpu.VMEM((N_HEADS, seq, LANES), F32),
        ],
        compiler_params=_cparams("parallel", "arbitrary"),
        name="softmax_mixers",
    )(dqt, dk, dvt, fqt, fk, fvt, lf, dbias, fmask, tril, lp, og)


GLA_LEVELS = (1, 2, 4, 8, 16, 32)
N_LEVELS = len(GLA_LEVELS)


def _gla_tables():
    c = CHUNK
    idx = np.arange(c)
    pair = np.zeros((N_LEVELS + 1, c, c), np.float32)
    for l, m in enumerate(GLA_LEVELS):
        upper = idx % (2 * m) >= m
        same = (idx[:, None] // (2 * m)) == (idx[None, :] // (2 * m))
        pair[l] = same & upper[:, None] & (~upper)[None, :]
    pair[N_LEVELS] = np.eye(c)
    pair = np.tile(pair, (1, N_HEADS, 1))
    tril = np.tril(np.ones((c, c), np.float32))
    return tril, pair


def _block_ref(b, m):
    c, w = b.shape
    if 2 * m >= 8:
        b3 = b.reshape(c // (2 * m), 2 * m, w)
        return jnp.broadcast_to(b3[:, m - 1:m, :], b3.shape).reshape(c, w)
    b3 = b.reshape(c // 8, 8, w)
    sub = lax.broadcasted_iota(jnp.int32, (1, 8, 1), 1)
    ref = jnp.broadcast_to(b3[:, m - 1:m, :], b3.shape)
    for start in range(2 * m, 8, 2 * m):
        pick = jnp.broadcast_to(b3[:, start + m - 1:start + m, :], b3.shape)
        ref = jnp.where(sub >= start, pick, ref)
    return ref.reshape(c, w)


def _gla_kernel(q_ref, k_ref, v_ref, r_ref, a_ref, tril_ref, pair_ref, og_ref, o_ref, state_ref):
    c = CHUNK
    hk = N_HEADS * GLA_K_DIM
    seqs = range(q_ref.shape[0])

    @pl.when(pl.program_id(1) == 0)
    def _():
        state_ref[...] = jnp.zeros_like(state_ref)

    col = lax.broadcasted_iota(jnp.int32, (1, hk), 1)
    head_cols = [col // GLA_K_DIM == h for h in range(N_HEADS)]

    def head_only(x, h):
        return jnp.where(head_cols[h], x, jnp.zeros_like(x))

    def stack_heads(x):
        x = x.astype(BF16)
        return jnp.concatenate([head_only(x, h) for h in range(N_HEADS)], axis=0)

    def chunk(ci, carry):
        r0 = pl.multiple_of(ci * c, c)
        rows = pl.ds(r0, c)
        tril = tril_ref[...]
        q = [q_ref[s, rows, :] for s in seqs]
        k = [k_ref[s, rows, :] for s in seqs]
        v = [v_ref[s, rows, :] for s in seqs]
        b = []
        for s in seqs:
            a_hi, a_mid, a_lo = _split3(a_ref[s, rows, :])
            b.append(_dot(tril, a_hi) + _dot(tril, a_mid) + _dot(tril, a_lo))
        b_last = [b[s][c - 1:c, :] for s in seqs]

        attn = [pair_ref[N_LEVELS] * _dot_nt(stack_heads(q[s]), k[s].astype(BF16)) for s in seqs]
        for l, m in enumerate(GLA_LEVELS):
            for s in seqs:
                ref = _block_ref(b[s], m)
                ql = q[s] * jnp.exp(jnp.minimum(b[s] - ref, 0.0))
                kl = k[s] * jnp.exp(jnp.minimum(ref - b[s], 0.0))
                attn[s] = attn[s] + pair_ref[l] * _dot_nt(stack_heads(ql), kl.astype(BF16))

        for s in seqs:
            attn_s = attn[s].astype(BF16)
            state_t = state_ref[s]
            o_inter = _dot_nt(stack_heads(q[s] * jnp.exp(b[s])), state_t.astype(BF16))
            k_dec = (k[s] * jnp.exp(b_last[s] - b[s])).astype(BF16)
            upd = None
            for h in range(N_HEADS):
                hr = slice(h * c, (h + 1) * c)
                hv = slice(h * GLA_V_DIM, (h + 1) * GLA_V_DIM)
                o = o_inter[hr] + _dot(attn_s[hr], v[s][:, hv])
                y = _rms(o, og_ref[...])
                gate = r_ref[s, rows, hv]
                o_ref[s, rows, hv] = (y * (gate * jax.nn.sigmoid(gate))).astype(BF16)
                term = _dot_tn(v[s][:, hv], head_only(k_dec, h))
                upd = term if upd is None else upd + term
            state_ref[s] = state_t * jnp.exp(b_last[s]) + upd
        return carry

    lax.fori_loop(0, q_ref.shape[1] // c, chunk, 0)


def _gla(gq, gk, gv, gr, la, layer, og, batch, seq):
    tg = min(GLA_TILE, seq)
    ng = seq // tg
    together = GLA_SEQS if batch % GLA_SEQS == 0 else 1
    tril_np, pair_np = _gla_tables()
    tril = jnp.asarray(tril_np, BF16)
    pair = jnp.asarray(pair_np)
    hk = N_HEADS * GLA_K_DIM
    hv = N_HEADS * GLA_V_DIM

    def seq_major(x):
        return x.reshape(batch // together, together, seq, x.shape[-1])

    def blk(width):
        return pl.BlockSpec((None, together, tg, width), lambda b, i: (b, 0, i, 0))

    out = pl.pallas_call(
        _gla_kernel,
        grid=(batch // together, ng),
        in_specs=[blk(hk), blk(hk), blk(hv), blk(hv), blk(hk),
                  pl.BlockSpec(tril.shape, lambda b, i: (0, 0)),
                  pl.BlockSpec(pair.shape, lambda b, i: (0, 0, 0)),
                  _layer_spec(og, layer)],
        out_specs=blk(hv),
        out_shape=jax.ShapeDtypeStruct((batch // together, together, seq, hv), BF16),
        scratch_shapes=[pltpu.VMEM((together, GLA_V_DIM, hk), F32)],
        compiler_params=_cparams("parallel", "arbitrary"),
        name="gla",
    )(seq_major(gq), seq_major(gk), seq_major(gv), seq_major(gr), seq_major(la), tril, pair, og)
    return out.reshape(batch * seq, hv)


def _in_proj_weights(w_in):
    w_in = w_in.astype(BF16)

    def cols(lo, hi):
        return w_in[..., lo:hi]

    def zeros(width):
        return jnp.zeros(w_in.shape[:-1] + (width,), w_in.dtype)

    w = jnp.concatenate([cols(256, 512), cols(768, 2304), cols(2576, 2832), cols(2304, 2320),
                         cols(IN_WIDTH - N_HEADS, IN_WIDTH), zeros(LANES - GLA_GATE_RANK - N_HEADS)], axis=-1)

    def value_slabs(lo):
        v = cols(lo, lo + DIFF_WIDTH)
        v = v.reshape(v.shape[:-1] + (N_HEADS, HEAD_V))
        v = jnp.pad(v, [(0, 0)] * (v.ndim - 1) + [(0, V_SLAB - HEAD_V)])
        return v.reshape(v.shape[:-2] + (VT_ROWS,))

    wt = jnp.concatenate([cols(0, 256), value_slabs(512), cols(2320, 2576), value_slabs(2832)], axis=-1)
    assert w.shape[-1] == PROJ_WIDTH and wt.shape[-1] == PROJ_T_ROWS
    return w, jnp.swapaxes(wt, -1, -2)


def _ffn_weights(w13, w2):
    pad = D_FF_PAD - D_FF
    lead = [(0, 0)] * (w13.ndim - 2)
    wg = jnp.pad(w13[..., :D_FF].astype(BF16), lead + [(0, 0), (0, pad)])
    wu = jnp.pad(w13[..., D_FF:].astype(BF16), lead + [(0, 0), (0, pad)])
    return wg, wu, jnp.pad(w2.astype(BF16), lead + [(0, pad), (0, 0)])


def _same_group(width):
    g = np.arange(QK_WIDTH) // width
    return jnp.asarray((g[:, None] == g[None, :]).astype(np.float32), BF16)


def kernel(x, ffn1_norm, ffn1_w13, ffn1_w2, mix_norm, w_in, w_out, diff_q_norm, diff_k_norm,
           diff_lambda, diff_out_norm, gla_alpha_w2, gla_alpha_b, gla_out_norm, fox_q_norm,
           fox_k_norm, fox_f_bias, ffn2_norm, ffn2_w13, ffn2_w2):
    batch, seq, _ = x.shape
    x2d = x.reshape(batch * seq, D_MODEL)
    grp32 = _same_group(DIFF_QK_DIM)
    grp64 = _same_group(HEAD_V)
    vone_np = np.zeros((VT_ROWS, 1), np.float32)
    vone_np[HEAD_V::V_SLAB, 0] = 1.0
    vone = jnp.asarray(vone_np)

    ffn1 = (ffn1_norm[:, None, :],) + _ffn_weights(ffn1_w13, ffn1_w2)
    ffn2 = (ffn2_norm[:, None, :],) + _ffn_weights(ffn2_w13, ffn2_w2)
    w, wt = _in_proj_weights(w_in)
    wo = w_out.astype(BF16)
    groups32 = QK_WIDTH // DIFF_QK_DIM
    kg = jnp.stack([jnp.tile(diff_k_norm, (1, groups32)), jnp.tile(fox_k_norm, (1, N_HEADS))], axis=1)
    qgt = jnp.stack([jnp.tile(diff_q_norm, (1, groups32)) * (LOG2E * DIFF_QK_DIM ** -0.5),
                     jnp.tile(fox_q_norm, (1, N_HEADS)) * (LOG2E * HEAD_V ** -0.5)], axis=2)
    aw2 = jnp.pad(gla_alpha_w2, ((0, 0), (0, LANES - GLA_GATE_RANK), (0, 0)))
    ab = gla_alpha_b[:, None, :]
    fb = jnp.pad(fox_f_bias, ((0, 0), (FOX_GATE_COL, LANES - FOX_GATE_COL - N_HEADS)))[:, None, :]
    mix_gain = mix_norm[:, None, :]
    diff_og = diff_out_norm[:, :, None]
    gla_og = gla_out_norm[:, None, :]

    for i in range(DEPTH):
        x2d = _ffn(x2d, i, *ffn1)
        (dqt, dk, dvt, gq, gk, gv, gr, la, fqt, fk, fvt, lf) = _proj(
            x2d, i, mix_gain, w, wt, kg, qgt, grp32, grp64, aw2, ab, fb, vone)
        lam_init = 0.8 - 0.6 * math.exp(-0.3 * i)
        at, ct = _softmax_mixers(dqt, dk, dvt, fqt, fk, fvt, lf, i, diff_lambda, diff_og, lam_init, batch, seq)
        g = _gla(gq, gk, gv, gr, la, i, gla_og, batch, seq)
        x2d = _mix_ffn(x2d, i, at, g, ct, wo, *ffn2)
    return x2d.reshape(batch, seq, D_MODEL)
```

```python
import functools
import math

import numpy as np
import jax
import jax.numpy as jnp
from jax import lax
from jax.experimental import pallas as pl
from jax.experimental.pallas import tpu as pltpu

F32 = jnp.float32
BF16 = jnp.bfloat16

D_MODEL = 1024
D_FF = 2752
DEPTH = 4
CHUNK = 64
N_HEADS = 4
DIFF_QK_DIM = 32
HEAD_V = 64
GLA_K_DIM = 64
GLA_V_DIM = 128
GLA_GATE_RANK = 16
GLA_TAU = 16.0
NORM_EPS = 1e-6
IN_WIDTH = 3092
DIFF_WIDTH = N_HEADS * HEAD_V
GLA_WIDTH = N_HEADS * GLA_V_DIM

LANES = 128
FF_CHUNK = 256
TOKEN_TILE = 512
FFN_TILE = 1024
ATTN_TILE = 256
GLA_TILE = 256
GLA_SEQS = 8
NEG = -1e30
LOG2E = math.log2(math.e)
VMEM_LIMIT = 56 * 1024 * 1024

QK_WIDTH = 256
V_SLAB = 80
VT_ROWS = N_HEADS * V_SLAB

C_DK, C_GQ, C_GK, C_GV, C_GR, C_FK, C_GATES = 0, 256, 512, 768, 1280, 1792, 2048
PROJ_WIDTH = 2176
FOX_GATE_COL = GLA_GATE_RANK
R_DQ, R_DV, R_FQ, R_FV = 0, 256, 576, 832
PROJ_T_ROWS = 1152

NT_DIMS = (((1,), (1,)), ((), ()))
TN_DIMS = (((0,), (0,)), ((), ()))


def _cparams(*sem):
    return pltpu.CompilerParams(dimension_semantics=sem, vmem_limit_bytes=VMEM_LIMIT)


def _layer_spec(stack, layer, **kwargs):
    zeros = (0,) * (stack.ndim - 1)
    return pl.BlockSpec((None,) + stack.shape[1:], lambda *_: (layer,) + zeros, **kwargs)


def _dot(a, b):
    return jnp.dot(a, b, preferred_element_type=F32)


def _dot_nt(a, b):
    return lax.dot_general(a, b, NT_DIMS, preferred_element_type=F32)


def _dot_tn(a, b):
    return lax.dot_general(a, b, TN_DIMS, preferred_element_type=F32)


def _split2(x):
    hi = x.astype(BF16)
    lo = (x - hi.astype(F32)).astype(BF16)
    return hi, lo


def _split3(x):
    hi = x.astype(BF16)
    r = x - hi.astype(F32)
    mid = r.astype(BF16)
    lo = (r - mid.astype(F32)).astype(BF16)
    return hi, mid, lo


def _log_sigmoid(z):
    return jnp.minimum(z, 0.0) - jnp.log(1.0 + jnp.exp(-jnp.abs(z)))


def _rms(x, g):
    ms = jnp.mean(x * x, axis=-1, keepdims=True)
    return x * lax.rsqrt(ms + NORM_EPS) * g


def _swiglu_residual(x, g_ref, wg_ref, wu_ref, w2_ref):
    h = _rms(x, g_ref[...]).astype(BF16)
    acc = None
    for lo in range(0, D_FF, FF_CHUNK):
        cols = slice(lo, min(lo + FF_CHUNK, D_FF))
        gate = _dot(h, wg_ref[:, cols])
        up = _dot(h, wu_ref[:, cols])
        act = (gate * jax.nn.sigmoid(gate) * up).astype(BF16)
        term = _dot(act, w2_ref[cols, :])
        acc = term if acc is None else acc + term
    return x + 0.5 * acc


def _ffn_kernel(x_ref, g_ref, wg_ref, wu_ref, w2_ref, o_ref):
    o_ref[...] = _swiglu_residual(x_ref[...], g_ref, wg_ref, wu_ref, w2_ref)


def _mix_ffn_kernel(x_ref, at_ref, gl_ref, ct_ref, wo_ref, g_ref, wg_ref, wu_ref, w2_ref, o_ref):
    mix = _dot_tn(at_ref[...], wo_ref[0:DIFF_WIDTH, :])
    mix = mix + _dot(gl_ref[...], wo_ref[DIFF_WIDTH:DIFF_WIDTH + GLA_WIDTH, :])
    mix = mix + _dot_tn(ct_ref[...], wo_ref[DIFF_WIDTH + GLA_WIDTH:, :])
    o_ref[...] = _swiglu_residual(x_ref[...] + mix, g_ref, wg_ref, wu_ref, w2_ref)


ONCE = dict(pipeline_mode=pl.Buffered(1))


def _ffn_specs(layer, gain, wg, wu, w2):
    return [_layer_spec(gain, layer), _layer_spec(wg, layer, **ONCE), _layer_spec(wu, layer, **ONCE),
            _layer_spec(w2, layer, **ONCE)]


def _ffn(x2d, layer, gain, wg, wu, w2):
    n = x2d.shape[0]
    tm = min(FFN_TILE, n)
    row = lambda i: (i, 0)
    return pl.pallas_call(
        _ffn_kernel,
        grid=(n // tm,),
        in_specs=[pl.BlockSpec((tm, D_MODEL), row)] + _ffn_specs(layer, gain, wg, wu, w2),
        out_specs=pl.BlockSpec((tm, D_MODEL), row),
        out_shape=jax.ShapeDtypeStruct((n, D_MODEL), F32),
        compiler_params=_cparams("parallel"),
        name="ffn",
    )(x2d, gain, wg, wu, w2)


def _mix_ffn(x2d, layer, at, gl, ct, wo, gain, wg, wu, w2):
    n = x2d.shape[0]
    tm = min(FFN_TILE, n)
    row = lambda i: (i, 0)
    colblk = lambda i: (0, i)
    return pl.pallas_call(
        _mix_ffn_kernel,
        grid=(n // tm,),
        in_specs=[pl.BlockSpec((tm, D_MODEL), row), pl.BlockSpec((DIFF_WIDTH, tm), colblk),
                  pl.BlockSpec((tm, GLA_WIDTH), row), pl.BlockSpec((DIFF_WIDTH, tm), colblk),
                  _layer_spec(wo, layer, **ONCE)] + _ffn_specs(layer, gain, wg, wu, w2),
        out_specs=pl.BlockSpec((tm, D_MODEL), row),
        out_shape=jax.ShapeDtypeStruct((n, D_MODEL), F32),
        compiler_params=_cparams("parallel"),
        name="mix_ffn",
    )(x2d, at, gl, ct, wo, gain, wg, wu, w2)


def _group_norm_lanes(t, grp, inv_d, gain):
    hi, lo = _split2(t * t)
    ms = (_dot(hi, grp) + _dot(lo, grp)) * inv_d
    return t * lax.rsqrt(ms + NORM_EPS) * gain


def _group_norm_rows(t, d, gain):
    rows, cols = t.shape
    t3 = t.reshape(rows // d, d, cols)
    ms = jnp.sum(t3 * t3, axis=1, keepdims=True) * (1.0 / d)
    return (t3 * lax.rsqrt(ms + NORM_EPS)).reshape(rows, cols) * gain


def _proj_kernel(x_ref, g_ref, w_ref, wt_ref, kg_ref, qgt_ref, grp32_ref, grp64_ref, aw2_ref,
                 ab_ref, fb_ref, vone_ref,
                 dqt_ref, dk_ref, dvt_ref, gq_ref, gk_ref, gv_ref, gr_ref, la_ref,
                 fqt_ref, fk_ref, fvt_ref, lf_ref):
    h = _rms(x_ref[...], g_ref[...]).astype(BF16)

    p_tok = _dot(h, w_ref[...])
    p_feat = _dot_nt(wt_ref[...], h)

    def proj(lo, width):
        return p_tok[:, lo:lo + width]

    def proj_t(lo, rows):
        return p_feat[lo:lo + rows, :]

    vone = vone_ref[...]
    dqt_ref[...] = _group_norm_rows(proj_t(R_DQ, QK_WIDTH), DIFF_QK_DIM, qgt_ref[:, 0:1]).astype(BF16)
    dk_ref[...] = _group_norm_lanes(proj(C_DK, QK_WIDTH), grp32_ref[...], 1.0 / DIFF_QK_DIM,
                                    kg_ref[0:1]).astype(BF16)
    dvt_ref[...] = (proj_t(R_DV, VT_ROWS) + vone).astype(BF16)
    gq_ref[...] = proj(C_GQ, 256) * (GLA_K_DIM ** -0.5)
    gk_ref[...] = proj(C_GK, 256)
    gv_ref[...] = proj(C_GV, 512).astype(BF16)
    gr_ref[...] = proj(C_GR, 512)
    fqt_ref[...] = _group_norm_rows(proj_t(R_FQ, QK_WIDTH), HEAD_V, qgt_ref[:, 1:2]).astype(BF16)
    fk_ref[...] = _group_norm_lanes(proj(C_FK, QK_WIDTH), grp64_ref[...], 1.0 / HEAD_V,
                                    kg_ref[1:2]).astype(BF16)
    fvt_ref[...] = (proj_t(R_FV, VT_ROWS) + vone).astype(BF16)

    gates = proj(C_GATES, LANES)
    ga_hi, ga_lo = _split2(gates)
    w_hi, w_lo = _split2(aw2_ref[...])
    z = _dot(ga_hi, w_hi) + _dot(ga_hi, w_lo) + _dot(ga_lo, w_hi) + ab_ref[...]
    la_ref[...] = _log_sigmoid(z) * (1.0 / GLA_TAU)
    lf_ref[...] = _log_sigmoid(gates + fb_ref[...])


def _proj(x2d, layer, gain, w, wt, kg, qgt, grp32, grp64, aw2, ab, fb, vone):
    n = x2d.shape[0]
    tm = min(TOKEN_TILE, n)
    row = lambda i: (i, 0)
    colblk = lambda i: (0, i)
    const = lambda i: (0, 0)

    def full(a):
        return pl.BlockSpec(a.shape, const)

    def per_layer(a):
        return _layer_spec(a, layer)

    def tok(width, dt):
        return jax.ShapeDtypeStruct((n, width), dt), pl.BlockSpec((tm, width), row)

    def feat(rows, dt):
        return jax.ShapeDtypeStruct((rows, n), dt), pl.BlockSpec((rows, tm), colblk)

    outs = [feat(QK_WIDTH, BF16), tok(QK_WIDTH, BF16), feat(VT_ROWS, BF16), tok(256, F32), tok(256, F32),
            tok(512, BF16), tok(512, F32), tok(256, F32), feat(QK_WIDTH, BF16), tok(QK_WIDTH, BF16),
            feat(VT_ROWS, BF16), tok(LANES, F32)]
    return pl.pallas_call(
        _proj_kernel,
        grid=(n // tm,),
        in_specs=[pl.BlockSpec((tm, D_MODEL), row), per_layer(gain), per_layer(w), per_layer(wt),
                  per_layer(kg), per_layer(qgt), full(grp32), full(grp64), per_layer(aw2), per_layer(ab),
                  per_layer(fb), full(vone)],
        out_specs=[o[1] for o in outs],
        out_shape=[o[0] for o in outs],
        compiler_params=_cparams("parallel"),
        name="proj_in",
    )(x2d, gain, w, wt, kg, qgt, grp32, grp64, aw2, ab, fb, vone)


def _max_over_rows(u):
    rows = u.shape[0]
    while rows > 8:
        rows //= 2
        u = jnp.maximum(u[:rows], u[rows:])
    return jnp.max(u, axis=0, keepdims=True)


def _attn_pipeline(n_blocks, n_stack, group, scores_of, shift_of, vt_of, u_ref, umax_ref, m_ref, acc_ref):
    t = u_ref.shape[-1]
    n_groups = n_stack // group
    n_slots = u_ref.shape[0]
    assert n_slots == max(n_groups, 2)
    m_ref[...] = jnp.full_like(m_ref, NEG)
    acc_ref[...] = jnp.zeros_like(acc_ref)

    def slot(j, p):
        return p if n_slots == n_groups else (j * n_groups + p) % n_slots

    def issue(j, p, dst):
        u = scores_of(j, p)
        for g in range(group):
            u_ref[dst, g] = u[:, g * t:(g + 1) * t]
        umax_ref[dst] = _max_over_rows(u)

    issue(0, 0, slot(0, 0))

    def body(j, carry):
        j_next = jnp.minimum(j + 1, n_blocks - 1)
        for p in range(n_groups):
            if p + 1 < n_groups:
                issue(j, p + 1, slot(j, p + 1))
            else:
                issue(j_next, 0, slot(j + 1, 0))
            src = slot(j, p)
            for e in range(group * p, group * (p + 1)):
                lanes = slice((e % group) * t, (e % group + 1) * t)
                u = u_ref[src, e % group]
                c = shift_of(j, e)
                m_old = m_ref[e]
                m_new = jnp.maximum(m_old, umax_ref[src, :, lanes] + c)
                alpha = jnp.exp2(m_old - m_new)
                p_t = jnp.exp2(u - (m_new - c)).astype(BF16)
                acc_ref[e] = alpha * acc_ref[e] + _dot(vt_of(j, e), p_t)
                m_ref[e] = m_new
        return carry

    lax.fori_loop(0, n_blocks // 2, lambda jj, carry: body(2 * jj + 1, body(2 * jj, carry)), 0)
    lax.fori_loop(0, n_blocks % 2, lambda _, carry: body(n_blocks - 1, carry), 0)


def _stack_queries(qt_ref, qst_ref, n_stack, group, clear):
    depth, t = qt_ref.shape[0] // n_stack, qt_ref.shape[1]

    @pl.when(clear)
    def _():
        qst_ref[...] = jnp.zeros_like(qst_ref)

    for e in range(n_stack):
        rows = slice(e * depth, (e + 1) * depth)
        qst_ref[e // group, rows, (e % group) * t:(e % group + 1) * t] = qt_ref[rows, :]


def _softmax_out(acc):
    return acc[0:HEAD_V] / acc[HEAD_V:HEAD_V + 1]


DIFF_SLOPES = tuple(2.0 ** (-8.0 * (h + 1) / N_HEADS) for h in range(N_HEADS))


def _diff_tables(t):
    sl = np.arange(t)[:, None]
    tl = np.arange(t)[None, :]
    allowed = (sl // CHUNK) <= (tl // CHUNK)
    past = [LOG2E * slope * np.broadcast_to(sl, (t, t)) for slope in DIFF_SLOPES]
    diag = [np.where(allowed, LOG2E * slope * (tl - np.abs(tl - sl)), NEG) for slope in DIFF_SLOPES]
    return np.stack(past + diag).astype(np.float32)


SM_GROUP = 4
N_DIFF_COPIES = 2 * N_HEADS
N_SM_COPIES = N_DIFF_COPIES + N_HEADS


def _softmax_mixers_kernel(lam_init, dqt_ref, dk_ref, dvt_ref, fqt_ref, fk_ref, fvt_ref, lf_ref,
                           dbias_ref, fmask_ref, tril_ref, lp_ref, og_ref, a_ref, c_ref,
                           dqst_ref, fqst_ref, u_ref, umax_ref, m_ref, acc_ref, fneg_ref):
    t = dqt_ref.shape[1]
    seq = dk_ref.shape[0]
    i = pl.program_id(1)
    n_diff_groups = N_DIFF_COPIES // SM_GROUP

    @pl.when(i == 0)
    def _():
        tril = tril_ref[...]
        for blk in range(seq // LANES):
            rows = slice(blk * LANES, (blk + 1) * LANES)
            x = jnp.concatenate(
                [jnp.broadcast_to(lf_ref[rows, FOX_GATE_COL + h:FOX_GATE_COL + h + 1], (LANES, LANES))
                 for h in range(N_HEADS)], axis=1)
            hi, mid, lo = _split3(x)
            f = _dot(tril, hi) + _dot(tril, mid) + _dot(tril, lo)
            for h in range(N_HEADS):
                fneg_ref[h, rows, :] = f[:, h * LANES:(h + 1) * LANES]
        for h in range(N_HEADS):
            carry = jnp.zeros((1, LANES), F32)
            for blk in range(seq // LANES):
                rows = slice(blk * LANES, (blk + 1) * LANES)
                f = fneg_ref[h, rows, :] + carry
                fneg_ref[h, rows, :] = f * (-LOG2E)
                carry = f[LANES - 1:LANES, :]

    _stack_queries(dqt_ref, dqst_ref, N_DIFF_COPIES, SM_GROUP, i == 0)
    _stack_queries(fqt_ref, fqst_ref, N_HEADS, SM_GROUP, i == 0)

    def keys(j):
        return pl.ds(pl.multiple_of(j * t, t), t)

    def scores(j, p):
        if p < n_diff_groups:
            first = jnp.where(j == i, N_HEADS, 0) + p * (SM_GROUP // 2)
            bias = []
            for h in range(SM_GROUP // 2):
                bias += [dbias_ref[first + h]] * 2
            return _dot(dk_ref[keys(j), :], dqst_ref[p]) + jnp.concatenate(bias, axis=1)
        mask = fmask_ref[jnp.where(j == i, 1, 0)]
        bias = [jnp.concatenate([fneg_ref[h, keys(j), :]] * (t // LANES), axis=1) + mask
                for h in range(N_HEADS)]
        return _dot(fk_ref[keys(j), :], fqst_ref[0]) + jnp.concatenate(bias, axis=1)

    def shift(j, e):
        if e < N_DIFF_COPIES:
            return (LOG2E * DIFF_SLOPES[e // 2]) * ((j - i) * t).astype(F32)
        return 0.0

    def vt(j, e):
        if e < N_DIFF_COPIES:
            return dvt_ref[(e // 2) * V_SLAB:(e // 2 + 1) * V_SLAB, keys(j)]
        h = e - N_DIFF_COPIES
        return fvt_ref[h * V_SLAB:(h + 1) * V_SLAB, keys(j)]

    _attn_pipeline(i + 1, N_SM_COPIES, SM_GROUP, scores, shift, vt, u_ref, umax_ref, m_ref, acc_ref)

    lp = lp_ref[...]
    lam = (jnp.exp(jnp.sum(lp[0:1] * lp[1:2], keepdims=True))
           - jnp.exp(jnp.sum(lp[2:3] * lp[3:4], keepdims=True)) + lam_init)
    for h in range(N_HEADS):
        rows = slice(h * HEAD_V, (h + 1) * HEAD_V)
        a = _softmax_out(acc_ref[2 * h]) - lam * _softmax_out(acc_ref[2 * h + 1])
        ms = jnp.sum(a * a, axis=0, keepdims=True) * (1.0 / HEAD_V)
        y = a * lax.rsqrt(ms + NORM_EPS) * og_ref[...] * (1.0 - lam_init)
        a_ref[rows, :] = y.astype(BF16)
        c_ref[rows, :] = _softmax_out(acc_ref[N_DIFF_COPIES + h]).astype(BF16)


def _softmax_mixers(dqt, dk, dvt, fqt, fk, fvt, lf, layer, lp, og, lam_init, batch, seq):
    t = min(ATTN_TILE, seq)
    nq = seq // t
    n = batch * seq
    dbias = jnp.asarray(_diff_tables(t))
    tril = jnp.asarray(np.tril(np.ones((LANES, LANES), np.float32)), BF16)
    pos = np.arange(t)
    causal = np.where(pos[:, None] <= pos[None, :], 0.0, NEG)
    fmask = jnp.asarray(np.stack([np.zeros((t, t)), causal]).astype(np.float32))
    n_groups = N_SM_COPIES // SM_GROUP
    q_blk = pl.BlockSpec((QK_WIDTH, t), lambda b, i: (0, b * nq + i))
    k_blk = pl.BlockSpec((seq, QK_WIDTH), lambda b, i: (b, 0))
    v_blk = pl.BlockSpec((VT_ROWS, seq), lambda b, i: (0, b))
    out_blk = pl.BlockSpec((DIFF_WIDTH, t), lambda b, i: (0, b * nq + i))

    def const(x):
        return pl.BlockSpec(x.shape, lambda b, i: (0,) * x.ndim)

    return pl.pallas_call(
        functools.partial(_softmax_mixers_kernel, lam_init),
        grid=(batch, nq),
        in_specs=[q_blk, k_blk, v_blk, q_blk, k_blk, v_blk,
                  pl.BlockSpec((seq, LANES), lambda b, i: (b, 0)),
                  const(dbias), const(fmask), const(tril), _layer_spec(lp, layer), _layer_spec(og, layer)],
        out_specs=[out_blk, out_blk],
        out_shape=[jax.ShapeDtypeStruct((DIFF_WIDTH, n), BF16)] * 2,
        scratch_shapes=[
            pltpu.VMEM((N_DIFF_COPIES // SM_GROUP, QK_WIDTH, SM_GROUP * t), BF16),
            pltpu.VMEM((N_HEADS // SM_GROUP, QK_WIDTH, SM_GROUP * t), BF16),
            pltpu.VMEM((n_groups, SM_GROUP, t, t), F32),
            pltpu.VMEM((n_groups, 1, SM_GROUP * t), F32),
            pltpu.VMEM((N_SM_COPIES, 1, t), F32),
            pltpu.VMEM((N_SM_COPIES, V_SLAB, t), F32),
            pltpu.VMEM((N_HEADS, seq, LANES), F32),
        ],
        compiler_params=_cparams("parallel", "arbitrary"),
        name="softmax_mixers",
    )(dqt, dk, dvt, fqt, fk, fvt, lf, dbias, fmask, tril, lp, og)


GLA_LEVELS = (1, 2, 4, 8, 16, 32)
N_LEVELS = len(GLA_LEVELS)


def _gla_tables():
    c = CHUNK
    idx = np.arange(c)
    pair = np.zeros((N_LEVELS + 1, c, c), np.float32)
    for l, m in enumerate(GLA_LEVELS):
        upper = idx % (2 * m) >= m
        same = (idx[:, None] // (2 * m)) == (idx[None, :] // (2 * m))
        pair[l] = same & upper[:, None] & (~upper)[None, :]
    pair[N_LEVELS] = np.eye(c)
    pair = np.tile(pair, (1, N_HEADS, 1))
    tril = np.tril(np.ones((c, c), np.float32))
    return tril, pair


def _block_ref(b, m):
    c, w = b.shape
    if 2 * m >= 8:
        b3 = b.reshape(c // (2 * m), 2 * m, w)
        return jnp.broadcast_to(b3[:, m - 1:m, :], b3.shape).reshape(c, w)
    b3 = b.reshape(c // 8, 8, w)
    sub = lax.broadcasted_iota(jnp.int32, (1, 8, 1), 1)
    ref = jnp.broadcast_to(b3[:, m - 1:m, :], b3.shape)
    for start in range(2 * m, 8, 2 * m):
        pick = jnp.broadcast_to(b3[:, start + m - 1:start + m, :], b3.shape)
        ref = jnp.where(sub >= start, pick, ref)
    return ref.reshape(c, w)


def _gla_kernel(q_ref, k_ref, v_ref, r_ref, a_ref, tril_ref, pair_ref, og_ref, o_ref, state_ref):
    c = CHUNK
    hk = N_HEADS * GLA_K_DIM
    seqs = range(q_ref.shape[0])

    @pl.when(pl.program_id(1) == 0)
    def _():
        state_ref[...] = jnp.zeros_like(state_ref)

    col = lax.broadcasted_iota(jnp.int32, (1, hk), 1)
    head_cols = [col // GLA_K_DIM == h for h in range(N_HEADS)]

    def head_only(x, h):
        return jnp.where(head_cols[h], x, jnp.zeros_like(x))

    def stack_heads(x):
        x = x.astype(BF16)
        return jnp.concatenate([head_only(x, h) for h in range(N_HEADS)], axis=0)

    def chunk(ci, carry):
        r0 = pl.multiple_of(ci * c, c)
        rows = pl.ds(r0, c)
        tril = tril_ref[...]
        q = [q_ref[s, rows, :] for s in seqs]
        k = [k_ref[s, rows, :] for s in seqs]
        v = [v_ref[s, rows, :] for s in seqs]
        b = []
        for s in seqs:
            a_hi, a_mid, a_lo = _split3(a_ref[s, rows, :])
            b.append(_dot(tril, a_hi) + _dot(tril, a_mid) + _dot(tril, a_lo))
        b_last = [b[s][c - 1:c, :] for s in seqs]

        attn = [pair_ref[N_LEVELS] * _dot_nt(stack_heads(q[s]), k[s].astype(BF16)) for s in seqs]
        for l, m in enumerate(GLA_LEVELS):
            for s in seqs:
                ref = _block_ref(b[s], m)
                ql = q[s] * jnp.exp(jnp.minimum(b[s] - ref, 0.0))
                kl = k[s] * jnp.exp(jnp.minimum(ref - b[s], 0.0))
                attn[s] = attn[s] + pair_ref[l] * _dot_nt(stack_heads(ql), kl.astype(BF16))

        for s in seqs:
            attn_s = attn[s].astype(BF16)
            state_t = state_ref[s]
            o_inter = _dot_nt(stack_heads(q[s] * jnp.exp(b[s])), state_t.astype(BF16))
            k_dec = (k[s] * jnp.exp(b_last[s] - b[s])).astype(BF16)
            upd = None
            for h in range(N_HEADS):
                hr = slice(h * c, (h + 1) * c)
                hv = slice(h * GLA_V_DIM, (h + 1) * GLA_V_DIM)
                o = o_inter[hr] + _dot(attn_s[hr], v[s][:, hv])
                y = _rms(o, og_ref[...])
                gate = r_ref[s, rows, hv]
                o_ref[s, rows, hv] = (y * (gate * jax.nn.sigmoid(gate))).astype(BF16)
                term = _dot_tn(v[s][:, hv], head_only(k_dec, h))
                upd = term if upd is None else upd + term
            state_ref[s] = state_t * jnp.exp(b_last[s]) + upd
        return carry

    lax.fori_loop(0, q_ref.shape[1] // c, chunk, 0)


def _gla(gq, gk, gv, gr, la, layer, og, batch, seq):
    tg = min(GLA_TILE, seq)
    ng = seq // tg
    together = GLA_SEQS if batch % GLA_SEQS == 0 else 1
    tril_np, pair_np = _gla_tables()
    tril = jnp.asarray(tril_np, BF16)
    pair = jnp.asarray(pair_np)
    hk = N_HEADS * GLA_K_DIM
    hv = N_HEADS * GLA_V_DIM

    def seq_major(x):
        return x.reshape(batch // together, together, seq, x.shape[-1])

    def blk(width):
        return pl.BlockSpec((None, together, tg, width), lambda b, i: (b, 0, i, 0))

    out = pl.pallas_call(
        _gla_kernel,
        grid=(batch // together, ng),
        in_specs=[blk(hk), blk(hk), blk(hv), blk(hv), blk(hk),
                  pl.BlockSpec(tril.shape, lambda b, i: (0, 0)),
                  pl.BlockSpec(pair.shape, lambda b, i: (0, 0, 0)),
                  _layer_spec(og, layer)],
        out_specs=blk(hv),
        out_shape=jax.ShapeDtypeStruct((batch // together, together, seq, hv), BF16),
        scratch_shapes=[pltpu.VMEM((together, GLA_V_DIM, hk), F32)],
        compiler_params=_cparams("parallel", "arbitrary"),
        name="gla",
    )(seq_major(gq), seq_major(gk), seq_major(gv), seq_major(gr), seq_major(la), tril, pair, og)
    return out.reshape(batch * seq, hv)


def _in_proj_weights(w_in):
    w_in = w_in.astype(BF16)

    def cols(lo, hi):
        return w_in[..., lo:hi]

    def zeros(width):
        return jnp.zeros(w_in.shape[:-1] + (width,), w_in.dtype)

    w = jnp.concatenate([cols(256, 512), cols(768, 2304), cols(2576, 2832), cols(2304, 2320),
                         cols(IN_WIDTH - N_HEADS, IN_WIDTH), zeros(LANES - GLA_GATE_RANK - N_HEADS)], axis=-1)

    def value_slabs(lo):
        v = cols(lo, lo + DIFF_WIDTH)
        v = v.reshape(v.shape[:-1] + (N_HEADS, HEAD_V))
        v = jnp.pad(v, [(0, 0)] * (v.ndim - 1) + [(0, V_SLAB - HEAD_V)])
        return v.reshape(v.shape[:-2] + (VT_ROWS,))

    wt = jnp.concatenate([cols(0, 256), value_slabs(512), cols(2320, 2576), value_slabs(2832)], axis=-1)
    assert w.shape[-1] == PROJ_WIDTH and wt.shape[-1] == PROJ_T_ROWS
    return w, jnp.swapaxes(wt, -1, -2)


def _ffn_weights(w13, w2):
    return w13[..., :D_FF].astype(BF16), w13[..., D_FF:].astype(BF16), w2.astype(BF16)


def _same_group(width):
    g = np.arange(QK_WIDTH) // width
    return jnp.asarray((g[:, None] == g[None, :]).astype(np.float32), BF16)


def kernel(x, ffn1_norm, ffn1_w13, ffn1_w2, mix_norm, w_in, w_out, diff_q_norm, diff_k_norm,
           diff_lambda, diff_out_norm, gla_alpha_w2, gla_alpha_b, gla_out_norm, fox_q_norm,
           fox_k_norm, fox_f_bias, ffn2_norm, ffn2_w13, ffn2_w2):
    batch, seq, _ = x.shape
    x2d = x.reshape(batch * seq, D_MODEL)
    grp32 = _same_group(DIFF_QK_DIM)
    grp64 = _same_group(HEAD_V)
    vone_np = np.zeros((VT_ROWS, 1), np.float32)
    vone_np[HEAD_V::V_SLAB, 0] = 1.0
    vone = jnp.asarray(vone_np)

    ffn1 = (ffn1_norm[:, None, :],) + _ffn_weights(ffn1_w13, ffn1_w2)
    ffn2 = (ffn2_norm[:, None, :],) + _ffn_weights(ffn2_w13, ffn2_w2)
    w, wt = _in_proj_weights(w_in)
    wo = w_out.astype(BF16)
    groups32 = QK_WIDTH // DIFF_QK_DIM
    kg = jnp.stack([jnp.tile(diff_k_norm, (1, groups32)), jnp.tile(fox_k_norm, (1, N_HEADS))], axis=1)
    qgt = jnp.stack([jnp.tile(diff_q_norm, (1, groups32)) * (LOG2E * DIFF_QK_DIM ** -0.5),
                     jnp.tile(fox_q_norm, (1, N_HEADS)) * (LOG2E * HEAD_V ** -0.5)], axis=2)
    aw2 = jnp.pad(gla_alpha_w2, ((0, 0), (0, LANES - GLA_GATE_RANK), (0, 0)))
    ab = gla_alpha_b[:, None, :]
    fb = jnp.pad(fox_f_bias, ((0, 0), (FOX_GATE_COL, LANES - FOX_GATE_COL - N_HEADS)))[:, None, :]
    mix_gain = mix_norm[:, None, :]
    diff_og = diff_out_norm[:, :, None]
    gla_og = gla_out_norm[:, None, :]

    for i in range(DEPTH):
        x2d = _ffn(x2d, i, *ffn1)
        (dqt, dk, dvt, gq, gk, gv, gr, la, fqt, fk, fvt, lf) = _proj(
            x2d, i, mix_gain, w, wt, kg, qgt, grp32, grp64, aw2, ab, fb, vone)
        lam_init = 0.8 - 0.6 * math.exp(-0.3 * i)
        at, ct = _softmax_mixers(dqt, dk, dvt, fqt, fk, fvt, lf, i, diff_lambda, diff_og, lam_init, batch, seq)
        g = _gla(gq, gk, gv, gr, la, i, gla_og, batch, seq)
        x2d = _mix_ffn(x2d, i, at, g, ct, wo, *ffn2)
    return x2d.reshape(batch, seq, D_MODEL)
```

```python
import functools
import math

import numpy as np
import jax
import jax.numpy as jnp
from jax import lax
from jax.experimental import pallas as pl
from jax.experimental.pallas import tpu as pltpu

F32 = jnp.float32
BF16 = jnp.bfloat16

D_MODEL = 1024
D_FF = 2752
DEPTH = 4
CHUNK = 64
N_HEADS = 4
DIFF_QK_DIM = 32
HEAD_V = 64
GLA_K_DIM = 64
GLA_V_DIM = 128
GLA_GATE_RANK = 16
GLA_TAU = 16.0
NORM_EPS = 1e-6
IN_WIDTH = 3092
DIFF_WIDTH = N_HEADS * HEAD_V
GLA_WIDTH = N_HEADS * GLA_V_DIM

LANES = 128
FF_CHUNK = 256
TOKEN_TILE = 512
FFN_TILE = 1024
ATTN_TILE = 256
GLA_TILE = 256
GLA_SEQS = 8
NEG = -1e30
LOG2E = math.log2(math.e)
VMEM_LIMIT = 56 * 1024 * 1024

QK_WIDTH = 256
V_SLAB = 80
VT_ROWS = N_HEADS * V_SLAB

C_DK, C_GQ, C_GK, C_GV, C_GR, C_FK, C_GATES = 0, 256, 512, 768, 1280, 1792, 2048
PROJ_WIDTH = 2176
FOX_GATE_COL = GLA_GATE_RANK
R_DQ, R_DV, R_FQ, R_FV = 0, 256, 576, 832
PROJ_T_ROWS = 1152

NT_DIMS = (((1,), (1,)), ((), ()))
TN_DIMS = (((0,), (0,)), ((), ()))


def _cparams(*sem):
    return pltpu.CompilerParams(dimension_semantics=sem, vmem_limit_bytes=VMEM_LIMIT)


def _layer_spec(stack, layer, **kwargs):
    zeros = (0,) * (stack.ndim - 1)
    return pl.BlockSpec((None,) + stack.shape[1:], lambda *_: (layer,) + zeros, **kwargs)


def _dot(a, b):
    return jnp.dot(a, b, preferred_element_type=F32)


def _dot_nt(a, b):
    return lax.dot_general(a, b, NT_DIMS, preferred_element_type=F32)


def _dot_tn(a, b):
    return lax.dot_general(a, b, TN_DIMS, preferred_element_type=F32)


def _split2(x):
    hi = x.astype(BF16)
    lo = (x - hi.astype(F32)).astype(BF16)
    return hi, lo


def _split3(x):
    hi = x.astype(BF16)
    r = x - hi.astype(F32)
    mid = r.astype(BF16)
    lo = (r - mid.astype(F32)).astype(BF16)
    return hi, mid, lo


def _log_sigmoid(z):
    return jnp.minimum(z, 0.0) - jnp.log(1.0 + jnp.exp(-jnp.abs(z)))


def _rms(x, g):
    ms = jnp.mean(x * x, axis=-1, keepdims=True)
    return x * lax.rsqrt(ms + NORM_EPS) * g


def _swiglu_residual(x, g_ref, wg_ref, wu_ref, w2_ref):
    h = _rms(x, g_ref[...]).astype(BF16)
    acc = None
    for lo in range(0, D_FF, FF_CHUNK):
        cols = slice(lo, min(lo + FF_CHUNK, D_FF))
        gate = _dot(h, wg_ref[:, cols])
        up = _dot(h, wu_ref[:, cols])
        act = (gate * jax.nn.sigmoid(gate) * up).astype(BF16)
        term = _dot(act, w2_ref[cols, :])
        acc = term if acc is None else acc + term
    return x + 0.5 * acc


def _ffn_kernel(x_ref, g_ref, wg_ref, wu_ref, w2_ref, o_ref):
    o_ref[...] = _swiglu_residual(x_ref[...], g_ref, wg_ref, wu_ref, w2_ref)


def _mix_ffn_kernel(x_ref, at_ref, gl_ref, ct_ref, wo_ref, g_ref, wg_ref, wu_ref, w2_ref, o_ref):
    mix = _dot_tn(at_ref[...], wo_ref[0:DIFF_WIDTH, :])
    mix = mix + _dot(gl_ref[...], wo_ref[DIFF_WIDTH:DIFF_WIDTH + GLA_WIDTH, :])
    mix = mix + _dot_tn(ct_ref[...], wo_ref[DIFF_WIDTH + GLA_WIDTH:, :])
    o_ref[...] = _swiglu_residual(x_ref[...] + mix, g_ref, wg_ref, wu_ref, w2_ref)


ONCE = dict(pipeline_mode=pl.Buffered(1))


def _ffn_specs(layer, gain, wg, wu, w2):
    return [_layer_spec(gain, layer), _layer_spec(wg, layer, **ONCE), _layer_spec(wu, layer, **ONCE),
            _layer_spec(w2, layer, **ONCE)]


def _ffn(x2d, layer, gain, wg, wu, w2):
    n = x2d.shape[0]
    tm = min(FFN_TILE, n)
    row = lambda i: (i, 0)
    return pl.pallas_call(
        _ffn_kernel,
        grid=(n // tm,),
        in_specs=[pl.BlockSpec((tm, D_MODEL), row)] + _ffn_specs(layer, gain, wg, wu, w2),
        out_specs=pl.BlockSpec((tm, D_MODEL), row),
        out_shape=jax.ShapeDtypeStruct((n, D_MODEL), F32),
        compiler_params=_cparams("parallel"),
        name="ffn",
    )(x2d, gain, wg, wu, w2)


def _mix_ffn(x2d, layer, at, gl, ct, wo, gain, wg, wu, w2):
    n = x2d.shape[0]
    tm = min(FFN_TILE, n)
    row = lambda i: (i, 0)
    colblk = lambda i: (0, i)
    return pl.pallas_call(
        _mix_ffn_kernel,
        grid=(n // tm,),
        in_specs=[pl.BlockSpec((tm, D_MODEL), row), pl.BlockSpec((DIFF_WIDTH, tm), colblk),
                  pl.BlockSpec((tm, GLA_WIDTH), row), pl.BlockSpec((DIFF_WIDTH, tm), colblk),
                  _layer_spec(wo, layer, **ONCE)] + _ffn_specs(layer, gain, wg, wu, w2),
        out_specs=pl.BlockSpec((tm, D_MODEL), row),
        out_shape=jax.ShapeDtypeStruct((n, D_MODEL), F32),
        compiler_params=_cparams("parallel"),
        name="mix_ffn",
    )(x2d, at, gl, ct, wo, gain, wg, wu, w2)


def _group_norm_lanes(t, grp, inv_d, gain):
    hi, lo = _split2(t * t)
    ms = (_dot(hi, grp) + _dot(lo, grp)) * inv_d
    return t * lax.rsqrt(ms + NORM_EPS) * gain


def _group_norm_rows(t, d, gain):
    rows, cols = t.shape
    t3 = t.reshape(rows // d, d, cols)
    ms = jnp.sum(t3 * t3, axis=1, keepdims=True) * (1.0 / d)
    return (t3 * lax.rsqrt(ms + NORM_EPS)).reshape(rows, cols) * gain


def _proj_kernel(x_ref, g_ref, w_ref, wt_ref, kg_ref, qgt_ref, grp32_ref, grp64_ref, aw2_ref,
                 ab_ref, fb_ref, vone_ref,
                 dqt_ref, dk_ref, dvt_ref, gq_ref, gk_ref, gv_ref, gr_ref, la_ref,
                 fqt_ref, fk_ref, fvt_ref, lf_ref):
    h = _rms(x_ref[...], g_ref[...]).astype(BF16)

    p_tok = _dot(h, w_ref[...])
    p_feat = _dot_nt(wt_ref[...], h)

    def proj(lo, width):
        return p_tok[:, lo:lo + width]

    def proj_t(lo, rows):
        return p_feat[lo:lo + rows, :]

    vone = vone_ref[...]
    dqt_ref[...] = _group_norm_rows(proj_t(R_DQ, QK_WIDTH), DIFF_QK_DIM, qgt_ref[:, 0:1]).astype(BF16)
    dk_ref[...] = _group_norm_lanes(proj(C_DK, QK_WIDTH), grp32_ref[...], 1.0 / DIFF_QK_DIM,
                                    kg_ref[0:1]).astype(BF16)
    dvt_ref[...] = (proj_t(R_DV, VT_ROWS) + vone).astype(BF16)
    gq_ref[...] = proj(C_GQ, 256) * (GLA_K_DIM ** -0.5)
    gk_ref[...] = proj(C_GK, 256)
    gv_ref[...] = proj(C_GV, 512).astype(BF16)
    gr_ref[...] = proj(C_GR, 512)
    fqt_ref[...] = _group_norm_rows(proj_t(R_FQ, QK_WIDTH), HEAD_V, qgt_ref[:, 1:2]).astype(BF16)
    fk_ref[...] = _group_norm_lanes(proj(C_FK, QK_WIDTH), grp64_ref[...], 1.0 / HEAD_V,
                                    kg_ref[1:2]).astype(BF16)
    fvt_ref[...] = (proj_t(R_FV, VT_ROWS) + vone).astype(BF16)

    gates = proj(C_GATES, LANES)
    ga_hi, ga_lo = _split2(gates)
    w_hi, w_lo = _split2(aw2_ref[...])
    z = _dot(ga_hi, w_hi) + _dot(ga_hi, w_lo) + _dot(ga_lo, w_hi) + ab_ref[...]
    la_ref[...] = _log_sigmoid(z) * (1.0 / GLA_TAU)
    lf_ref[...] = _log_sigmoid(gates + fb_ref[...])


def _proj(x2d, layer, gain, w, wt, kg, qgt, grp32, grp64, aw2, ab, fb, vone):
    n = x2d.shape[0]
    tm = min(TOKEN_TILE, n)
    row = lambda i: (i, 0)
    colblk = lambda i: (0, i)
    const = lambda i: (0, 0)

    def full(a):
        return pl.BlockSpec(a.shape, const)

    def per_layer(a):
        return _layer_spec(a, layer)

    def tok(width, dt):
        return jax.ShapeDtypeStruct((n, width), dt), pl.BlockSpec((tm, width), row)

    def feat(rows, dt):
        return jax.ShapeDtypeStruct((rows, n), dt), pl.BlockSpec((rows, tm), colblk)

    outs = [feat(QK_WIDTH, BF16), tok(QK_WIDTH, BF16), feat(VT_ROWS, BF16), tok(256, F32), tok(256, F32),
            tok(512, BF16), tok(512, F32), tok(256, F32), feat(QK_WIDTH, BF16), tok(QK_WIDTH, BF16),
            feat(VT_ROWS, BF16), tok(LANES, F32)]
    return pl.pallas_call(
        _proj_kernel,
        grid=(n // tm,),
        in_specs=[pl.BlockSpec((tm, D_MODEL), row), per_layer(gain), per_layer(w), per_layer(wt),
                  per_layer(kg), per_layer(qgt), full(grp32), full(grp64), per_layer(aw2), per_layer(ab),
                  per_layer(fb), full(vone)],
        out_specs=[o[1] for o in outs],
        out_shape=[o[0] for o in outs],
        compiler_params=_cparams("parallel"),
        name="proj_in",
    )(x2d, gain, w, wt, kg, qgt, grp32, grp64, aw2, ab, fb, vone)


def _max_over_rows(u):
    rows = u.shape[0]
    while rows > 8:
        rows //= 2
        u = jnp.maximum(u[:rows], u[rows:])
    return jnp.max(u, axis=0, keepdims=True)


def _attn_pipeline(n_blocks, n_stack, group, scores_of, shift_of, vt_of, u_ref, umax_ref, m_ref, acc_ref):
    t = u_ref.shape[-1]
    n_groups = n_stack // group
    n_slots = u_ref.shape[0]
    assert n_slots == max(n_groups, 2)
    m_ref[...] = jnp.full_like(m_ref, NEG)
    acc_ref[...] = jnp.zeros_like(acc_ref)

    def slot(j, p):
        return p if n_slots == n_groups else (j * n_groups + p) % n_slots

    def issue(j, p, dst):
        u = scores_of(j, p)
        for g in range(group):
            u_ref[dst, g] = u[:, g * t:(g + 1) * t]
        umax_ref[dst] = _max_over_rows(u)

    issue(0, 0, slot(0, 0))

    def body(j, carry):
        j_next = jnp.minimum(j + 1, n_blocks - 1)
        for p in range(n_groups):
            if p + 1 < n_groups:
                issue(j, p + 1, slot(j, p + 1))
            else:
                issue(j_next, 0, slot(j + 1, 0))
            src = slot(j, p)
            for e in range(group * p, group * (p + 1)):
                lanes = slice((e % group) * t, (e % group + 1) * t)
                u = u_ref[src, e % group]
                c = shift_of(j, e)
                m_old = m_ref[e]
                m_new = jnp.maximum(m_old, umax_ref[src, :, lanes] + c)
                alpha = jnp.exp2(m_old - m_new)
                p_t = jnp.exp2(u - (m_new - c)).astype(BF16)
                acc_ref[e] = alpha * acc_ref[e] + _dot(vt_of(j, e), p_t)
                m_ref[e] = m_new
        return carry

    lax.fori_loop(0, n_blocks // 2, lambda jj, carry: body(2 * jj + 1, body(2 * jj, carry)), 0)
    lax.fori_loop(0, n_blocks % 2, lambda _, carry: body(n_blocks - 1, carry), 0)


def _stack_queries(qt_ref, qst_ref, n_stack, group, clear):
    depth, t = qt_ref.shape[0] // n_stack, qt_ref.shape[1]

    @pl.when(clear)
    def _():
        qst_ref[...] = jnp.zeros_like(qst_ref)

    for e in range(n_stack):
        rows = slice(e * depth, (e + 1) * depth)
        qst_ref[e // group, rows, (e % group) * t:(e % group + 1) * t] = qt_ref[rows, :]


def _softmax_out(acc):
    return acc[0:HEAD_V] / acc[HEAD_V:HEAD_V + 1]


DIFF_SLOPES = tuple(2.0 ** (-8.0 * (h + 1) / N_HEADS) for h in range(N_HEADS))


def _diff_tables(t):
    sl = np.arange(t)[:, None]
    tl = np.arange(t)[None, :]
    allowed = (sl // CHUNK) <= (tl // CHUNK)
    past = [LOG2E * slope * np.broadcast_to(sl, (t, t)) for slope in DIFF_SLOPES]
    diag = [np.where(allowed, LOG2E * slope * (tl - np.abs(tl - sl)), NEG) for slope in DIFF_SLOPES]
    return np.stack(past + diag).astype(np.float32)


SM_GROUP = 4
N_DIFF_COPIES = 2 * N_HEADS
N_SM_COPIES = N_DIFF_COPIES + N_HEADS


def _softmax_mixers_kernel(lam_init, dqt_ref, dk_ref, dvt_ref, fqt_ref, fk_ref, fvt_ref, lf_ref,
                           dbias_ref, fmask_ref, tril_ref, lp_ref, og_ref, a_ref, c_ref,
                           dqst_ref, fqst_ref, u_ref, umax_ref, m_ref, acc_ref, fneg_ref):
    t = dqt_ref.shape[1]
    seq = dk_ref.shape[0]
    i = pl.program_id(1)
    n_diff_groups = N_DIFF_COPIES // SM_GROUP

    @pl.when(i == 0)
    def _():
        tril = tril_ref[...]
        for blk in range(seq // LANES):
            rows = slice(blk * LANES, (blk + 1) * LANES)
            x = jnp.concatenate(
                [jnp.broadcast_to(lf_ref[rows, FOX_GATE_COL + h:FOX_GATE_COL + h + 1], (LANES, LANES))
                 for h in range(N_HEADS)], axis=1)
            hi, mid, lo = _split3(x)
            f = _dot(tril, hi) + _dot(tril, mid) + _dot(tril, lo)
            for h in range(N_HEADS):
                fneg_ref[h, rows, :] = f[:, h * LANES:(h + 1) * LANES]
        for h in range(N_HEADS):
            carry = jnp.zeros((1, LANES), F32)
            for blk in range(seq // LANES):
                rows = slice(blk * LANES, (blk + 1) * LANES)
                f = fneg_ref[h, rows, :] + carry
                fneg_ref[h, rows, :] = f * (-LOG2E)
                carry = f[LANES - 1:LANES, :]

    _stack_queries(dqt_ref, dqst_ref, N_DIFF_COPIES, SM_GROUP, i == 0)
    _stack_queries(fqt_ref, fqst_ref, N_HEADS, SM_GROUP, i == 0)

    def keys(j):
        return pl.ds(pl.multiple_of(j * t, t), t)

    def scores(j, p):
        if p < n_diff_groups:
            first = jnp.where(j == i, N_HEADS, 0) + p * (SM_GROUP // 2)
            bias = []
            for h in range(SM_GROUP // 2):
                bias += [dbias_ref[first + h]] * 2
            return _dot(dk_ref[keys(j), :], dqst_ref[p]) + jnp.concatenate(bias, axis=1)
        mask = fmask_ref[jnp.where(j == i, 1, 0)]
        bias = [jnp.concatenate([fneg_ref[h, keys(j), :]] * (t // LANES), axis=1) + mask
                for h in range(N_HEADS)]
        return _dot(fk_ref[keys(j), :], fqst_ref[0]) + jnp.concatenate(bias, axis=1)

    def shift(j, e):
        if e < N_DIFF_COPIES:
            return (LOG2E * DIFF_SLOPES[e // 2]) * ((j - i) * t).astype(F32)
        return 0.0

    def vt(j, e):
        if e < N_DIFF_COPIES:
            return dvt_ref[(e // 2) * V_SLAB:(e // 2 + 1) * V_SLAB, keys(j)]
        h = e - N_DIFF_COPIES
        return fvt_ref[h * V_SLAB:(h + 1) * V_SLAB, keys(j)]

    _attn_pipeline(i + 1, N_SM_COPIES, SM_GROUP, scores, shift, vt, u_ref, umax_ref, m_ref, acc_ref)

    lp = lp_ref[...]
    lam = (jnp.exp(jnp.sum(lp[0:1] * lp[1:2], keepdims=True))
           - jnp.exp(jnp.sum(lp[2:3] * lp[3:4], keepdims=True)) + lam_init)
    for h in range(N_HEADS):
        rows = slice(h * HEAD_V, (h + 1) * HEAD_V)
        a = _softmax_out(acc_ref[2 * h]) - lam * _softmax_out(acc_ref[2 * h + 1])
        ms = jnp.sum(a * a, axis=0, keepdims=True) * (1.0 / HEAD_V)
        y = a * lax.rsqrt(ms + NORM_EPS) * og_ref[...] * (1.0 - lam_init)
        a_ref[rows, :] = y.astype(BF16)
        c_ref[rows, :] = _softmax_out(acc_ref[N_DIFF_COPIES + h]).astype(BF16)


def _softmax_mixers(dqt, dk, dvt, fqt, fk, fvt, lf, layer, lp, og, lam_init, batch, seq):
    t = min(ATTN_TILE, seq)
    nq = seq // t
    n = batch * seq
    dbias = jnp.asarray(_diff_tables(t))
    tril = jnp.asarray(np.tril(np.ones((LANES, LANES), np.float32)), BF16)
    pos = np.arange(t)
    causal = np.where(pos[:, None] <= pos[None, :], 0.0, NEG)
    fmask = jnp.asarray(np.stack([np.zeros((t, t)), causal]).astype(np.float32))
    n_groups = N_SM_COPIES // SM_GROUP
    q_blk = pl.BlockSpec((QK_WIDTH, t), lambda b, i: (0, b * nq + i))
    k_blk = pl.BlockSpec((seq, QK_WIDTH), lambda b, i: (b, 0))
    v_blk = pl.BlockSpec((VT_ROWS, seq), lambda b, i: (0, b))
    out_blk = pl.BlockSpec((DIFF_WIDTH, t), lambda b, i: (0, b * nq + i))

    def const(x):
        return pl.BlockSpec(x.shape, lambda b, i: (0,) * x.ndim)

    return pl.pallas_call(
        functools.partial(_softmax_mixers_kernel, lam_init),
        grid=(batch, nq),
        in_specs=[q_blk, k_blk, v_blk, q_blk, k_blk, v_blk,
                  pl.BlockSpec((seq, LANES), lambda b, i: (b, 0)),
                  const(dbias), const(fmask), const(tril), _layer_spec(lp, layer), _layer_spec(og, layer)],
        out_specs=[out_blk, out_blk],
        out_shape=[jax.ShapeDtypeStruct((DIFF_WIDTH, n), BF16)] * 2,
        scratch_shapes=[
            pltpu.VMEM((N_DIFF_COPIES // SM_GROUP, QK_WIDTH, SM_GROUP * t), BF16),
            pltpu.VMEM((N_HEADS // SM_GROUP, QK_WIDTH, SM_GROUP * t), BF16),
            pltpu.VMEM((n_groups, SM_GROUP, t, t), F32),
            pltpu.VMEM((n_groups, 1, SM_GROUP * t), F32),
            pltpu.VMEM((N_SM_COPIES, 1, t), F32),
            pltpu.VMEM((N_SM_COPIES, V_SLAB, t), F32),
            pltpu.VMEM((N_HEADS, seq, LANES), F32),
        ],
        compiler_params=_cparams("parallel", "arbitrary"),
        name="softmax_mixers",
    )(dqt, dk, dvt, fqt, fk, fvt, lf, dbias, fmask, tril, lp, og)


GLA_LEVELS = (1, 2, 4, 8, 16, 32)
N_LEVELS = len(GLA_LEVELS)


def _gla_tables():
    c = CHUNK
    idx = np.arange(c)
    pair = np.zeros((N_LEVELS, c, c), np.float32)
    for l, m in enumerate(GLA_LEVELS):
        upper = idx % (2 * m) >= m
        same = (idx[:, None] // (2 * m)) == (idx[None, :] // (2 * m))
        pair[l] = same & upper[:, None] & (~upper)[None, :]
    pair = np.tile(pair, (1, N_HEADS, 1))
    tril = np.tril(np.ones((c, c), np.float32))
    return tril, pair


def _block_ref(b, m):
    c, w = b.shape
    if 2 * m >= 8:
        b3 = b.reshape(c // (2 * m), 2 * m, w)
        return jnp.broadcast_to(b3[:, m - 1:m, :], b3.shape).reshape(c, w)
    b3 = b.reshape(c // 8, 8, w)
    sub = lax.broadcasted_iota(jnp.int32, (1, 8, 1), 1)
    ref = jnp.broadcast_to(b3[:, m - 1:m, :], b3.shape)
    for start in range(2 * m, 8, 2 * m):
        pick = jnp.broadcast_to(b3[:, start + m - 1:start + m, :], b3.shape)
        ref = jnp.where(sub >= start, pick, ref)
    return ref.reshape(c, w)


def _gla_kernel(q_ref, k_ref, v_ref, r_ref, a_ref, tril_ref, pair_ref, og_ref, o_ref, state_ref):
    c = CHUNK
    hk = N_HEADS * GLA_K_DIM
    seqs = range(q_ref.shape[0])

    @pl.when(pl.program_id(1) == 0)
    def _():
        state_ref[...] = jnp.zeros_like(state_ref)

    col = lax.broadcasted_iota(jnp.int32, (1, hk), 1)
    head_cols = [col // GLA_K_DIM == h for h in range(N_HEADS)]

    def head_only(x, h):
        return jnp.where(head_cols[h], x, jnp.zeros_like(x))

    def stack_heads(x):
        x = x.astype(BF16)
        return jnp.concatenate([head_only(x, h) for h in range(N_HEADS)], axis=0)

    def chunk(ci, carry):
        r0 = pl.multiple_of(ci * c, c)
        rows = pl.ds(r0, c)
        tril = tril_ref[...]
        q = [q_ref[s, rows, :] for s in seqs]
        k = [k_ref[s, rows, :] for s in seqs]
        v = [v_ref[s, rows, :] for s in seqs]
        b = []
        for s in seqs:
            a_hi, a_mid, a_lo = _split3(a_ref[s, rows, :])
            b.append(_dot(tril, a_hi) + _dot(tril, a_mid) + _dot(tril, a_lo))
        b_last = [b[s][c - 1:c, :] for s in seqs]

        attn = [None for s in seqs]
        for l, m in enumerate(GLA_LEVELS):
            for s in seqs:
                ref = _block_ref(b[s], m)
                ql = q[s] * jnp.exp(jnp.minimum(b[s] - ref, 0.0))
                kl = k[s] * jnp.exp(jnp.minimum(ref - b[s], 0.0))
                term = pair_ref[l] * _dot_nt(stack_heads(ql), kl.astype(BF16))
                attn[s] = term if attn[s] is None else attn[s] + term

        for s in seqs:
            attn_s = attn[s].astype(BF16)
            state_t = state_ref[s]
            o_inter = _dot_nt(stack_heads(q[s] * jnp.exp(b[s])), state_t.astype(BF16))
            k_dec = (k[s] * jnp.exp(b_last[s] - b[s])).astype(BF16)
            qk = q[s] * k[s]
            upd = None
            for h in range(N_HEADS):
                hr = slice(h * c, (h + 1) * c)
                hv = slice(h * GLA_V_DIM, (h + 1) * GLA_V_DIM)
                own = jnp.sum(head_only(qk, h), axis=-1, keepdims=True)
                o = o_inter[hr] + _dot(attn_s[hr], v[s][:, hv]) + own * v[s][:, hv].astype(F32)
                y = _rms(o, og_ref[...])
                gate = r_ref[s, rows, hv]
                o_ref[s, rows, hv] = (y * (gate * jax.nn.sigmoid(gate))).astype(BF16)
                term = _dot_tn(v[s][:, hv], head_only(k_dec, h))
                upd = term if upd is None else upd + term
            state_ref[s] = state_t * jnp.exp(b_last[s]) + upd
        return carry

    lax.fori_loop(0, q_ref.shape[1] // c, chunk, 0)


def _gla(gq, gk, gv, gr, la, layer, og, batch, seq):
    tg = min(GLA_TILE, seq)
    ng = seq // tg
    together = GLA_SEQS if batch % GLA_SEQS == 0 else 1
    tril_np, pair_np = _gla_tables()
    tril = jnp.asarray(tril_np, BF16)
    pair = jnp.asarray(pair_np)
    hk = N_HEADS * GLA_K_DIM
    hv = N_HEADS * GLA_V_DIM

    def seq_major(x):
        return x.reshape(batch // together, together, seq, x.shape[-1])

    def blk(width):
        return pl.BlockSpec((None, together, tg, width), lambda b, i: (b, 0, i, 0))

    out = pl.pallas_call(
        _gla_kernel,
        grid=(batch // together, ng),
        in_specs=[blk(hk), blk(hk), blk(hv), blk(hv), blk(hk),
                  pl.BlockSpec(tril.shape, lambda b, i: (0, 0)),
                  pl.BlockSpec(pair.shape, lambda b, i: (0, 0, 0)),
                  _layer_spec(og, layer)],
        out_specs=blk(hv),
        out_shape=jax.ShapeDtypeStruct((batch // together, together, seq, hv), BF16),
        scratch_shapes=[pltpu.VMEM((together, GLA_V_DIM, hk), F32)],
        compiler_params=_cparams("parallel", "arbitrary"),
        name="gla",
    )(seq_major(gq), seq_major(gk), seq_major(gv), seq_major(gr), seq_major(la), tril, pair, og)
    return out.reshape(batch * seq, hv)


def _in_proj_weights(w_in):
    w_in = w_in.astype(BF16)

    def cols(lo, hi):
        return w_in[..., lo:hi]

    def zeros(width):
        return jnp.zeros(w_in.shape[:-1] + (width,), w_in.dtype)

    w = jnp.concatenate([cols(256, 512), cols(768, 2304), cols(2576, 2832), cols(2304, 2320),
                         cols(IN_WIDTH - N_HEADS, IN_WIDTH), zeros(LANES - GLA_GATE_RANK - N_HEADS)], axis=-1)

    def value_slabs(lo):
        v = cols(lo, lo + DIFF_WIDTH)
        v = v.reshape(v.shape[:-1] + (N_HEADS, HEAD_V))
        v = jnp.pad(v, [(0, 0)] * (v.ndim - 1) + [(0, V_SLAB - HEAD_V)])
        return v.reshape(v.shape[:-2] + (VT_ROWS,))

    wt = jnp.concatenate([cols(0, 256), value_slabs(512), cols(2320, 2576), value_slabs(2832)], axis=-1)
    assert w.shape[-1] == PROJ_WIDTH and wt.shape[-1] == PROJ_T_ROWS
    return w, jnp.swapaxes(wt, -1, -2)


def _ffn_weights(w13, w2):
    return w13[..., :D_FF].astype(BF16), w13[..., D_FF:].astype(BF16), w2.astype(BF16)


def _same_group(width):
    g = np.arange(QK_WIDTH) // width
    return jnp.asarray((g[:, None] == g[None, :]).astype(np.float32), BF16)


def kernel(x, ffn1_norm, ffn1_w13, ffn1_w2, mix_norm, w_in, w_out, diff_q_norm, diff_k_norm,
           diff_lambda, diff_out_norm, gla_alpha_w2, gla_alpha_b, gla_out_norm, fox_q_norm,
           fox_k_norm, fox_f_bias, ffn2_norm, ffn2_w13, ffn2_w2):
    batch, seq, _ = x.shape
    x2d = x.reshape(batch * seq, D_MODEL)
    grp32 = _same_group(DIFF_QK_DIM)
    grp64 = _same_group(HEAD_V)
    vone_np = np.zeros((VT_ROWS, 1), np.float32)
    vone_np[HEAD_V::V_SLAB, 0] = 1.0
    vone = jnp.asarray(vone_np)

    ffn1 = (ffn1_norm[:, None, :],) + _ffn_weights(ffn1_w13, ffn1_w2)
    ffn2 = (ffn2_norm[:, None, :],) + _ffn_weights(ffn2_w13, ffn2_w2)
    w, wt = _in_proj_weights(w_in)
    wo = w_out.astype(BF16)
    groups32 = QK_WIDTH // DIFF_QK_DIM
    kg = jnp.stack([jnp.tile(diff_k_norm, (1, groups32)), jnp.tile(fox_k_norm, (1, N_HEADS))], axis=1)
    qgt = jnp.stack([jnp.tile(diff_q_norm, (1, groups32)) * (LOG2E * DIFF_QK_DIM ** -0.5),
                     jnp.tile(fox_q_norm, (1, N_HEADS)) * (LOG2E * HEAD_V ** -0.5)], axis=2)
    aw2 = jnp.pad(gla_alpha_w2, ((0, 0), (0, LANES - GLA_GATE_RANK), (0, 0)))
    ab = gla_alpha_b[:, None, :]
    fb = jnp.pad(fox_f_bias, ((0, 0), (FOX_GATE_COL, LANES - FOX_GATE_COL - N_HEADS)))[:, None, :]
    mix_gain = mix_norm[:, None, :]
    diff_og = diff_out_norm[:, :, None]
    gla_og = gla_out_norm[:, None, :]

    for i in range(DEPTH):
        x2d = _ffn(x2d, i, *ffn1)
        (dqt, dk, dvt, gq, gk, gv, gr, la, fqt, fk, fvt, lf) = _proj(
            x2d, i, mix_gain, w, wt, kg, qgt, grp32, grp64, aw2, ab, fb, vone)
        lam_init = 0.8 - 0.6 * math.exp(-0.3 * i)
        at, ct = _softmax_mixers(dqt, dk, dvt, fqt, fk, fvt, lf, i, diff_lambda, diff_og, lam_init, batch, seq)
        g = _gla(gq, gk, gv, gr, la, i, gla_og, batch, seq)
        x2d = _mix_ffn(x2d, i, at, g, ct, wo, *ffn2)
    return x2d.reshape(batch, seq, D_MODEL)
```

```python
import functools
import math

import numpy as np
import jax
import jax.numpy as jnp
from jax import lax
from jax.experimental import pallas as pl
from jax.experimental.pallas import tpu as pltpu

F32 = jnp.float32
BF16 = jnp.bfloat16

D_MODEL = 1024
D_FF = 2752
DEPTH = 4
CHUNK = 64
N_HEADS = 4
DIFF_QK_DIM = 32
HEAD_V = 64
GLA_K_DIM = 64
GLA_V_DIM = 128
GLA_GATE_RANK = 16
GLA_TAU = 16.0
NORM_EPS = 1e-6
IN_WIDTH = 3092
DIFF_WIDTH = N_HEADS * HEAD_V
GLA_WIDTH = N_HEADS * GLA_V_DIM

LANES = 128
FF_CHUNK = 256
TOKEN_TILE = 512
FFN_TILE = 1024
ATTN_TILE = 256
GLA_TILE = 256
GLA_SEQS = 8
NEG = -1e30
LOG2E = math.log2(math.e)
VMEM_LIMIT = 56 * 1024 * 1024

QK_WIDTH = 256
V_SLAB = 80
VT_ROWS = N_HEADS * V_SLAB

C_DK, C_GQ, C_GK, C_GV, C_GR, C_FK, C_GATES = 0, 256, 512, 768, 1280, 1792, 2048
PROJ_WIDTH = 2176
FOX_GATE_COL = GLA_GATE_RANK
R_DQ, R_DV, R_FQ, R_FV = 0, 256, 576, 832
PROJ_T_ROWS = 1152

NT_DIMS = (((1,), (1,)), ((), ()))
TN_DIMS = (((0,), (0,)), ((), ()))


def _cparams(*sem):
    return pltpu.CompilerParams(dimension_semantics=sem, vmem_limit_bytes=VMEM_LIMIT)


def _layer_spec(stack, layer, **kwargs):
    zeros = (0,) * (stack.ndim - 1)
    return pl.BlockSpec((None,) + stack.shape[1:], lambda *_: (layer,) + zeros, **kwargs)


def _dot(a, b):
    return jnp.dot(a, b, preferred_element_type=F32)


def _dot_nt(a, b):
    return lax.dot_general(a, b, NT_DIMS, preferred_element_type=F32)


def _dot_tn(a, b):
    return lax.dot_general(a, b, TN_DIMS, preferred_element_type=F32)


def _split2(x):
    hi = x.astype(BF16)
    lo = (x - hi.astype(F32)).astype(BF16)
    return hi, lo


def _split3(x):
    hi = x.astype(BF16)
    r = x - hi.astype(F32)
    mid = r.astype(BF16)
    lo = (r - mid.astype(F32)).astype(BF16)
    return hi, mid, lo


def _log_sigmoid(z):
    return jnp.minimum(z, 0.0) - jnp.log(1.0 + jnp.exp(-jnp.abs(z)))


def _rms(x, g):
    ms = jnp.mean(x * x, axis=-1, keepdims=True)
    return x * lax.rsqrt(ms + NORM_EPS) * g


def _swiglu_residual(x, g_ref, wg_ref, wu_ref, w2_ref):
    h = _rms(x, g_ref[...]).astype(BF16)
    acc = None
    for lo in range(0, D_FF, FF_CHUNK):
        cols = slice(lo, min(lo + FF_CHUNK, D_FF))
        gate = _dot(h, wg_ref[:, cols])
        up = _dot(h, wu_ref[:, cols])
        act = (gate * jax.nn.sigmoid(gate) * up).astype(BF16)
        term = _dot(act, w2_ref[cols, :])
        acc = term if acc is None else acc + term
    return x + 0.5 * acc


def _ffn_kernel(x_ref, g_ref, wg_ref, wu_ref, w2_ref, o_ref):
    o_ref[...] = _swiglu_residual(x_ref[...], g_ref, wg_ref, wu_ref, w2_ref)


def _mix_ffn_kernel(x_ref, at_ref, gl_ref, ct_ref, wo_ref, g_ref, wg_ref, wu_ref, w2_ref, o_ref):
    mix = _dot_tn(at_ref[...], wo_ref[0:DIFF_WIDTH, :])
    mix = mix + _dot(gl_ref[...], wo_ref[DIFF_WIDTH:DIFF_WIDTH + GLA_WIDTH, :])
    mix = mix + _dot_tn(ct_ref[...], wo_ref[DIFF_WIDTH + GLA_WIDTH:, :])
    o_ref[...] = _swiglu_residual(x_ref[...] + mix, g_ref, wg_ref, wu_ref, w2_ref)


ONCE = dict(pipeline_mode=pl.Buffered(1))


def _ffn_specs(layer, gain, wg, wu, w2):
    return [_layer_spec(gain, layer), _layer_spec(wg, layer, **ONCE), _layer_spec(wu, layer, **ONCE),
            _layer_spec(w2, layer, **ONCE)]


def _ffn(x2d, layer, gain, wg, wu, w2):
    n = x2d.shape[0]
    tm = min(FFN_TILE, n)
    row = lambda i: (i, 0)
    return pl.pallas_call(
        _ffn_kernel,
        grid=(n // tm,),
        in_specs=[pl.BlockSpec((tm, D_MODEL), row)] + _ffn_specs(layer, gain, wg, wu, w2),
        out_specs=pl.BlockSpec((tm, D_MODEL), row),
        out_shape=jax.ShapeDtypeStruct((n, D_MODEL), F32),
        compiler_params=_cparams("parallel"),
        name="ffn",
    )(x2d, gain, wg, wu, w2)


def _mix_ffn(x2d, layer, at, gl, ct, wo, gain, wg, wu, w2):
    n = x2d.shape[0]
    tm = min(FFN_TILE, n)
    row = lambda i: (i, 0)
    colblk = lambda i: (0, i)
    return pl.pallas_call(
        _mix_ffn_kernel,
        grid=(n // tm,),
        in_specs=[pl.BlockSpec((tm, D_MODEL), row), pl.BlockSpec((DIFF_WIDTH, tm), colblk),
                  pl.BlockSpec((tm, GLA_WIDTH), row), pl.BlockSpec((DIFF_WIDTH, tm), colblk),
                  _layer_spec(wo, layer, **ONCE)] + _ffn_specs(layer, gain, wg, wu, w2),
        out_specs=pl.BlockSpec((tm, D_MODEL), row),
        out_shape=jax.ShapeDtypeStruct((n, D_MODEL), F32),
        compiler_params=_cparams("parallel"),
        name="mix_ffn",
    )(x2d, at, gl, ct, wo, gain, wg, wu, w2)


def _group_norm_lanes(t, grp, inv_d, gain):
    hi, lo = _split2(t * t)
    ms = (_dot(hi, grp) + _dot(lo, grp)) * inv_d
    return t * lax.rsqrt(ms + NORM_EPS) * gain


def _group_norm_rows(t, d, gain):
    rows, cols = t.shape
    t3 = t.reshape(rows // d, d, cols)
    ms = jnp.sum(t3 * t3, axis=1, keepdims=True) * (1.0 / d)
    return (t3 * lax.rsqrt(ms + NORM_EPS)).reshape(rows, cols) * gain


def _proj_kernel(x_ref, g_ref, w_ref, wt_ref, kg_ref, qgt_ref, grp32_ref, grp64_ref, aw2_ref,
                 ab_ref, fb_ref, vone_ref,
                 dqt_ref, dk_ref, dvt_ref, gq_ref, gk_ref, gv_ref, gr_ref, la_ref,
                 fqt_ref, fk_ref, fvt_ref, lf_ref):
    h = _rms(x_ref[...], g_ref[...]).astype(BF16)

    p_tok = _dot(h, w_ref[...])
    p_feat = _dot_nt(wt_ref[...], h)

    def proj(lo, width):
        return p_tok[:, lo:lo + width]

    def proj_t(lo, rows):
        return p_feat[lo:lo + rows, :]

    vone = vone_ref[...]
    dqt_ref[...] = _group_norm_rows(proj_t(R_DQ, QK_WIDTH), DIFF_QK_DIM, qgt_ref[:, 0:1]).astype(BF16)
    dk_ref[...] = _group_norm_lanes(proj(C_DK, QK_WIDTH), grp32_ref[...], 1.0 / DIFF_QK_DIM,
                                    kg_ref[0:1]).astype(BF16)
    dvt_ref[...] = (proj_t(R_DV, VT_ROWS) + vone).astype(BF16)
    gq_ref[...] = proj(C_GQ, 256) * (GLA_K_DIM ** -0.5)
    gk_ref[...] = proj(C_GK, 256)
    gv_ref[...] = proj(C_GV, 512).astype(BF16)
    gr_ref[...] = proj(C_GR, 512)
    fqt_ref[...] = _group_norm_rows(proj_t(R_FQ, QK_WIDTH), HEAD_V, qgt_ref[:, 1:2]).astype(BF16)
    fk_ref[...] = _group_norm_lanes(proj(C_FK, QK_WIDTH), grp64_ref[...], 1.0 / HEAD_V,
                                    kg_ref[1:2]).astype(BF16)
    fvt_ref[...] = (proj_t(R_FV, VT_ROWS) + vone).astype(BF16)

    gates = proj(C_GATES, LANES)
    ga_hi, ga_lo = _split2(gates)
    w_hi, w_lo = _split2(aw2_ref[...])
    z = _dot(ga_hi, w_hi) + _dot(ga_hi, w_lo) + _dot(ga_lo, w_hi) + ab_ref[...]
    la_ref[...] = _log_sigmoid(z) * (1.0 / GLA_TAU)
    lf_ref[...] = _log_sigmoid(gates + fb_ref[...])


def _proj(x2d, layer, gain, w, wt, kg, qgt, grp32, grp64, aw2, ab, fb, vone):
    n = x2d.shape[0]
    tm = min(TOKEN_TILE, n)
    row = lambda i: (i, 0)
    colblk = lambda i: (0, i)
    const = lambda i: (0, 0)

    def full(a):
        return pl.BlockSpec(a.shape, const)

    def per_layer(a):
        return _layer_spec(a, layer)

    def tok(width, dt):
        return jax.ShapeDtypeStruct((n, width), dt), pl.BlockSpec((tm, width), row)

    def feat(rows, dt):
        return jax.ShapeDtypeStruct((rows, n), dt), pl.BlockSpec((rows, tm), colblk)

    outs = [feat(QK_WIDTH, BF16), tok(QK_WIDTH, BF16), feat(VT_ROWS, BF16), tok(256, F32), tok(256, F32),
            tok(512, BF16), tok(512, F32), tok(256, F32), feat(QK_WIDTH, BF16), tok(QK_WIDTH, BF16),
            feat(VT_ROWS, BF16), tok(LANES, F32)]
    return pl.pallas_call(
        _proj_kernel,
        grid=(n // tm,),
        in_specs=[pl.BlockSpec((tm, D_MODEL), row), per_layer(gain), per_layer(w), per_layer(wt),
                  per_layer(kg), per_layer(qgt), full(grp32), full(grp64), per_layer(aw2), per_layer(ab),
                  per_layer(fb), full(vone)],
        out_specs=[o[1] for o in outs],
        out_shape=[o[0] for o in outs],
        compiler_params=_cparams("parallel"),
        name="proj_in",
    )(x2d, gain, w, wt, kg, qgt, grp32, grp64, aw2, ab, fb, vone)


def _max_over_rows(u):
    rows = u.shape[0]
    while rows > 8:
        rows //= 2
        u = jnp.maximum(u[:rows], u[rows:])
    return jnp.max(u, axis=0, keepdims=True)


def _attn_pipeline(n_blocks, n_stack, group, scores_of, shift_of, vt_of, u_ref, umax_ref, m_ref, acc_ref):
    t = u_ref.shape[-1]
    n_groups = n_stack // group
    n_slots = u_ref.shape[0]
    assert n_slots == max(n_groups, 2)
    m_ref[...] = jnp.full_like(m_ref, NEG)
    acc_ref[...] = jnp.zeros_like(acc_ref)

    def slot(j, p):
        return p if n_slots == n_groups else (j * n_groups + p) % n_slots

    def issue(j, p, dst):
        u = scores_of(j, p)
        for g in range(group):
            u_ref[dst, g] = u[:, g * t:(g + 1) * t]
        umax_ref[dst] = _max_over_rows(u)

    issue(0, 0, slot(0, 0))

    def body(j, carry):
        j_next = jnp.minimum(j + 1, n_blocks - 1)
        for p in range(n_groups):
            if p + 1 < n_groups:
                issue(j, p + 1, slot(j, p + 1))
            else:
                issue(j_next, 0, slot(j + 1, 0))
            src = slot(j, p)
            for e in range(group * p, group * (p + 1)):
                lanes = slice((e % group) * t, (e % group + 1) * t)
                u = u_ref[src, e % group]
                c = shift_of(j, e)
                m_old = m_ref[e]
                m_new = jnp.maximum(m_old, umax_ref[src, :, lanes] + c)
                alpha = jnp.exp2(m_old - m_new)
                p_t = jnp.exp2((u - (m_new - c)).astype(BF16))
                acc_ref[e] = alpha * acc_ref[e] + _dot(vt_of(j, e), p_t)
                m_ref[e] = m_new
        return carry

    lax.fori_loop(0, n_blocks // 2, lambda jj, carry: body(2 * jj + 1, body(2 * jj, carry)), 0)
    lax.fori_loop(0, n_blocks % 2, lambda _, carry: body(n_blocks - 1, carry), 0)


def _stack_queries(qt_ref, qst_ref, n_stack, group, clear):
    depth, t = qt_ref.shape[0] // n_stack, qt_ref.shape[1]

    @pl.when(clear)
    def _():
        qst_ref[...] = jnp.zeros_like(qst_ref)

    for e in range(n_stack):
        rows = slice(e * depth, (e + 1) * depth)
        qst_ref[e // group, rows, (e % group) * t:(e % group + 1) * t] = qt_ref[rows, :]


def _softmax_out(acc):
    return acc[0:HEAD_V] / acc[HEAD_V:HEAD_V + 1]


DIFF_SLOPES = tuple(2.0 ** (-8.0 * (h + 1) / N_HEADS) for h in range(N_HEADS))


def _diff_tables(t):
    sl = np.arange(t)[:, None]
    tl = np.arange(t)[None, :]
    allowed = (sl // CHUNK) <= (tl // CHUNK)
    past = [LOG2E * slope * np.broadcast_to(sl, (t, t)) for slope in DIFF_SLOPES]
    diag = [np.where(allowed, LOG2E * slope * (tl - np.abs(tl - sl)), NEG) for slope in DIFF_SLOPES]
    return np.stack(past + diag).astype(np.float32)


SM_GROUP = 4
N_DIFF_COPIES = 2 * N_HEADS
N_SM_COPIES = N_DIFF_COPIES + N_HEADS


def _softmax_mixers_kernel(lam_init, dqt_ref, dk_ref, dvt_ref, fqt_ref, fk_ref, fvt_ref, lf_ref,
                           dbias_ref, fmask_ref, tril_ref, lp_ref, og_ref, a_ref, c_ref,
                           dqst_ref, fqst_ref, u_ref, umax_ref, m_ref, acc_ref, fneg_ref):
    t = dqt_ref.shape[1]
    seq = dk_ref.shape[0]
    i = pl.program_id(1)
    n_diff_groups = N_DIFF_COPIES // SM_GROUP

    @pl.when(i == 0)
    def _():
        tril = tril_ref[...]
        for blk in range(seq // LANES):
            rows = slice(blk * LANES, (blk + 1) * LANES)
            x = jnp.concatenate(
                [jnp.broadcast_to(lf_ref[rows, FOX_GATE_COL + h:FOX_GATE_COL + h + 1], (LANES, LANES))
                 for h in range(N_HEADS)], axis=1)
            hi, mid, lo = _split3(x)
            f = _dot(tril, hi) + _dot(tril, mid) + _dot(tril, lo)
            for h in range(N_HEADS):
                fneg_ref[h, rows, :] = f[:, h * LANES:(h + 1) * LANES]
        for h in range(N_HEADS):
            carry = jnp.zeros((1, LANES), F32)
            for blk in range(seq // LANES):
                rows = slice(blk * LANES, (blk + 1) * LANES)
                f = fneg_ref[h, rows, :] + carry
                fneg_ref[h, rows, :] = f * (-LOG2E)
                carry = f[LANES - 1:LANES, :]

    _stack_queries(dqt_ref, dqst_ref, N_DIFF_COPIES, SM_GROUP, i == 0)
    _stack_queries(fqt_ref, fqst_ref, N_HEADS, SM_GROUP, i == 0)

    def keys(j):
        return pl.ds(pl.multiple_of(j * t, t), t)

    def scores(j, p):
        if p < n_diff_groups:
            first = jnp.where(j == i, N_HEADS, 0) + p * (SM_GROUP // 2)
            bias = []
            for h in range(SM_GROUP // 2):
                bias += [dbias_ref[first + h]] * 2
            return _dot(dk_ref[keys(j), :], dqst_ref[p]) + jnp.concatenate(bias, axis=1)
        mask = fmask_ref[jnp.where(j == i, 1, 0)]
        bias = [jnp.concatenate([fneg_ref[h, keys(j), :]] * (t // LANES), axis=1) + mask
                for h in range(N_HEADS)]
        return _dot(fk_ref[keys(j), :], fqst_ref[0]) + jnp.concatenate(bias, axis=1)

    def shift(j, e):
        if e < N_DIFF_COPIES:
            return (LOG2E * DIFF_SLOPES[e // 2]) * ((j - i) * t).astype(F32)
        return 0.0

    def vt(j, e):
        if e < N_DIFF_COPIES:
            return dvt_ref[(e // 2) * V_SLAB:(e // 2 + 1) * V_SLAB, keys(j)]
        h = e - N_DIFF_COPIES
        return fvt_ref[h * V_SLAB:(h + 1) * V_SLAB, keys(j)]

    _attn_pipeline(i + 1, N_SM_COPIES, SM_GROUP, scores, shift, vt, u_ref, umax_ref, m_ref, acc_ref)

    lp = lp_ref[...]
    lam = (jnp.exp(jnp.sum(lp[0:1] * lp[1:2], keepdims=True))
           - jnp.exp(jnp.sum(lp[2:3] * lp[3:4], keepdims=True)) + lam_init)
    for h in range(N_HEADS):
        rows = slice(h * HEAD_V, (h + 1) * HEAD_V)
        a = _softmax_out(acc_ref[2 * h]) - lam * _softmax_out(acc_ref[2 * h + 1])
        ms = jnp.sum(a * a, axis=0, keepdims=True) * (1.0 / HEAD_V)
        y = a * lax.rsqrt(ms + NORM_EPS) * og_ref[...] * (1.0 - lam_init)
        a_ref[rows, :] = y.astype(BF16)
        c_ref[rows, :] = _softmax_out(acc_ref[N_DIFF_COPIES + h]).astype(BF16)


def _softmax_mixers(dqt, dk, dvt, fqt, fk, fvt, lf, layer, lp, og, lam_init, batch, seq):
    t = min(ATTN_TILE, seq)
    nq = seq // t
    n = batch * seq
    dbias = jnp.asarray(_diff_tables(t))
    tril = jnp.asarray(np.tril(np.ones((LANES, LANES), np.float32)), BF16)
    pos = np.arange(t)
    causal = np.where(pos[:, None] <= pos[None, :], 0.0, NEG)
    fmask = jnp.asarray(np.stack([np.zeros((t, t)), causal]).astype(np.float32))
    n_groups = N_SM_COPIES // SM_GROUP
    q_blk = pl.BlockSpec((QK_WIDTH, t), lambda b, i: (0, b * nq + i))
    k_blk = pl.BlockSpec((seq, QK_WIDTH), lambda b, i: (b, 0))
    v_blk = pl.BlockSpec((VT_ROWS, seq), lambda b, i: (0, b))
    out_blk = pl.BlockSpec((DIFF_WIDTH, t), lambda b, i: (0, b * nq + i))

    def const(x):
        return pl.BlockSpec(x.shape, lambda b, i: (0,) * x.ndim)

    return pl.pallas_call(
        functools.partial(_softmax_mixers_kernel, lam_init),
        grid=(batch, nq),
        in_specs=[q_blk, k_blk, v_blk, q_blk, k_blk, v_blk,
                  pl.BlockSpec((seq, LANES), lambda b, i: (b, 0)),
                  const(dbias), const(fmask), const(tril), _layer_spec(lp, layer), _layer_spec(og, layer)],
        out_specs=[out_blk, out_blk],
        out_shape=[jax.ShapeDtypeStruct((DIFF_WIDTH, n), BF16)] * 2,
        scratch_shapes=[
            pltpu.VMEM((N_DIFF_COPIES // SM_GROUP, QK_WIDTH, SM_GROUP * t), BF16),
            pltpu.VMEM((N_HEADS // SM_GROUP, QK_WIDTH, SM_GROUP * t), BF16),
            pltpu.VMEM((n_groups, SM_GROUP, t, t), F32),
            pltpu.VMEM((n_groups, 1, SM_GROUP * t), F32),
            pltpu.VMEM((N_SM_COPIES, 1, t), F32),
            pltpu.VMEM((N_SM_COPIES, V_SLAB, t), F32),
            pltpu.VMEM((N_HEADS, seq, LANES), F32),
        ],
        compiler_params=_cparams("parallel", "arbitrary"),
        name="softmax_mixers",
    )(dqt, dk, dvt, fqt, fk, fvt, lf, dbias, fmask, tril, lp, og)


GLA_LEVELS = (1, 2, 4, 8, 16, 32)
N_LEVELS = len(GLA_LEVELS)


def _gla_tables():
    c = CHUNK
    idx = np.arange(c)
    pair = np.zeros((N_LEVELS, c, c), np.float32)
    for l, m in enumerate(GLA_LEVELS):
        upper = idx % (2 * m) >= m
        same = (idx[:, None] // (2 * m)) == (idx[None, :] // (2 * m))
        pair[l] = same & upper[:, None] & (~upper)[None, :]
    pair = np.tile(pair, (1, N_HEADS, 1))
    tril = np.tril(np.ones((c, c), np.float32))
    return tril, pair


def _block_ref(b, m):
    c, w = b.shape
    if 2 * m >= 8:
        b3 = b.reshape(c // (2 * m), 2 * m, w)
        return jnp.broadcast_to(b3[:, m - 1:m, :], b3.shape).reshape(c, w)
    b3 = b.reshape(c // 8, 8, w)
    sub = lax.broadcasted_iota(jnp.int32, (1, 8, 1), 1)
    ref = jnp.broadcast_to(b3[:, m - 1:m, :], b3.shape)
    for start in range(2 * m, 8, 2 * m):
        pick = jnp.broadcast_to(b3[:, start + m - 1:start + m, :], b3.shape)
        ref = jnp.where(sub >= start, pick, ref)
    return ref.reshape(c, w)


def _gla_kernel(q_ref, k_ref, v_ref, r_ref, a_ref, tril_ref, pair_ref, og_ref, o_ref, state_ref):
    c = CHUNK
    hk = N_HEADS * GLA_K_DIM
    seqs = range(q_ref.shape[0])

    @pl.when(pl.program_id(1) == 0)
    def _():
        state_ref[...] = jnp.zeros_like(state_ref)

    col = lax.broadcasted_iota(jnp.int32, (1, hk), 1)
    head_cols = [col // GLA_K_DIM == h for h in range(N_HEADS)]

    def head_only(x, h):
        return jnp.where(head_cols[h], x, jnp.zeros_like(x))

    def stack_heads(x):
        x = x.astype(BF16)
        return jnp.concatenate([head_only(x, h) for h in range(N_HEADS)], axis=0)

    def chunk(ci, carry):
        r0 = pl.multiple_of(ci * c, c)
        rows = pl.ds(r0, c)
        tril = tril_ref[...]
        q = [q_ref[s, rows, :] for s in seqs]
        k = [k_ref[s, rows, :] for s in seqs]
        v = [v_ref[s, rows, :] for s in seqs]
        b = []
        for s in seqs:
            a_hi, a_mid, a_lo = _split3(a_ref[s, rows, :])
            b.append(_dot(tril, a_hi) + _dot(tril, a_mid) + _dot(tril, a_lo))
        b_last = [b[s][c - 1:c, :] for s in seqs]

        attn = [None for s in seqs]
        for l, m in enumerate(GLA_LEVELS):
            for s in seqs:
                ref = _block_ref(b[s], m)
                ql = q[s] * jnp.exp(jnp.minimum(b[s] - ref, 0.0))
                kl = k[s] * jnp.exp(jnp.minimum(ref - b[s], 0.0))
                term = pair_ref[l] * _dot_nt(stack_heads(ql), kl.astype(BF16))
                attn[s] = term if attn[s] is None else attn[s] + term

        for s in seqs:
            attn_s = attn[s].astype(BF16)
            state_t = state_ref[s]
            o_inter = _dot_nt(stack_heads(q[s] * jnp.exp(b[s])), state_t.astype(BF16))
            k_dec = (k[s] * jnp.exp(b_last[s] - b[s])).astype(BF16)
            qk = q[s] * k[s]
            upd = None
            for h in range(N_HEADS):
                hr = slice(h * c, (h + 1) * c)
                hv = slice(h * GLA_V_DIM, (h + 1) * GLA_V_DIM)
                own = jnp.sum(head_only(qk, h), axis=-1, keepdims=True)
                o = o_inter[hr] + _dot(attn_s[hr], v[s][:, hv]) + own * v[s][:, hv].astype(F32)
                y = _rms(o, og_ref[...])
                gate = r_ref[s, rows, hv]
                o_ref[s, rows, hv] = (y * (gate * jax.nn.sigmoid(gate))).astype(BF16)
                term = _dot_tn(v[s][:, hv], head_only(k_dec, h))
                upd = term if upd is None else upd + term
            state_ref[s] = state_t * jnp.exp(b_last[s]) + upd
        return carry

    lax.fori_loop(0, q_ref.shape[1] // c, chunk, 0)


def _gla(gq, gk, gv, gr, la, layer, og, batch, seq):
    tg = min(GLA_TILE, seq)
    ng = seq // tg
    together = GLA_SEQS if batch % GLA_SEQS == 0 else 1
    tril_np, pair_np = _gla_tables()
    tril = jnp.asarray(tril_np, BF16)
    pair = jnp.asarray(pair_np)
    hk = N_HEADS * GLA_K_DIM
    hv = N_HEADS * GLA_V_DIM

    def seq_major(x):
        return x.reshape(batch // together, together, seq, x.shape[-1])

    def blk(width):
        return pl.BlockSpec((None, together, tg, width), lambda b, i: (b, 0, i, 0))

    out = pl.pallas_call(
        _gla_kernel,
        grid=(batch // together, ng),
        in_specs=[blk(hk), blk(hk), blk(hv), blk(hv), blk(hk),
                  pl.BlockSpec(tril.shape, lambda b, i: (0, 0)),
                  pl.BlockSpec(pair.shape, lambda b, i: (0, 0, 0)),
                  _layer_spec(og, layer)],
        out_specs=blk(hv),
        out_shape=jax.ShapeDtypeStruct((batch // together, together, seq, hv), BF16),
        scratch_shapes=[pltpu.VMEM((together, GLA_V_DIM, hk), F32)],
        compiler_params=_cparams("parallel", "arbitrary"),
        name="gla",
    )(seq_major(gq), seq_major(gk), seq_major(gv), seq_major(gr), seq_major(la), tril, pair, og)
    return out.reshape(batch * seq, hv)


def _in_proj_weights(w_in):
    w_in = w_in.astype(BF16)

    def cols(lo, hi):
        return w_in[..., lo:hi]

    def zeros(width):
        return jnp.zeros(w_in.shape[:-1] + (width,), w_in.dtype)

    w = jnp.concatenate([cols(256, 512), cols(768, 2304), cols(2576, 2832), cols(2304, 2320),
                         cols(IN_WIDTH - N_HEADS, IN_WIDTH), zeros(LANES - GLA_GATE_RANK - N_HEADS)], axis=-1)

    def value_slabs(lo):
        v = cols(lo, lo + DIFF_WIDTH)
        v = v.reshape(v.shape[:-1] + (N_HEADS, HEAD_V))
        v = jnp.pad(v, [(0, 0)] * (v.ndim - 1) + [(0, V_SLAB - HEAD_V)])
        return v.reshape(v.shape[:-2] + (VT_ROWS,))

    wt = jnp.concatenate([cols(0, 256), value_slabs(512), cols(2320, 2576), value_slabs(2832)], axis=-1)
    assert w.shape[-1] == PROJ_WIDTH and wt.shape[-1] == PROJ_T_ROWS
    return w, jnp.swapaxes(wt, -1, -2)


def _ffn_weights(w13, w2):
    return w13[..., :D_FF].astype(BF16), w13[..., D_FF:].astype(BF16), w2.astype(BF16)


def _same_group(width):
    g = np.arange(QK_WIDTH) // width
    return jnp.asarray((g[:, None] == g[None, :]).astype(np.float32), BF16)


def kernel(x, ffn1_norm, ffn1_w13, ffn1_w2, mix_norm, w_in, w_out, diff_q_norm, diff_k_norm,
           diff_lambda, diff_out_norm, gla_alpha_w2, gla_alpha_b, gla_out_norm, fox_q_norm,
           fox_k_norm, fox_f_bias, ffn2_norm, ffn2_w13, ffn2_w2):
    batch, seq, _ = x.shape
    x2d = x.reshape(batch * seq, D_MODEL)
    grp32 = _same_group(DIFF_QK_DIM)
    grp64 = _same_group(HEAD_V)
    vone_np = np.zeros((VT_ROWS, 1), np.float32)
    vone_np[HEAD_V::V_SLAB, 0] = 1.0
    vone = jnp.asarray(vone_np)

    ffn1 = (ffn1_norm[:, None, :],) + _ffn_weights(ffn1_w13, ffn1_w2)
    ffn2 = (ffn2_norm[:, None, :],) + _ffn_weights(ffn2_w13, ffn2_w2)
    w, wt = _in_proj_weights(w_in)
    wo = w_out.astype(BF16)
    groups32 = QK_WIDTH // DIFF_QK_DIM
    kg = jnp.stack([jnp.tile(diff_k_norm, (1, groups32)), jnp.tile(fox_k_norm, (1, N_HEADS))], axis=1)
    qgt = jnp.stack([jnp.tile(diff_q_norm, (1, groups32)) * (LOG2E * DIFF_QK_DIM ** -0.5),
                     jnp.tile(fox_q_norm, (1, N_HEADS)) * (LOG2E * HEAD_V ** -0.5)], axis=2)
    aw2 = jnp.pad(gla_alpha_w2, ((0, 0), (0, LANES - GLA_GATE_RANK), (0, 0)))
    ab = gla_alpha_b[:, None, :]
    fb = jnp.pad(fox_f_bias, ((0, 0), (FOX_GATE_COL, LANES - FOX_GATE_COL - N_HEADS)))[:, None, :]
    mix_gain = mix_norm[:, None, :]
    diff_og = diff_out_norm[:, :, None]
    gla_og = gla_out_norm[:, None, :]

    for i in range(DEPTH):
        x2d = _ffn(x2d, i, *ffn1)
        (dqt, dk, dvt, gq, gk, gv, gr, la, fqt, fk, fvt, lf) = _proj(
            x2d, i, mix_gain, w, wt, kg, qgt, grp32, grp64, aw2, ab, fb, vone)
        lam_init = 0.8 - 0.6 * math.exp(-0.3 * i)
        at, ct = _softmax_mixers(dqt, dk, dvt, fqt, fk, fvt, lf, i, diff_lambda, diff_og, lam_init, batch, seq)
        g = _gla(gq, gk, gv, gr, la, i, gla_og, batch, seq)
        x2d = _mix_ffn(x2d, i, at, g, ct, wo, *ffn2)
    return x2d.reshape(batch, seq, D_MODEL)
```

```python
import functools
import math

import numpy as np
import jax
import jax.numpy as jnp
from jax import lax
from jax.experimental import pallas as pl
from jax.experimental.pallas import tpu as pltpu

F32 = jnp.float32
BF16 = jnp.bfloat16

D_MODEL = 1024
D_FF = 2752
DEPTH = 4
CHUNK = 64
N_HEADS = 4
DIFF_QK_DIM = 32
HEAD_V = 64
GLA_K_DIM = 64
GLA_V_DIM = 128
GLA_GATE_RANK = 16
GLA_TAU = 16.0
NORM_EPS = 1e-6
IN_WIDTH = 3092
DIFF_WIDTH = N_HEADS * HEAD_V
GLA_WIDTH = N_HEADS * GLA_V_DIM

LANES = 128
FF_CHUNK = 256
TOKEN_TILE = 512
FFN_TILE = 1024
ATTN_TILE = 256
GLA_TILE = 256
GLA_SEQS = 8
NEG = -1e30
LOG2E = math.log2(math.e)
VMEM_LIMIT = 56 * 1024 * 1024

QK_WIDTH = 256
V_SLAB = 80
VT_ROWS = N_HEADS * V_SLAB

C_DK, C_GQ, C_GK, C_GV, C_GR, C_FK, C_GATES = 0, 256, 512, 768, 1280, 1792, 2048
PROJ_WIDTH = 2176
FOX_GATE_COL = GLA_GATE_RANK
R_DQ, R_DV, R_FQ, R_FV = 0, 256, 576, 832
PROJ_T_ROWS = 1152

NT_DIMS = (((1,), (1,)), ((), ()))
TN_DIMS = (((0,), (0,)), ((), ()))


def _cparams(*sem):
    return pltpu.CompilerParams(dimension_semantics=sem, vmem_limit_bytes=VMEM_LIMIT)


def _layer_spec(stack, layer, **kwargs):
    zeros = (0,) * (stack.ndim - 1)
    return pl.BlockSpec((None,) + stack.shape[1:], lambda *_: (layer,) + zeros, **kwargs)


def _dot(a, b):
    return jnp.dot(a, b, preferred_element_type=F32)


def _dot_nt(a, b):
    return lax.dot_general(a, b, NT_DIMS, preferred_element_type=F32)


def _dot_tn(a, b):
    return lax.dot_general(a, b, TN_DIMS, preferred_element_type=F32)


def _split2(x):
    hi = x.astype(BF16)
    lo = (x - hi.astype(F32)).astype(BF16)
    return hi, lo


def _split3(x):
    hi = x.astype(BF16)
    r = x - hi.astype(F32)
    mid = r.astype(BF16)
    lo = (r - mid.astype(F32)).astype(BF16)
    return hi, mid, lo


def _log_sigmoid(z):
    return jnp.minimum(z, 0.0) - jnp.log(1.0 + jnp.exp(-jnp.abs(z)))


def _rms(x, g):
    ms = jnp.mean(x * x, axis=-1, keepdims=True)
    return x * lax.rsqrt(ms + NORM_EPS) * g


def _swiglu_residual(x, g_ref, wg_ref, wu_ref, w2_ref):
    h = _rms(x, g_ref[...]).astype(BF16)
    acc = None
    for lo in range(0, D_FF, FF_CHUNK):
        cols = slice(lo, min(lo + FF_CHUNK, D_FF))
        gate = _dot(h, wg_ref[:, cols])
        up = _dot(h, wu_ref[:, cols])
        act = (gate * jax.nn.sigmoid(gate) * up).astype(BF16)
        term = _dot(act, w2_ref[cols, :])
        acc = term if acc is None else acc + term
    return x + 0.5 * acc


def _ffn_kernel(x_ref, g_ref, wg_ref, wu_ref, w2_ref, o_ref):
    o_ref[...] = _swiglu_residual(x_ref[...], g_ref, wg_ref, wu_ref, w2_ref)


def _mix_ffn_kernel(x_ref, at_ref, gl_ref, ct_ref, wo_ref, g_ref, wg_ref, wu_ref, w2_ref, o_ref):
    mix = _dot_tn(at_ref[...], wo_ref[0:DIFF_WIDTH, :])
    mix = mix + _dot(gl_ref[...], wo_ref[DIFF_WIDTH:DIFF_WIDTH + GLA_WIDTH, :])
    mix = mix + _dot_tn(ct_ref[...], wo_ref[DIFF_WIDTH + GLA_WIDTH:, :])
    o_ref[...] = _swiglu_residual(x_ref[...] + mix, g_ref, wg_ref, wu_ref, w2_ref)


ONCE = dict(pipeline_mode=pl.Buffered(1))


def _ffn_specs(layer, gain, wg, wu, w2):
    return [_layer_spec(gain, layer), _layer_spec(wg, layer, **ONCE), _layer_spec(wu, layer, **ONCE),
            _layer_spec(w2, layer, **ONCE)]


def _ffn(x2d, layer, gain, wg, wu, w2):
    n = x2d.shape[0]
    tm = min(FFN_TILE, n)
    row = lambda i: (i, 0)
    return pl.pallas_call(
        _ffn_kernel,
        grid=(n // tm,),
        in_specs=[pl.BlockSpec((tm, D_MODEL), row)] + _ffn_specs(layer, gain, wg, wu, w2),
        out_specs=pl.BlockSpec((tm, D_MODEL), row),
        out_shape=jax.ShapeDtypeStruct((n, D_MODEL), F32),
        compiler_params=_cparams("parallel"),
        name="ffn",
    )(x2d, gain, wg, wu, w2)


def _mix_ffn(x2d, layer, at, gl, ct, wo, gain, wg, wu, w2):
    n = x2d.shape[0]
    tm = min(FFN_TILE, n)
    row = lambda i: (i, 0)
    colblk = lambda i: (0, i)
    return pl.pallas_call(
        _mix_ffn_kernel,
        grid=(n // tm,),
        in_specs=[pl.BlockSpec((tm, D_MODEL), row), pl.BlockSpec((DIFF_WIDTH, tm), colblk),
                  pl.BlockSpec((tm, GLA_WIDTH), row), pl.BlockSpec((DIFF_WIDTH, tm), colblk),
                  _layer_spec(wo, layer, **ONCE)] + _ffn_specs(layer, gain, wg, wu, w2),
        out_specs=pl.BlockSpec((tm, D_MODEL), row),
        out_shape=jax.ShapeDtypeStruct((n, D_MODEL), F32),
        compiler_params=_cparams("parallel"),
        name="mix_ffn",
    )(x2d, at, gl, ct, wo, gain, wg, wu, w2)


def _group_norm_lanes(t, grp, inv_d, gain):
    hi, lo = _split2(t * t)
    ms = (_dot(hi, grp) + _dot(lo, grp)) * inv_d
    return t * lax.rsqrt(ms + NORM_EPS) * gain


def _group_norm_rows(t, d, gain):
    rows, cols = t.shape
    t3 = t.reshape(rows // d, d, cols)
    ms = jnp.sum(t3 * t3, axis=1, keepdims=True) * (1.0 / d)
    return (t3 * lax.rsqrt(ms + NORM_EPS)).reshape(rows, cols) * gain


def _proj_kernel(x_ref, g_ref, w_ref, wt_ref, kg_ref, qgt_ref, grp32_ref, grp64_ref, aw2_ref,
                 ab_ref, fb_ref, vone_ref,
                 dqt_ref, dk_ref, dvt_ref, gq_ref, gk_ref, gv_ref, gr_ref, la_ref,
                 fqt_ref, fk_ref, fvt_ref, lf_ref):
    h = _rms(x_ref[...], g_ref[...]).astype(BF16)

    p_tok = _dot(h, w_ref[...])
    p_feat = _dot_nt(wt_ref[...], h)

    def proj(lo, width):
        return p_tok[:, lo:lo + width]

    def proj_t(lo, rows):
        return p_feat[lo:lo + rows, :]

    vone = vone_ref[...]
    dqt_ref[...] = _group_norm_rows(proj_t(R_DQ, QK_WIDTH), DIFF_QK_DIM, qgt_ref[:, 0:1]).astype(BF16)
    dk_ref[...] = _group_norm_lanes(proj(C_DK, QK_WIDTH), grp32_ref[...], 1.0 / DIFF_QK_DIM,
                                    kg_ref[0:1]).astype(BF16)
    dvt_ref[...] = (proj_t(R_DV, VT_ROWS) + vone).astype(BF16)
    gq_ref[...] = proj(C_GQ, 256) * (GLA_K_DIM ** -0.5)
    gk_ref[...] = proj(C_GK, 256)
    gv_ref[...] = proj(C_GV, 512).astype(BF16)
    gr_ref[...] = proj(C_GR, 512)
    fqt_ref[...] = _group_norm_rows(proj_t(R_FQ, QK_WIDTH), HEAD_V, qgt_ref[:, 1:2]).astype(BF16)
    fk_ref[...] = _group_norm_lanes(proj(C_FK, QK_WIDTH), grp64_ref[...], 1.0 / HEAD_V,
                                    kg_ref[1:2]).astype(BF16)
    fvt_ref[...] = (proj_t(R_FV, VT_ROWS) + vone).astype(BF16)

    gates = proj(C_GATES, LANES)
    ga_hi, ga_lo = _split2(gates)
    w_hi, w_lo = _split2(aw2_ref[...])
    z = _dot(ga_hi, w_hi) + _dot(ga_hi, w_lo) + _dot(ga_lo, w_hi) + ab_ref[...]
    la_ref[...] = _log_sigmoid(z) * (1.0 / GLA_TAU)
    lf_ref[...] = _log_sigmoid(gates + fb_ref[...])


def _proj(x2d, layer, gain, w, wt, kg, qgt, grp32, grp64, aw2, ab, fb, vone):
    n = x2d.shape[0]
    tm = min(TOKEN_TILE, n)
    row = lambda i: (i, 0)
    colblk = lambda i: (0, i)
    const = lambda i: (0, 0)

    def full(a):
        return pl.BlockSpec(a.shape, const)

    def per_layer(a):
        return _layer_spec(a, layer)

    def tok(width, dt):
        return jax.ShapeDtypeStruct((n, width), dt), pl.BlockSpec((tm, width), row)

    def feat(rows, dt):
        return jax.ShapeDtypeStruct((rows, n), dt), pl.BlockSpec((rows, tm), colblk)

    outs = [feat(QK_WIDTH, BF16), tok(QK_WIDTH, BF16), feat(VT_ROWS, BF16), tok(256, F32), tok(256, F32),
            tok(512, BF16), tok(512, F32), tok(256, F32), feat(QK_WIDTH, BF16), tok(QK_WIDTH, BF16),
            feat(VT_ROWS, BF16), tok(LANES, F32)]
    return pl.pallas_call(
        _proj_kernel,
        grid=(n // tm,),
        in_specs=[pl.BlockSpec((tm, D_MODEL), row), per_layer(gain), per_layer(w), per_layer(wt),
                  per_layer(kg), per_layer(qgt), full(grp32), full(grp64), per_layer(aw2), per_layer(ab),
                  per_layer(fb), full(vone)],
        out_specs=[o[1] for o in outs],
        out_shape=[o[0] for o in outs],
        compiler_params=_cparams("parallel"),
        name="proj_in",
    )(x2d, gain, w, wt, kg, qgt, grp32, grp64, aw2, ab, fb, vone)


def _max_over_rows(u):
    rows = u.shape[0]
    while rows > 8:
        rows //= 2
        u = jnp.maximum(u[:rows], u[rows:])
    return jnp.max(u, axis=0, keepdims=True)


def _attn_pipeline(n_blocks, n_stack, group, scores_of, shift_of, vt_of, u_ref, umax_ref, m_ref, acc_ref):
    t = u_ref.shape[-1]
    n_groups = n_stack // group
    n_slots = u_ref.shape[0]
    assert n_slots == max(n_groups, 2)
    m_ref[...] = jnp.full_like(m_ref, NEG)
    acc_ref[...] = jnp.zeros_like(acc_ref)

    def slot(j, p):
        return p if n_slots == n_groups else (j * n_groups + p) % n_slots

    def issue(j, p, dst, last):
        u = scores_of(j, p, last)
        for g in range(group):
            u_ref[dst, g] = u[:, g * t:(g + 1) * t]
        umax_ref[dst] = _max_over_rows(u)

    issue(0, 0, slot(0, 0), None)

    def body(j, carry, last):
        for p in range(n_groups):
            if p + 1 < n_groups:
                issue(j, p + 1, slot(j, p + 1), last)
            elif not last:
                issue(j + 1, 0, slot(j + 1, 0), None)
            src = slot(j, p)
            for e in range(group * p, group * (p + 1)):
                lanes = slice((e % group) * t, (e % group + 1) * t)
                u = u_ref[src, e % group]
                c = shift_of(j, e)
                m_old = m_ref[e]
                m_new = jnp.maximum(m_old, umax_ref[src, :, lanes] + c)
                alpha = jnp.exp2(m_old - m_new)
                p_t = jnp.exp2(u - (m_new - c)).astype(BF16)
                acc_ref[e] = alpha * acc_ref[e] + _dot(vt_of(j, e), p_t)
                m_ref[e] = m_new
        return carry

    def run(first, count, width):
        def several(jj, carry):
            for w in range(width):
                carry = body(first + width * jj + w, carry, False)
            return carry
        lax.fori_loop(0, count, several, 0)

    past = n_blocks - 1
    fours = past // 4
    pairs = (past % 4) // 2
    run(0, fours, 4)
    run(4 * fours, pairs, 2)
    run(4 * fours + 2 * pairs, past % 2, 1)
    body(past, 0, True)


def _stack_queries(qt_ref, qst_ref, n_stack, group, clear):
    depth, t = qt_ref.shape[0] // n_stack, qt_ref.shape[1]

    @pl.when(clear)
    def _():
        qst_ref[...] = jnp.zeros_like(qst_ref)

    for e in range(n_stack):
        rows = slice(e * depth, (e + 1) * depth)
        qst_ref[e // group, rows, (e % group) * t:(e % group + 1) * t] = qt_ref[rows, :]


def _softmax_out(acc):
    return acc[0:HEAD_V] / acc[HEAD_V:HEAD_V + 1]


DIFF_SLOPES = tuple(2.0 ** (-8.0 * (h + 1) / N_HEADS) for h in range(N_HEADS))


def _diff_tables(t):
    sl = np.arange(t)[:, None]
    tl = np.arange(t)[None, :]
    allowed = (sl // CHUNK) <= (tl // CHUNK)
    past = [LOG2E * slope * np.broadcast_to(sl, (t, t)) for slope in DIFF_SLOPES]
    diag = [np.where(allowed, LOG2E * slope * (tl - np.abs(tl - sl)), NEG) for slope in DIFF_SLOPES]
    return np.stack(past + diag).astype(np.float32)


SM_GROUP = 4
N_DIFF_COPIES = 2 * N_HEADS
N_SM_COPIES = N_DIFF_COPIES + N_HEADS


def _softmax_mixers_kernel(lam_init, dqt_ref, dk_ref, dvt_ref, fqt_ref, fk_ref, fvt_ref, lf_ref,
                           dbias_ref, fmask_ref, tril_ref, lp_ref, og_ref, a_ref, c_ref,
                           dqst_ref, fqst_ref, u_ref, umax_ref, m_ref, acc_ref, fneg_ref):
    t = dqt_ref.shape[1]
    seq = dk_ref.shape[0]
    i = pl.program_id(1)
    n_diff_groups = N_DIFF_COPIES // SM_GROUP

    @pl.when(i == 0)
    def _():
        tril = tril_ref[...]
        for blk in range(seq // LANES):
            rows = slice(blk * LANES, (blk + 1) * LANES)
            x = jnp.concatenate(
                [jnp.broadcast_to(lf_ref[rows, FOX_GATE_COL + h:FOX_GATE_COL + h + 1], (LANES, LANES))
                 for h in range(N_HEADS)], axis=1)
            hi, mid, lo = _split3(x)
            f = _dot(tril, hi) + _dot(tril, mid) + _dot(tril, lo)
            for h in range(N_HEADS):
                fneg_ref[h, rows, :] = f[:, h * LANES:(h + 1) * LANES]
        for h in range(N_HEADS):
            carry = jnp.zeros((1, LANES), F32)
            for blk in range(seq // LANES):
                rows = slice(blk * LANES, (blk + 1) * LANES)
                f = fneg_ref[h, rows, :] + carry
                fneg_ref[h, rows, :] = f * (-LOG2E)
                carry = f[LANES - 1:LANES, :]

    _stack_queries(dqt_ref, dqst_ref, N_DIFF_COPIES, SM_GROUP, i == 0)
    _stack_queries(fqt_ref, fqst_ref, N_HEADS, SM_GROUP, i == 0)

    def keys(j):
        return pl.ds(pl.multiple_of(j * t, t), t)

    def scores(j, p, last):
        if p < n_diff_groups:
            diag_tiles = jnp.where(j == i, N_HEADS, 0) if last is None else (N_HEADS if last else 0)
            first = diag_tiles + p * (SM_GROUP // 2)
            bias = []
            for h in range(SM_GROUP // 2):
                bias += [dbias_ref[first + h]] * 2
            return _dot(dk_ref[keys(j), :], dqst_ref[p]) + jnp.concatenate(bias, axis=1)
        bias = [jnp.concatenate([fneg_ref[h, keys(j), :]] * (t // LANES), axis=1) for h in range(N_HEADS)]
        if last:
            bias = [b + fmask_ref[...] for b in bias]
        return _dot(fk_ref[keys(j), :], fqst_ref[0]) + jnp.concatenate(bias, axis=1)

    def shift(j, e):
        if e < N_DIFF_COPIES:
            return (LOG2E * DIFF_SLOPES[e // 2]) * ((j - i) * t).astype(F32)
        return 0.0

    def vt(j, e):
        if e < N_DIFF_COPIES:
            return dvt_ref[(e // 2) * V_SLAB:(e // 2 + 1) * V_SLAB, keys(j)]
        h = e - N_DIFF_COPIES
        return fvt_ref[h * V_SLAB:(h + 1) * V_SLAB, keys(j)]

    _attn_pipeline(i + 1, N_SM_COPIES, SM_GROUP, scores, shift, vt, u_ref, umax_ref, m_ref, acc_ref)

    lp = lp_ref[...]
    lam = (jnp.exp(jnp.sum(lp[0:1] * lp[1:2], keepdims=True))
           - jnp.exp(jnp.sum(lp[2:3] * lp[3:4], keepdims=True)) + lam_init)
    for h in range(N_HEADS):
        rows = slice(h * HEAD_V, (h + 1) * HEAD_V)
        a = _softmax_out(acc_ref[2 * h]) - lam * _softmax_out(acc_ref[2 * h + 1])
        ms = jnp.sum(a * a, axis=0, keepdims=True) * (1.0 / HEAD_V)
        y = a * lax.rsqrt(ms + NORM_EPS) * og_ref[...] * (1.0 - lam_init)
        a_ref[rows, :] = y.astype(BF16)
        c_ref[rows, :] = _softmax_out(acc_ref[N_DIFF_COPIES + h]).astype(BF16)


def _softmax_mixers(dqt, dk, dvt, fqt, fk, fvt, lf, layer, lp, og, lam_init, batch, seq):
    t = min(ATTN_TILE, seq)
    nq = seq // t
    n = batch * seq
    dbias = jnp.asarray(_diff_tables(t))
    tril = jnp.asarray(np.tril(np.ones((LANES, LANES), np.float32)), BF16)
    pos = np.arange(t)
    fmask = jnp.asarray(np.where(pos[:, None] <= pos[None, :], 0.0, NEG).astype(np.float32))
    n_groups = N_SM_COPIES // SM_GROUP
    q_blk = pl.BlockSpec((QK_WIDTH, t), lambda b, i: (0, b * nq + i))
    k_blk = pl.BlockSpec((seq, QK_WIDTH), lambda b, i: (b, 0))
    v_blk = pl.BlockSpec((VT_ROWS, seq), lambda b, i: (0, b))
    out_blk = pl.BlockSpec((DIFF_WIDTH, t), lambda b, i: (0, b * nq + i))

    def const(x):
        return pl.BlockSpec(x.shape, lambda b, i: (0,) * x.ndim)

    return pl.pallas_call(
        functools.partial(_softmax_mixers_kernel, lam_init),
        grid=(batch, nq),
        in_specs=[q_blk, k_blk, v_blk, q_blk, k_blk, v_blk,
                  pl.BlockSpec((seq, LANES), lambda b, i: (b, 0)),
                  const(dbias), const(fmask), const(tril), _layer_spec(lp, layer), _layer_spec(og, layer)],
        out_specs=[out_blk, out_blk],
        out_shape=[jax.ShapeDtypeStruct((DIFF_WIDTH, n), BF16)] * 2,
        scratch_shapes=[
            pltpu.VMEM((N_DIFF_COPIES // SM_GROUP, QK_WIDTH, SM_GROUP * t), BF16),
            pltpu.VMEM((N_HEADS // SM_GROUP, QK_WIDTH, SM_GROUP * t), BF16),
            pltpu.VMEM((n_groups, SM_GROUP, t, t), F32),
            pltpu.VMEM((n_groups, 1, SM_GROUP * t), F32),
            pltpu.VMEM((N_SM_COPIES, 1, t), F32),
            pltpu.VMEM((N_SM_COPIES, V_SLAB, t), F32),
            pltpu.VMEM((N_HEADS, seq, LANES), F32),
        ],
        compiler_params=_cparams("parallel", "arbitrary"),
        name="softmax_mixers",
    )(dqt, dk, dvt, fqt, fk, fvt, lf, dbias, fmask, tril, lp, og)


GLA_LEVELS = (1, 2, 4, 8, 16, 32)
N_LEVELS = len(GLA_LEVELS)


def _gla_tables():
    c = CHUNK
    idx = np.arange(c)
    pair = np.zeros((N_LEVELS, c, c), np.float32)
    for l, m in enumerate(GLA_LEVELS):
        upper = idx % (2 * m) >= m
        same = (idx[:, None] // (2 * m)) == (idx[None, :] // (2 * m))
        pair[l] = same & upper[:, None] & (~upper)[None, :]
    pair = np.tile(pair, (1, N_HEADS, 1))
    tril = np.tril(np.ones((c, c), np.float32))
    return tril, pair


def _block_ref(b, m):
    c, w = b.shape
    if 2 * m >= 8:
        b3 = b.reshape(c // (2 * m), 2 * m, w)
        return jnp.broadcast_to(b3[:, m - 1:m, :], b3.shape).reshape(c, w)
    b3 = b.reshape(c // 8, 8, w)
    sub = lax.broadcasted_iota(jnp.int32, (1, 8, 1), 1)
    ref = jnp.broadcast_to(b3[:, m - 1:m, :], b3.shape)
    for start in range(2 * m, 8, 2 * m):
        pick = jnp.broadcast_to(b3[:, start + m - 1:start + m, :], b3.shape)
        ref = jnp.where(sub >= start, pick, ref)
    return ref.reshape(c, w)


def _gla_kernel(q_ref, k_ref, v_ref, r_ref, a_ref, tril_ref, pair_ref, og_ref, o_ref, state_ref):
    c = CHUNK
    hk = N_HEADS * GLA_K_DIM
    seqs = range(q_ref.shape[0])

    @pl.when(pl.program_id(1) == 0)
    def _():
        state_ref[...] = jnp.zeros_like(state_ref)

    col = lax.broadcasted_iota(jnp.int32, (1, hk), 1)
    head_cols = [col // GLA_K_DIM == h for h in range(N_HEADS)]

    def head_only(x, h):
        return jnp.where(head_cols[h], x, jnp.zeros_like(x))

    def stack_heads(x):
        x = x.astype(BF16)
        return jnp.concatenate([head_only(x, h) for h in range(N_HEADS)], axis=0)

    def chunk(ci, carry):
        r0 = pl.multiple_of(ci * c, c)
        rows = pl.ds(r0, c)
        tril = tril_ref[...]
        q = [q_ref[s, rows, :] for s in seqs]
        k = [k_ref[s, rows, :] for s in seqs]
        v = [v_ref[s, rows, :] for s in seqs]
        b = []
        for s in seqs:
            a_hi, a_mid, a_lo = _split3(a_ref[s, rows, :])
            b.append(_dot(tril, a_hi) + _dot(tril, a_mid) + _dot(tril, a_lo))
        b_last = [b[s][c - 1:c, :] for s in seqs]

        attn = [None for s in seqs]
        for l, m in enumerate(GLA_LEVELS):
            for s in seqs:
                ref = _block_ref(b[s], m)
                ql = q[s] * jnp.exp(jnp.minimum(b[s] - ref, 0.0))
                kl = k[s] * jnp.exp(jnp.minimum(ref - b[s], 0.0))
                term = pair_ref[l] * _dot_nt(stack_heads(ql), kl.astype(BF16))
                attn[s] = term if attn[s] is None else attn[s] + term

        for s in seqs:
            attn_s = attn[s].astype(BF16)
            state_t = state_ref[s]
            o_inter = _dot_nt(stack_heads(q[s] * jnp.exp(b[s])), state_t.astype(BF16))
            k_dec = (k[s] * jnp.exp(b_last[s] - b[s])).astype(BF16)
            qk = q[s] * k[s]
            upd = None
            for h in range(N_HEADS):
                hr = slice(h * c, (h + 1) * c)
                hv = slice(h * GLA_V_DIM, (h + 1) * GLA_V_DIM)
                own = jnp.sum(head_only(qk, h), axis=-1, keepdims=True)
                o = o_inter[hr] + _dot(attn_s[hr], v[s][:, hv]) + own * v[s][:, hv].astype(F32)
                y = _rms(o, og_ref[...])
                gate = r_ref[s, rows, hv]
                o_ref[s, rows, hv] = (y * (gate * jax.nn.sigmoid(gate))).astype(BF16)
                term = _dot_tn(v[s][:, hv], head_only(k_dec, h))
                upd = term if upd is None else upd + term
            state_ref[s] = state_t * jnp.exp(b_last[s]) + upd
        return carry

    lax.fori_loop(0, q_ref.shape[1] // c, chunk, 0)


def _gla(gq, gk, gv, gr, la, layer, og, batch, seq):
    tg = min(GLA_TILE, seq)
    ng = seq // tg
    together = GLA_SEQS if batch % GLA_SEQS == 0 else 1
    tril_np, pair_np = _gla_tables()
    tril = jnp.asarray(tril_np, BF16)
    pair = jnp.asarray(pair_np)
    hk = N_HEADS * GLA_K_DIM
    hv = N_HEADS * GLA_V_DIM

    def seq_major(x):
        return x.reshape(batch // together, together, seq, x.shape[-1])

    def blk(width):
        return pl.BlockSpec((None, together, tg, width), lambda b, i: (b, 0, i, 0))

    out = pl.pallas_call(
        _gla_kernel,
        grid=(batch // together, ng),
        in_specs=[blk(hk), blk(hk), blk(hv), blk(hv), blk(hk),
                  pl.BlockSpec(tril.shape, lambda b, i: (0, 0)),
                  pl.BlockSpec(pair.shape, lambda b, i: (0, 0, 0)),
                  _layer_spec(og, layer)],
        out_specs=blk(hv),
        out_shape=jax.ShapeDtypeStruct((batch // together, together, seq, hv), BF16),
        scratch_shapes=[pltpu.VMEM((together, GLA_V_DIM, hk), F32)],
        compiler_params=_cparams("parallel", "arbitrary"),
        name="gla",
    )(seq_major(gq), seq_major(gk), seq_major(gv), seq_major(gr), seq_major(la), tril, pair, og)
    return out.reshape(batch * seq, hv)


def _in_proj_weights(w_in):
    w_in = w_in.astype(BF16)

    def cols(lo, hi):
        return w_in[..., lo:hi]

    def zeros(width):
        return jnp.zeros(w_in.shape[:-1] + (width,), w_in.dtype)

    w = jnp.concatenate([cols(256, 512), cols(768, 2304), cols(2576, 2832), cols(2304, 2320),
                         cols(IN_WIDTH - N_HEADS, IN_WIDTH), zeros(LANES - GLA_GATE_RANK - N_HEADS)], axis=-1)

    def value_slabs(lo):
        v = cols(lo, lo + DIFF_WIDTH)
        v = v.reshape(v.shape[:-1] + (N_HEADS, HEAD_V))
        v = jnp.pad(v, [(0, 0)] * (v.ndim - 1) + [(0, V_SLAB - HEAD_V)])
        return v.reshape(v.shape[:-2] + (VT_ROWS,))

    wt = jnp.concatenate([cols(0, 256), value_slabs(512), cols(2320, 2576), value_slabs(2832)], axis=-1)
    assert w.shape[-1] == PROJ_WIDTH and wt.shape[-1] == PROJ_T_ROWS
    return w, jnp.swapaxes(wt, -1, -2)


def _ffn_weights(w13, w2):
    return w13[..., :D_FF].astype(BF16), w13[..., D_FF:].astype(BF16), w2.astype(BF16)


def _same_group(width):
    g = np.arange(QK_WIDTH) // width
    return jnp.asarray((g[:, None] == g[None, :]).astype(np.float32), BF16)


def kernel(x, ffn1_norm, ffn1_w13, ffn1_w2, mix_norm, w_in, w_out, diff_q_norm, diff_k_norm,
           diff_lambda, diff_out_norm, gla_alpha_w2, gla_alpha_b, gla_out_norm, fox_q_norm,
           fox_k_norm, fox_f_bias, ffn2_norm, ffn2_w13, ffn2_w2):
    batch, seq, _ = x.shape
    x2d = x.reshape(batch * seq, D_MODEL)
    grp32 = _same_group(DIFF_QK_DIM)
    grp64 = _same_group(HEAD_V)
    vone_np = np.zeros((VT_ROWS, 1), np.float32)
    vone_np[HEAD_V::V_SLAB, 0] = 1.0
    vone = jnp.asarray(vone_np)

    ffn1 = (ffn1_norm[:, None, :],) + _ffn_weights(ffn1_w13, ffn1_w2)
    ffn2 = (ffn2_norm[:, None, :],) + _ffn_weights(ffn2_w13, ffn2_w2)
    w, wt = _in_proj_weights(w_in)
    wo = w_out.astype(BF16)
    groups32 = QK_WIDTH // DIFF_QK_DIM
    kg = jnp.stack([jnp.tile(diff_k_norm, (1, groups32)), jnp.tile(fox_k_norm, (1, N_HEADS))], axis=1)
    qgt = jnp.stack([jnp.tile(diff_q_norm, (1, groups32)) * (LOG2E * DIFF_QK_DIM ** -0.5),
                     jnp.tile(fox_q_norm, (1, N_HEADS)) * (LOG2E * HEAD_V ** -0.5)], axis=2)
    aw2 = jnp.pad(gla_alpha_w2, ((0, 0), (0, LANES - GLA_GATE_RANK), (0, 0)))
    ab = gla_alpha_b[:, None, :]
    fb = jnp.pad(fox_f_bias, ((0, 0), (FOX_GATE_COL, LANES - FOX_GATE_COL - N_HEADS)))[:, None, :]
    mix_gain = mix_norm[:, None, :]
    diff_og = diff_out_norm[:, :, None]
    gla_og = gla_out_norm[:, None, :]

    for i in range(DEPTH):
        x2d = _ffn(x2d, i, *ffn1)
        (dqt, dk, dvt, gq, gk, gv, gr, la, fqt, fk, fvt, lf) = _proj(
            x2d, i, mix_gain, w, wt, kg, qgt, grp32, grp64, aw2, ab, fb, vone)
        lam_init = 0.8 - 0.6 * math.exp(-0.3 * i)
        at, ct = _softmax_mixers(dqt, dk, dvt, fqt, fk, fvt, lf, i, diff_lambda, diff_og, lam_init, batch, seq)
        g = _gla(gq, gk, gv, gr, la, i, gla_og, batch, seq)
        x2d = _mix_ffn(x2d, i, at, g, ct, wo, *ffn2)
    return x2d.reshape(batch, seq, D_MODEL)
```
